```python
import math, functools
import jax, jax.numpy as jnp
from jax import lax
import numpy as np

D_MODEL = 2048
BATCH = 2
SEQ = 4096
DEPTH = 1
DEC_BATCH = 128
DEC_SEQ = 8
PAST_LEN = 8192
PAGE_SIZE = 128

SSM_WIDTH = D_MODEL // 2
SSM_CH = 16
SSM_GROUPS = SSM_WIDTH // SSM_CH
SSM_STATE = 64
DT_MIN = 0.001
DT_MAX = 0.1
N_HEADS = 8
NOPE_DIM = 128
ROPE_DIM = 64
V_DIM = 128
ATTN_WIDTH = N_HEADS * V_DIM
Q_LORA = 512
KV_LORA = 512
ROPE_BASE = 10000.0
SCALE = (NOPE_DIM + ROPE_DIM) ** -0.5
Q_BLOCK = 128
IN_WIDTH = SSM_WIDTH + Q_LORA + KV_LORA + ROPE_DIM
D_FF = ((8 * D_MODEL // 3 + 255) // 256) * 256
DN_ALPHA = (2 * DEPTH) ** 0.25
DN_BETA = (8 * DEPTH) ** -0.25

kernel_name = 'hymba_s5_mla_deepnorm_step'


def rms_norm(x, g, eps=1e-6):
    xf = x.astype(jnp.float32)
    y = xf * lax.rsqrt(jnp.mean(xf * xf, axis=-1, keepdims=True) + eps)
    return (y * g.astype(jnp.float32)).astype(x.dtype)


def layer_norm(x, g, b, eps=1e-5):
    xf = x.astype(jnp.float32)
    mu = jnp.mean(xf, axis=-1, keepdims=True)
    var = jnp.mean(jnp.square(xf - mu), axis=-1, keepdims=True)
    y = (xf - mu) * lax.rsqrt(var + eps)
    return (y * g.astype(jnp.float32) + b.astype(jnp.float32)).astype(x.dtype)


def apply_rope(x, pos):
    half = ROPE_DIM // 2
    inv = ROPE_BASE ** (-2.0 * jnp.arange(half, dtype=jnp.float32) / ROPE_DIM)
    ang = pos.astype(jnp.float32)[:, None] * inv[None, :]
    cos = jnp.cos(ang)[None, :, None, :]
    sin = jnp.sin(ang)[None, :, None, :]
    x1 = x[..., :half].astype(jnp.float32)
    x2 = x[..., half:].astype(jnp.float32)
    return jnp.concatenate([x1 * cos - x2 * sin, x1 * sin + x2 * cos], axis=-1).astype(x.dtype)


def _ssm_combine(e1, e2):
    a1, b1 = e1
    a2, b2 = e2
    return a1 * a2, a2 * b1 + b2


def s5_group(u, h0, p):
    bsz, s, _ = u.shape
    f32 = jnp.float32
    uf = u.astype(f32).reshape(bsz, s, SSM_GROUPS, SSM_CH)
    lam = lax.complex(p['ssm_a_re'].astype(f32), p['ssm_a_im'].astype(f32))
    delta = jnp.exp(p['ssm_log_step'].astype(f32))[:, None]
    a_bar = jnp.exp(lam * delta)
    b_mat = lax.complex(p['ssm_b_re'].astype(f32), p['ssm_b_im'].astype(f32))
    b_bar = ((a_bar - 1.0) / lam)[..., None] * b_mat
    bu = jnp.einsum('bsgh,gph->bsgp', uf.astype(jnp.complex64), b_bar)
    if h0 is not None:
        bu = bu.at[:, 0].add(a_bar * h0)
    a_seq = jnp.broadcast_to(a_bar, bu.shape)
    _, h = lax.associative_scan(_ssm_combine, (a_seq, bu), axis=1)
    c_mat = lax.complex(p['ssm_c_re'].astype(f32), p['ssm_c_im'].astype(f32))
    y = jnp.real(jnp.einsum('bsgp,ghp->bsgh', h, c_mat)) + p['ssm_d'].astype(f32) * uf
    g = jax.nn.gelu(y.reshape(bsz, s, SSM_WIDTH))
    out = g * jax.nn.sigmoid(g @ p['w_glu'].astype(f32) + p['b_glu'].astype(f32))
    return out.astype(u.dtype), h[:, -1]


def _scores(q_lat, q_rope, k_lat, k_rope):
    sc = jnp.einsum('bshr,bkr->bhsk', q_lat, k_lat) + jnp.einsum('bshd,bkd->bhsk', q_rope, k_rope)
    return sc.astype(jnp.float32) * SCALE


def prompt_attention(q_lat, q_rope, ckv, krope):
    bsz, s, h, r = q_lat.shape
    nb = s // Q_BLOCK
    ql = q_lat.reshape(bsz, nb, Q_BLOCK, h, r).transpose(1, 0, 2, 3, 4)
    qr = q_rope.reshape(bsz, nb, Q_BLOCK, h, ROPE_DIM).transpose(1, 0, 2, 3, 4)
    k_pos = jnp.arange(s)
    ckv32 = ckv.astype(jnp.float32)

    def block(args):
        i, qlb, qrb = args
        sc = _scores(qlb, qrb, ckv, krope)
        q_pos = i * Q_BLOCK + jnp.arange(Q_BLOCK)
        sc = jnp.where(k_pos[None, :] <= q_pos[:, None], sc, -jnp.inf)
        pr = jax.nn.softmax(sc, axis=-1)
        return jnp.einsum('bhqk,bkr->bqhr', pr, ckv32).astype(q_lat.dtype)

    o = lax.map(block, (jnp.arange(nb), ql, qr))
    return o.transpose(1, 0, 2, 3, 4).reshape(bsz, s, h, r)


def _online_update(carry, sc, k_lat):
    m, l, acc = carry
    m_new = jnp.maximum(m, jnp.max(sc, axis=-1))
    corr = jnp.exp(m - m_new)
    pr = jnp.exp(sc - m_new[..., None])
    l = l * corr + jnp.sum(pr, axis=-1)
    acc = acc * corr[..., None] + jnp.einsum('bhsk,bkr->bhsr', pr, k_lat.astype(jnp.float32))
    return m_new, l, acc


def sample_attention(q_lat, q_rope, ckv_new, kr_new, cache_ckv, cache_kr, page_table):
    bsz, s, h, r = q_lat.shape
    init = (jnp.full((bsz, h, s), -jnp.inf, jnp.float32),
            jnp.zeros((bsz, h, s), jnp.float32),
            jnp.zeros((bsz, h, s, r), jnp.float32))

    def page_step(carry, pages):
        k_lat = cache_ckv[pages]
        k_rope = cache_kr[pages]
        return _online_update(carry, _scores(q_lat, q_rope, k_lat, k_rope), k_lat), None

    carry, _ = lax.scan(page_step, init, page_table.T)
    causal = jnp.tril(jnp.ones((s, s), dtype=bool))
    sc_new = jnp.where(causal, _scores(q_lat, q_rope, ckv_new, kr_new), -jnp.inf)
    _, l, acc = _online_update(carry, sc_new, ckv_new)
    return (acc / l[..., None]).transpose(0, 2, 1, 3).astype(q_lat.dtype)


def layer_forward(x, pos, h0, attend, p):
    bsz, s, _ = x.shape
    proj = x @ p['w_in']
    u, cq, ckv_raw, kr_raw = jnp.split(
        proj, [SSM_WIDTH, SSM_WIDTH + Q_LORA, SSM_WIDTH + Q_LORA + KV_LORA], axis=-1)
    y_ssm, h_last = s5_group(u, h0, p)
    q = (rms_norm(cq, p['g_q']) @ p['w_uq']).reshape(bsz, s, N_HEADS, NOPE_DIM + ROPE_DIM)
    q_nope = q[..., :NOPE_DIM]
    q_rope = apply_rope(q[..., NOPE_DIM:], pos)
    q_lat = jnp.einsum('bshn,rhn->bshr', q_nope, p['w_uk'])
    ckv = rms_norm(ckv_raw, p['g_kv'])
    krope = apply_rope(kr_raw[:, :, None, :], pos)[:, :, 0, :]
    o_lat = attend(q_lat, q_rope, ckv, krope)
    y_att = jnp.einsum('bshr,rhv->bshv', o_lat, p['w_uv']).reshape(bsz, s, ATTN_WIDTH)
    mix = jnp.concatenate([y_ssm, y_att], axis=-1) @ p['w_out']
    x = layer_norm(DN_ALPHA * x + mix, p['ln1_g'], p['ln1_b'])
    ffn = (jax.nn.silu(x @ p['w_gate']) * (x @ p['w_up'])) @ p['w_down']
    x = layer_norm(DN_ALPHA * x + ffn, p['ln2_g'], p['ln2_b'])
    return x, ckv, krope, h_last


def _normal(k, shape, scale):
    return jax.random.normal(k, shape, jnp.float32) * scale


def setup_inputs(seed: int = 0) -> dict:
    key = jax.random.key(seed)
    ks = jax.random.split(key, 32)
    L = DEPTH
    n_pages = PAST_LEN // PAGE_SIZE
    n_pool = (DEC_BATCH * n_pages * 5) // 4
    page_table = jax.random.permutation(ks[6], n_pool)[:DEC_BATCH * n_pages]
    page_table = page_table.reshape(DEC_BATCH, n_pages).astype(jnp.int32)
    a_im = jnp.pi * jnp.arange(SSM_STATE, dtype=jnp.float32) + _normal(ks[14], (L, SSM_GROUPS, SSM_STATE), 0.01)
    return {
        'x_prompt': _normal(ks[0], (BATCH, SEQ, D_MODEL), 1.0),
        'x_sample': _normal(ks[1], (DEC_BATCH, DEC_SEQ, D_MODEL), 1.0),
        'cache_ckv': _normal(ks[2], (L, n_pool, PAGE_SIZE, KV_LORA), 1.0),
        'cache_krope': _normal(ks[3], (L, n_pool, PAGE_SIZE, ROPE_DIM), 1.0),
        'state_ssm_re': _normal(ks[4], (L, DEC_BATCH, SSM_GROUPS, SSM_STATE), 0.5),
        'state_ssm_im': _normal(ks[5], (L, DEC_BATCH, SSM_GROUPS, SSM_STATE), 0.5),
        'page_table': page_table,
        'w_in': _normal(ks[7], (L, D_MODEL, IN_WIDTH), D_MODEL ** -0.5),
        'g_q': 1.0 + _normal(ks[8], (L, Q_LORA), 0.01),
        'w_uq': _normal(ks[9], (L, Q_LORA, N_HEADS * (NOPE_DIM + ROPE_DIM)), Q_LORA ** -0.5),
        'w_uk': _normal(ks[10], (L, KV_LORA, N_HEADS, NOPE_DIM), KV_LORA ** -0.5),
        'g_kv': 1.0 + _normal(ks[11], (L, KV_LORA), 0.01),
        'w_uv': _normal(ks[12], (L, KV_LORA, N_HEADS, V_DIM), KV_LORA ** -0.5),
        'ssm_a_re': -0.5 + _normal(ks[13], (L, SSM_GROUPS, SSM_STATE), 0.01),
        'ssm_a_im': a_im,
        'ssm_log_step': jax.random.uniform(ks[15], (L, SSM_GROUPS), jnp.float32, math.log(DT_MIN), math.log(DT_MAX)),
        'ssm_b_re': _normal(ks[16], (L, SSM_GROUPS, SSM_STATE, SSM_CH), (2 * SSM_CH) ** -0.5),
        'ssm_b_im': _normal(ks[17], (L, SSM_GROUPS, SSM_STATE, SSM_CH), (2 * SSM_CH) ** -0.5),
        'ssm_c_re': _normal(ks[18], (L, SSM_GROUPS, SSM_CH, SSM_STATE), SSM_STATE ** -0.5),
        'ssm_c_im': _normal(ks[19], (L, SSM_GROUPS, SSM_CH, SSM_STATE), SSM_STATE ** -0.5),
        'ssm_d': _normal(ks[20], (L, SSM_GROUPS, SSM_CH), 1.0),
        'w_glu': _normal(ks[21], (L, SSM_WIDTH, SSM_WIDTH), SSM_WIDTH ** -0.5),
        'b_glu': _normal(ks[22], (L, SSM_WIDTH), 0.01),
        'w_out': _normal(ks[23], (L, D_MODEL, D_MODEL), DN_BETA * D_MODEL ** -0.5),
        'ln1_g': 1.0 + _normal(ks[24], (L, D_MODEL), 0.01),
        'ln1_b': _normal(ks[25], (L, D_MODEL), 0.01),
        'w_gate': _normal(ks[26], (L, D_MODEL, D_FF), D_MODEL ** -0.5),
        'w_up': _normal(ks[27], (L, D_MODEL, D_FF), D_MODEL ** -0.5),
        'w_down': _normal(ks[28], (L, D_FF, D_MODEL), DN_BETA * D_FF ** -0.5),
        'ln2_g': 1.0 + _normal(ks[29], (L, D_MODEL), 0.01),
        'ln2_b': _normal(ks[30], (L, D_MODEL), 0.01),
    }


def reference(x_prompt, x_sample, cache_ckv, cache_krope, state_ssm_re, state_ssm_im, page_table,
              w_in, g_q, w_uq, w_uk, g_kv, w_uv, ssm_a_re, ssm_a_im, ssm_log_step,
              ssm_b_re, ssm_b_im, ssm_c_re, ssm_c_im, ssm_d, w_glu, b_glu, w_out,
              ln1_g, ln1_b, w_gate, w_up, w_down, ln2_g, ln2_b):
    pos_p = jnp.arange(x_prompt.shape[1], dtype=jnp.int32)
    past = page_table.shape[1] * PAGE_SIZE
    pos_s = past + jnp.arange(x_sample.shape[1], dtype=jnp.int32)
    y_p, y_s = x_prompt, x_sample
    ckv_p_l, kr_p_l, re_p_l, im_p_l = [], [], [], []
    ckv_s_l, kr_s_l, re_s_l, im_s_l = [], [], [], []
    for l in range(DEPTH):
        p = {
            'w_in': w_in[l], 'g_q': g_q[l], 'w_uq': w_uq[l], 'w_uk': w_uk[l], 'g_kv': g_kv[l],
            'w_uv': w_uv[l], 'ssm_a_re': ssm_a_re[l], 'ssm_a_im': ssm_a_im[l],
            'ssm_log_step': ssm_log_step[l], 'ssm_b_re': ssm_b_re[l], 'ssm_b_im': ssm_b_im[l],
            'ssm_c_re': ssm_c_re[l], 'ssm_c_im': ssm_c_im[l], 'ssm_d': ssm_d[l],
            'w_glu': w_glu[l], 'b_glu': b_glu[l], 'w_out': w_out[l], 'ln1_g': ln1_g[l],
            'ln1_b': ln1_b[l], 'w_gate': w_gate[l], 'w_up': w_up[l], 'w_down': w_down[l],
            'ln2_g': ln2_g[l], 'ln2_b': ln2_b[l],
        }
        y_p, ckv_p, kr_p, h_p = layer_forward(y_p, pos_p, None, prompt_attention, p)
        attend_s = functools.partial(sample_attention, cache_ckv=cache_ckv[l], cache_kr=cache_krope[l],
                                     page_table=page_table)
        h0 = lax.complex(state_ssm_re[l].astype(jnp.float32), state_ssm_im[l].astype(jnp.float32))
        y_s, ckv_s, kr_s, h_s = layer_forward(y_s, pos_s, h0, attend_s, p)
        ckv_p_l.append(ckv_p); kr_p_l.append(kr_p)
        re_p_l.append(jnp.real(h_p)); im_p_l.append(jnp.imag(h_p))
        ckv_s_l.append(ckv_s); kr_s_l.append(kr_s)
        re_s_l.append(jnp.real(h_s)); im_s_l.append(jnp.imag(h_s))
    return (y_p, y_s,
            jnp.stack(ckv_p_l), jnp.stack(kr_p_l), jnp.stack(re_p_l), jnp.stack(im_p_l),
            jnp.stack(ckv_s_l), jnp.stack(kr_s_l), jnp.stack(re_s_l), jnp.stack(im_s_l))
```

```python
import functools
import math

import jax
import jax.numpy as jnp
from jax import lax
from jax.experimental import pallas as pl
from jax.experimental.pallas import tpu as pltpu

F32 = jnp.float32
BF16 = jnp.bfloat16

D_MODEL = 2048
SSM_WIDTH = 1024
SSM_CH = 16
SSM_GROUPS = 64
SSM_STATE = 64
N_HEADS = 8
NOPE_DIM = 128
ROPE_DIM = 64
V_DIM = 128
Q_LORA = 512
KV_LORA = 512
ROPE_BASE = 10000.0
SCALE = (NOPE_DIM + ROPE_DIM) ** -0.5
PAGE_SIZE = 128
LANES = 128
SUBLANES = 8
SLAB_GROUPS = LANES // SSM_CH
N_SLABS = SSM_GROUPS // SLAB_GROUPS
SLAB_STATE = SLAB_GROUPS * SSM_STATE
VMEM_LIMIT = 56 * 1024 * 1024


def _dot(a, b):
    return jnp.dot(a, b, preferred_element_type=F32)


def _dot_nt(a, b):
    return lax.dot_general(a, b, (((1,), (1,)), ((), ())), preferred_element_type=F32)


def _const_spec(shape):
    n = len(shape)
    return pl.BlockSpec(shape, lambda *_: (0,) * n, pipeline_mode=pl.Buffered(1))


def _params(sem):
    return pltpu.CompilerParams(dimension_semantics=sem, vmem_limit_bytes=VMEM_LIMIT)


def _rms(x, g, eps=1e-6):
    return x * lax.rsqrt(jnp.mean(x * x, axis=-1, keepdims=True) + eps) * g


def _layer_norm(x, g, b, eps=1e-5):
    mu = jnp.mean(x, axis=-1, keepdims=True)
    xc = x - mu
    var = jnp.mean(xc * xc, axis=-1, keepdims=True)
    return xc * lax.rsqrt(var + eps) * g + b


def _inproj_kernel(x_ref, wm_ref, wkr_ref, gq_ref, gkv_ref, cos_ref, sin_ref,
                   wqn_ref, wqr_ref, wqs_ref, wuk_ref,
                   u_ref, ckv_ref, kr_ref, ckvb_ref, krb_ref, ql_ref, qr_ref):
    xb = x_ref[...].astype(BF16)
    cos = cos_ref[...]
    sin = sin_ref[...]
    proj = _dot(xb, wm_ref[...])
    u_ref[...] = proj[:, :SSM_WIDTH]
    cq = proj[:, SSM_WIDTH:SSM_WIDTH + Q_LORA]
    ckv = _rms(proj[:, SSM_WIDTH + Q_LORA:], gkv_ref[...])
    ckv_ref[...] = ckv
    ckvb_ref[...] = ckv.astype(BF16)
    kr2 = _dot(xb, wkr_ref[...])
    krope = kr2[:, :LANES] * cos + kr2[:, LANES:] * sin
    kr_ref[...] = krope[:, :ROPE_DIM]
    krb_ref[...] = krope.astype(BF16)
    cqn = _rms(cq, gq_ref[...]).astype(BF16)
    qn = _dot(cqn, wqn_ref[...])
    qa = _dot(cqn, wqr_ref[...])
    qs = _dot(cqn, wqs_ref[...])
    for h in range(N_HEADS):
        sl = slice(h * LANES, (h + 1) * LANES)
        ql_ref[h] = _dot(qn[:, sl].astype(BF16), wuk_ref[h]).astype(BF16)
        qr_ref[h] = (qa[:, sl] * cos + qs[:, sl] * sin).astype(BF16)


def _inproj(x, cos, sin, wts, tm):
    t = x.shape[0]
    row = lambda w: pl.BlockSpec((tm, w), lambda i: (i, 0))
    hrow = lambda w: pl.BlockSpec((N_HEADS, tm, w), lambda i: (0, i, 0))
    return pl.pallas_call(
        _inproj_kernel,
        grid=(t // tm,),
        in_specs=[row(D_MODEL),
                  _const_spec((D_MODEL, 2048)), _const_spec((D_MODEL, 2 * LANES)),
                  _const_spec((1, Q_LORA)), _const_spec((1, KV_LORA)),
                  row(LANES), row(LANES),
                  _const_spec((Q_LORA, N_HEADS * LANES)), _const_spec((Q_LORA, N_HEADS * LANES)),
                  _const_spec((Q_LORA, N_HEADS * LANES)), _const_spec((N_HEADS, NOPE_DIM, KV_LORA))],
        out_specs=[row(SSM_WIDTH), row(KV_LORA), row(ROPE_DIM), row(KV_LORA), row(LANES),
                   hrow(KV_LORA), hrow(LANES)],
        out_shape=[jax.ShapeDtypeStruct((t, SSM_WIDTH), F32),
                   jax.ShapeDtypeStruct((t, KV_LORA), F32),
                   jax.ShapeDtypeStruct((t, ROPE_DIM), F32),
                   jax.ShapeDtypeStruct((t, KV_LORA), BF16),
                   jax.ShapeDtypeStruct((t, LANES), BF16),
                   jax.ShapeDtypeStruct((N_HEADS, t, KV_LORA), BF16),
                   jax.ShapeDtypeStruct((N_HEADS, t, LANES), BF16)],
        compiler_params=_params(("parallel",)),
        name="inproj",
    )(x, wts["w_main"], wts["w_kr"], wts["g_q"], wts["g_kv"], cos, sin,
      wts["w_qn"], wts["w_qr"], wts["w_qs"], wts["w_ukT"])


def _s5prep_kernel(are_ref, aim_ref, ls_ref, bre_ref, bim_ref,
                   abr_ref, abi_ref, bbr_ref, bbi_ref):
    lr = are_ref[...]
    li = aim_ref[...]
    delta = jnp.exp(ls_ref[...])
    mag = jnp.exp(lr * delta)
    ar = mag * jnp.cos(li * delta)
    ai = mag * jnp.sin(li * delta)
    abr_ref[...] = ar
    abi_ref[...] = ai
    den = lr * lr + li * li
    fr = ((ar - 1.0) * lr + ai * li) / den
    fi = (ai * lr - (ar - 1.0) * li) / den
    br = bre_ref[...]
    bi = bim_ref[...]
    bbr_ref[...] = fr * br - fi * bi
    bbi_ref[...] = fr * bi + fi * br


def _s5prep(a_re, a_im, log_step, bt_re, bt_im):
    shp = jax.ShapeDtypeStruct((SSM_GROUPS * SSM_CH, SSM_STATE), F32)
    return pl.pallas_call(_s5prep_kernel, out_shape=[shp] * 4, name="s5prep")(
        a_re, a_im, log_step, bt_re, bt_im)


def _cmul_add(ar, ai, hr, hi, br, bi):
    return ar * hr - ai * hi + br, ar * hi + ai * hr + bi


def _s5p_kernel(u_ref, wb_ref, wc_ref, a_ref, d_ref, g_ref, hl_ref, bu_scr, e_scr, *, seg_len, chunk):
    n_rows = SUBLANES * seg_len
    half = SLAB_STATE
    wb = wb_ref[...]
    for c in range(n_rows // chunk):
        rows = pl.ds(c * chunk, chunk)
        bu_scr[rows, :] = _dot(u_ref[rows, :].astype(BF16), wb)
    ar = jnp.broadcast_to(a_ref[:, :half], (SUBLANES, half))
    ai = jnp.broadcast_to(a_ref[:, half:], (SUBLANES, half))

    def step(t, carry, store):
        hr, hi = carry
        rows = pl.ds(pl.multiple_of(t * SUBLANES, SUBLANES), SUBLANES)
        hr, hi = _cmul_add(ar, ai, hr, hi, bu_scr[rows, :half], bu_scr[rows, half:])
        if store:
            bu_scr[rows, :half] = hr
            bu_scr[rows, half:] = hi
        return hr, hi

    zero = jnp.zeros((SUBLANES, half), F32)
    er, ei = lax.fori_loop(0, seg_len, functools.partial(step, store=False), (zero, zero), unroll=8)
    pr, pi = ar, ai
    for _ in range(int(math.log2(seg_len))):
        pr, pi = pr * pr - pi * pi, 2.0 * pr * pi
    e_scr[0:SUBLANES, :half] = er
    e_scr[0:SUBLANES, half:] = ei
    cr = jnp.zeros((1, half), F32)
    ci = jnp.zeros((1, half), F32)
    e_scr[SUBLANES:SUBLANES + 1, :half] = cr
    e_scr[SUBLANES:SUBLANES + 1, half:] = ci
    for s in range(SUBLANES - 1):
        cr, ci = _cmul_add(pr[0:1], pi[0:1], cr, ci, e_scr[s:s + 1, :half], e_scr[s:s + 1, half:])
        e_scr[SUBLANES + s + 1:SUBLANES + s + 2, :half] = cr
        e_scr[SUBLANES + s + 1:SUBLANES + s + 2, half:] = ci
    init = (e_scr[SUBLANES:2 * SUBLANES, :half], e_scr[SUBLANES:2 * SUBLANES, half:])
    hr, hi = lax.fori_loop(0, seg_len, functools.partial(step, store=True), init, unroll=8)
    hl_ref[:, :half] = hr
    hl_ref[:, half:] = hi
    wc = wc_ref[...]
    d = d_ref[...]
    for c in range(n_rows // chunk):
        rows = pl.ds(c * chunk, chunk)
        y = _dot(bu_scr[rows, :].astype(BF16), wc) + d * u_ref[rows, :]
        g_ref[rows, :] = jax.nn.gelu(y)


def _s5_prompt(u_perm, s5w):
    bsz, n_rows, _ = u_perm.shape
    seg_len = n_rows // SUBLANES
    kern = functools.partial(_s5p_kernel, seg_len=seg_len, chunk=512)
    slab = lambda shape: pl.BlockSpec((None,) + shape, lambda b, s: (s, 0, 0))
    return pl.pallas_call(
        kern,
        grid=(bsz, N_SLABS),
        in_specs=[pl.BlockSpec((None, n_rows, LANES), lambda b, s: (b, 0, s)),
                  slab((LANES, 2 * SLAB_STATE)), slab((2 * SLAB_STATE, LANES)),
                  slab((1, 2 * SLAB_STATE)), slab((1, LANES))],
        out_specs=[pl.BlockSpec((None, n_rows, LANES), lambda b, s: (b, 0, s)),
                   pl.BlockSpec((None, None, SUBLANES, 2 * SLAB_STATE), lambda b, s: (b, s, 0, 0))],
        out_shape=[jax.ShapeDtypeStruct((bsz, n_rows, SSM_WIDTH), F32),
                   jax.ShapeDtypeStruct((bsz, N_SLABS, SUBLANES, 2 * SLAB_STATE), F32)],
        scratch_shapes=[pltpu.VMEM((n_rows, 2 * SLAB_STATE), F32),
                        pltpu.VMEM((2 * SUBLANES, 2 * SLAB_STATE), F32)],
        compiler_params=_params(("parallel", "parallel")),
        name="s5_prompt",
    )(u_perm, s5w["wb"], s5w["wc"], s5w["a"], s5w["d"])


def _s5s_kernel(u_ref, wb_ref, wc_ref, a_ref, d_ref, h0r_ref, h0i_ref,
                g_ref, hlr_ref, hli_ref, h_scr, *, n_steps, bsz):
    half = SLAB_STATE
    u = u_ref[...]
    h_scr[...] = _dot(u.astype(BF16), wb_ref[...])
    ar = a_ref[:, :half]
    ai = a_ref[:, half:]
    hr = h0r_ref[...]
    hi = h0i_ref[...]
    for t in range(n_steps):
        rows = pl.ds(t * bsz, bsz)
        hr, hi = _cmul_add(ar, ai, hr, hi, h_scr[rows, :half], h_scr[rows, half:])
        h_scr[rows, :half] = hr
        h_scr[rows, half:] = hi
    hlr_ref[...] = hr
    hli_ref[...] = hi
    y = _dot(h_scr[...].astype(BF16), wc_ref[...]) + d_ref[...] * u
    g_ref[...] = jax.nn.gelu(y)


def _s5_sample(u_perm, h0_re, h0_im, s5w, n_steps):
    n_rows = u_perm.shape[0]
    bsz = n_rows // n_steps
    kern = functools.partial(_s5s_kernel, n_steps=n_steps, bsz=bsz)
    slab = lambda shape: pl.BlockSpec((None,) + shape, lambda s: (s, 0, 0))
    col = lambda r, w: pl.BlockSpec((r, w), lambda s: (0, s))
    return pl.pallas_call(
        kern,
        grid=(N_SLABS,),
        in_specs=[col(n_rows, LANES),
                  slab((LANES, 2 * SLAB_STATE)), slab((2 * SLAB_STATE, LANES)),
                  slab((1, 2 * SLAB_STATE)), slab((1, LANES)),
                  col(bsz, SLAB_STATE), col(bsz, SLAB_STATE)],
        out_specs=[col(n_rows, LANES), col(bsz, SLAB_STATE), col(bsz, SLAB_STATE)],
        out_shape=[jax.ShapeDtypeStruct((n_rows, SSM_WIDTH), F32),
                   jax.ShapeDtypeStruct((bsz, SSM_GROUPS * SSM_STATE), F32),
                   jax.ShapeDtypeStruct((bsz, SSM_GROUPS * SSM_STATE), F32)],
        scratch_shapes=[pltpu.VMEM((n_rows, 2 * SLAB_STATE), F32)],
        compiler_params=_params(("parallel",)),
        name="s5_sample",
    )(u_perm, s5w["wb"], s5w["wc"], s5w["a"], s5w["d"], h0_re, h0_im)


def _online_update(s, v, m_scr, l_scr, acc_scr):
    m_prev = m_scr[...]
    m_new = jnp.maximum(m_prev, jnp.max(s, axis=-1, keepdims=True))
    corr = jnp.exp(m_prev - m_new)
    p = jnp.exp(s - m_new)
    l_scr[...] = l_scr[...] * corr + jnp.sum(p, axis=-1, keepdims=True)
    acc_scr[...] = acc_scr[...] * corr + _dot(p.astype(BF16), v)
    m_scr[...] = m_new


def _value_up(acc_scr, l_scr, wuv_ref, o_ref, rows_per_head):
    o = acc_scr[...] / l_scr[...]
    for h in range(N_HEADS):
        oh = o[h * rows_per_head:(h + 1) * rows_per_head].astype(BF16)
        o_ref[:, h * V_DIM:(h + 1) * V_DIM] = _dot(oh, wuv_ref[h]).astype(o_ref.dtype)


def _pattn_kernel(ql_ref, qr_ref, kc_ref, kr_ref, wuv_ref, o_ref, m_scr, l_scr, acc_scr, *, tq):
    i = pl.program_id(1)
    n_rows = N_HEADS * tq
    ql = ql_ref[...].reshape(n_rows, KV_LORA)
    qr = qr_ref[...].reshape(n_rows, LANES)
    m_scr[...] = jnp.full(m_scr.shape, -jnp.inf, F32)
    l_scr[...] = jnp.zeros(l_scr.shape, F32)
    acc_scr[...] = jnp.zeros(acc_scr.shape, F32)

    def block(j, masked):
        rows = pl.ds(pl.multiple_of(j * tq, tq), tq)
        kc = kc_ref[rows, :]
        s = (_dot_nt(ql, kc) + _dot_nt(qr, kr_ref[rows, :])) * SCALE
        if masked:
            q_tok = lax.broadcasted_iota(jnp.int32, (n_rows, tq), 0) & (tq - 1)
            k_tok = lax.broadcasted_iota(jnp.int32, (n_rows, tq), 1)
            s = jnp.where(k_tok <= q_tok, s, -jnp.inf)
        _online_update(s, kc, m_scr, l_scr, acc_scr)

    def body(j, carry):
        block(j, False)
        return carry

    lax.fori_loop(0, i, body, 0)
    block(i, True)
    _value_up(acc_scr, l_scr, wuv_ref, o_ref, tq)


def _prompt_attention(ql, qr, kc, kr, wuv, bsz, seq, tq):
    nq = seq // tq
    n_rows = N_HEADS * tq
    kern = functools.partial(_pattn_kernel, tq=tq)
    return pl.pallas_call(
        kern,
        grid=(bsz, nq),
        in_specs=[pl.BlockSpec((N_HEADS, tq, KV_LORA), lambda b, i: (0, b * nq + i, 0)),
                  pl.BlockSpec((N_HEADS, tq, LANES), lambda b, i: (0, b * nq + i, 0)),
                  pl.BlockSpec((seq, KV_LORA), lambda b, i: (b, 0)),
                  pl.BlockSpec((seq, LANES), lambda b, i: (b, 0)),
                  _const_spec((N_HEADS, KV_LORA, V_DIM))],
        out_specs=pl.BlockSpec((tq, N_HEADS * V_DIM), lambda b, i: (b * nq + i, 0)),
        out_shape=jax.ShapeDtypeStruct((bsz * seq, N_HEADS * V_DIM), BF16),
        scratch_shapes=[pltpu.VMEM((n_rows, 1), F32), pltpu.VMEM((n_rows, 1), F32),
                        pltpu.VMEM((n_rows, KV_LORA), F32)],
        compiler_params=_params(("parallel", "arbitrary")),
        name="prompt_attention",
    )(ql, qr, kc, kr, wuv)


def _sattn_kernel(pt_ref, ql_ref, qr_ref, kn_ref, krn_ref, *rest, n_new, pages_per_step):
    del pt_ref
    ck_refs = rest[:pages_per_step]
    kr_refs = rest[pages_per_step:2 * pages_per_step]
    wuv_ref, o_ref, m_scr, l_scr, acc_scr = rest[2 * pages_per_step:]
    j = pl.program_id(1)
    n_rows = N_HEADS * n_new
    ql = ql_ref[...]
    qr = qr_ref[:, :ROPE_DIM]

    @pl.when(j == 0)
    def _():
        kc = kn_ref[...].astype(BF16)
        s = (_dot_nt(ql, kc) + _dot_nt(qr, krn_ref[...].astype(BF16))) * SCALE
        q_tok = lax.broadcasted_iota(jnp.int32, (n_rows, n_new), 0) & (n_new - 1)
        k_tok = lax.broadcasted_iota(jnp.int32, (n_rows, n_new), 1)
        s = jnp.where(k_tok <= q_tok, s, -jnp.inf)
        m = jnp.max(s, axis=-1, keepdims=True)
        p = jnp.exp(s - m)
        m_scr[...] = m
        l_scr[...] = jnp.sum(p, axis=-1, keepdims=True)
        acc_scr[...] = _dot(p.astype(BF16), kc)

    for ck_ref, kr_ref in zip(ck_refs, kr_refs):
        kc = ck_ref[...].astype(BF16)
        s = (_dot_nt(ql, kc) + _dot_nt(qr, kr_ref[...].astype(BF16))) * SCALE
        _online_update(s, kc, m_scr, l_scr, acc_scr)

    @pl.when(j == pl.num_programs(1) - 1)
    def _():
        _value_up(acc_scr, l_scr, wuv_ref, o_ref, n_new)


def _sample_attention(page_table, ql, qr, kc_new, kr_new, cache_ckv, cache_kr, wuv, n_new, pages_per_step):
    bsz, n_pages = page_table.shape
    n_rows = N_HEADS * n_new
    n_steps = n_pages // pages_per_step
    kern = functools.partial(_sattn_kernel, n_new=n_new, pages_per_step=pages_per_step)

    def page_spec(width, k):
        return pl.BlockSpec((None, PAGE_SIZE, width),
                            lambda b, j, pt: (pt[b, j * pages_per_step + k], 0, 0))

    in_specs = [pl.BlockSpec((None, n_rows, KV_LORA), lambda b, j, pt: (b, 0, 0)),
                pl.BlockSpec((None, n_rows, LANES), lambda b, j, pt: (b, 0, 0)),
                pl.BlockSpec((n_new, KV_LORA), lambda b, j, pt: (b, 0)),
                pl.BlockSpec((n_new, ROPE_DIM), lambda b, j, pt: (b, 0))]
    in_specs += [page_spec(KV_LORA, k) for k in range(pages_per_step)]
    in_specs += [page_spec(ROPE_DIM, k) for k in range(pages_per_step)]
    in_specs += [pl.BlockSpec((N_HEADS, KV_LORA, V_DIM), lambda b, j, pt: (0, 0, 0))]
    return pl.pallas_call(
        kern,
        grid_spec=pltpu.PrefetchScalarGridSpec(
            num_scalar_prefetch=1,
            grid=(bsz, n_steps),
            in_specs=in_specs,
            out_specs=pl.BlockSpec((n_new, N_HEADS * V_DIM), lambda b, j, pt: (b, 0)),
            scratch_shapes=[pltpu.VMEM((n_rows, 1), F32), pltpu.VMEM((n_rows, 1), F32),
                            pltpu.VMEM((n_rows, KV_LORA), F32)]),
        out_shape=jax.ShapeDtypeStruct((bsz * n_new, N_HEADS * V_DIM), F32),
        compiler_params=_params(("parallel", "arbitrary")),
        name="sample_attention",
    )(page_table, ql, qr, kc_new, kr_new,
      *([cache_ckv] * pages_per_step), *([cache_kr] * pages_per_step), wuv)


def _mix_kernel(g_ref, ya_ref, x_ref, wglu_ref, bglu_ref, wout_ref, lg_ref, lb_ref, o_ref, *, alpha):
    g = g_ref[...]
    z = _dot(g.astype(BF16), wglu_ref[...]) + bglu_ref[...]
    y_ssm = g * jax.nn.sigmoid(z)
    mix = (_dot(y_ssm.astype(BF16), wout_ref[:SSM_WIDTH, :])
           + _dot(ya_ref[...].astype(BF16), wout_ref[SSM_WIDTH:, :]))
    o_ref[...] = _layer_norm(alpha * x_ref[...] + mix, lg_ref[...], lb_ref[...])


def _mix(g, y_att, x, wts, alpha, tm):
    t = x.shape[0]
    row = lambda w: pl.BlockSpec((tm, w), lambda i: (i, 0))
    return pl.pallas_call(
        functools.partial(_mix_kernel, alpha=alpha),
        grid=(t // tm,),
        in_specs=[row(SSM_WIDTH), row(N_HEADS * V_DIM), row(D_MODEL),
                  _const_spec((SSM_WIDTH, SSM_WIDTH)), _const_spec((1, SSM_WIDTH)),
                  _const_spec((D_MODEL, D_MODEL)), _const_spec((1, D_MODEL)), _const_spec((1, D_MODEL))],
        out_specs=row(D_MODEL),
        out_shape=jax.ShapeDtypeStruct((t, D_MODEL), F32),
        compiler_params=_params(("parallel",)),
        name="mix",
    )(g, y_att, x, wts["w_glu"], wts["b_glu"], wts["w_out"], wts["ln1_g"], wts["ln1_b"])


def _ffn_kernel(x_ref, wg_ref, wu_ref, wd_ref, lg_ref, lb_ref, o_ref, xb_scr, acc_scr, *, alpha):
    f = pl.program_id(1)

    @pl.when(f == 0)
    def _():
        xb_scr[...] = x_ref[...].astype(BF16)
        acc_scr[...] = jnp.zeros(acc_scr.shape, F32)

    xb = xb_scr[...]
    gate = _dot(xb, wg_ref[...])
    up = _dot(xb, wu_ref[...])
    acc_scr[...] += _dot((jax.nn.silu(gate) * up).astype(BF16), wd_ref[...])

    @pl.when(f == pl.num_programs(1) - 1)
    def _():
        o_ref[...] = _layer_norm(alpha * x_ref[...] + acc_scr[...], lg_ref[...], lb_ref[...])


def _ffn(x, wts, alpha, tm, tf):
    t = x.shape[0]
    d_ff = wts["w_gate"].shape[1]
    return pl.pallas_call(
        functools.partial(_ffn_kernel, alpha=alpha),
        grid=(t // tm, d_ff // tf),
        in_specs=[pl.BlockSpec((tm, D_MODEL), lambda i, f: (i, 0)),
                  pl.BlockSpec((D_MODEL, tf), lambda i, f: (0, f)),
                  pl.BlockSpec((D_MODEL, tf), lambda i, f: (0, f)),
                  pl.BlockSpec((tf, D_MODEL), lambda i, f: (f, 0)),
                  pl.BlockSpec((1, D_MODEL), lambda i, f: (0, 0)),
                  pl.BlockSpec((1, D_MODEL), lambda i, f: (0, 0))],
        out_specs=pl.BlockSpec((tm, D_MODEL), lambda i, f: (i, 0)),
        out_shape=jax.ShapeDtypeStruct((t, D_MODEL), F32),
        scratch_shapes=[pltpu.VMEM((tm, D_MODEL), BF16), pltpu.VMEM((tm, D_MODEL), F32)],
        compiler_params=_params(("parallel", "arbitrary")),
        name="ffn",
    )(x, wts["w_gate"], wts["w_up"], wts["w_down"], wts["ln2_g"], wts["ln2_b"])


def _swap_halves(w):
    half = w.shape[-1] // 2
    return jnp.concatenate([w[..., half:], w[..., :half]], axis=-1)


def _pad_lanes(w):
    return jnp.concatenate([w, jnp.zeros(w.shape[:-1] + (LANES - w.shape[-1],), w.dtype)], axis=-1)


def _layer_weights(w_in, g_q, w_uq, w_uk, g_kv, w_uv, w_glu, b_glu, w_out, ln1_g, ln1_b,
                   w_gate, w_up, w_down, ln2_g, ln2_b):
    n_main = SSM_WIDTH + Q_LORA + KV_LORA
    w_kr = w_in[:, n_main:]
    uq = w_uq.reshape(Q_LORA, N_HEADS, NOPE_DIM + ROPE_DIM)
    uq_rope = uq[:, :, NOPE_DIM:]
    return {
        "w_main": w_in[:, :n_main].astype(BF16),
        "w_kr": jnp.concatenate([_pad_lanes(w_kr), _pad_lanes(_swap_halves(w_kr))], axis=1).astype(BF16),
        "g_q": g_q.reshape(1, Q_LORA), "g_kv": g_kv.reshape(1, KV_LORA),
        "w_qn": uq[:, :, :NOPE_DIM].reshape(Q_LORA, N_HEADS * NOPE_DIM).astype(BF16),
        "w_qr": _pad_lanes(uq_rope).reshape(Q_LORA, N_HEADS * LANES).astype(BF16),
        "w_qs": _pad_lanes(_swap_halves(uq_rope)).reshape(Q_LORA, N_HEADS * LANES).astype(BF16),
        "w_ukT": jnp.transpose(w_uk, (1, 2, 0)).astype(BF16),
        "w_uv": jnp.transpose(w_uv, (1, 0, 2)).astype(BF16),
        "w_glu": w_glu.astype(BF16), "b_glu": b_glu.reshape(1, SSM_WIDTH),
        "w_out": w_out.astype(BF16),
        "ln1_g": ln1_g.reshape(1, D_MODEL), "ln1_b": ln1_b.reshape(1, D_MODEL),
        "w_gate": w_gate.astype(BF16), "w_up": w_up.astype(BF16), "w_down": w_down.astype(BF16),
        "ln2_g": ln2_g.reshape(1, D_MODEL), "ln2_b": ln2_b.reshape(1, D_MODEL),
    }


def _s5_weights(a_re, a_im, log_step, b_re, b_im, c_re, c_im, d):
    rep = lambda a: jnp.repeat(a, SSM_CH, axis=0)
    bt = lambda b: jnp.transpose(b, (0, 2, 1)).reshape(SSM_GROUPS * SSM_CH, SSM_STATE)
    abr, abi, bbr, bbi = _s5prep(rep(a_re), rep(a_im), rep(log_step.reshape(SSM_GROUPS, 1)),
                                 bt(b_re), bt(b_im))
    abr = abr[::SSM_CH]
    abi = abi[::SSM_CH]
    eye = jnp.eye(SLAB_GROUPS, dtype=F32)

    def blockdiag_in(b):
        b4 = b.reshape(N_SLABS, SLAB_GROUPS, SSM_CH, SSM_STATE)
        return jnp.einsum("sghp,gk->sghkp", b4, eye).reshape(N_SLABS, LANES, SLAB_STATE)

    def blockdiag_out(c):
        c4 = c.reshape(N_SLABS, SLAB_GROUPS, SSM_CH, SSM_STATE)
        return jnp.einsum("sghp,gk->skpgh", c4, eye).reshape(N_SLABS, SLAB_STATE, LANES)

    slab_row = lambda a: a.reshape(N_SLABS, 1, SLAB_STATE)
    return {
        "wb": jnp.concatenate([blockdiag_in(bbr), blockdiag_in(bbi)], axis=2).astype(BF16),
        "wc": jnp.concatenate([blockdiag_out(c_re), -blockdiag_out(c_im)], axis=1).astype(BF16),
        "a": jnp.concatenate([slab_row(abr), slab_row(abi)], axis=2),
        "d": d.reshape(N_SLABS, 1, LANES),
    }


def _rope_tables(pos):
    half = ROPE_DIM // 2
    inv = ROPE_BASE ** (-2.0 * jnp.arange(half, dtype=F32) / ROPE_DIM)
    ang = pos.astype(F32)[:, None] * inv[None, :]
    cos = jnp.cos(ang)
    sin = jnp.sin(ang)
    return (_pad_lanes(jnp.concatenate([cos, cos], axis=1)),
            _pad_lanes(jnp.concatenate([-sin, sin], axis=1)))


def kernel(x_prompt, x_sample, cache_ckv, cache_krope, state_ssm_re, state_ssm_im, page_table, w_in, g_q, w_uq, w_uk, g_kv, w_uv, ssm_a_re, ssm_a_im, ssm_log_step, ssm_b_re, ssm_b_im, ssm_c_re, ssm_c_im, ssm_d, w_glu, b_glu, w_out, ln1_g, ln1_b, w_gate, w_up, w_down, ln2_g, ln2_b):
    depth = w_in.shape[0]
    alpha = (2 * depth) ** 0.25
    bsz, seq, _ = x_prompt.shape
    dbsz, dseq, _ = x_sample.shape
    past = page_table.shape[1] * PAGE_SIZE
    cos_p, sin_p = _rope_tables(jnp.arange(seq, dtype=jnp.int32))
    cos_p = jnp.tile(cos_p, (bsz, 1))
    sin_p = jnp.tile(sin_p, (bsz, 1))
    cos_s, sin_s = _rope_tables(past + jnp.arange(dseq, dtype=jnp.int32))
    cos_s = jnp.tile(cos_s, (dbsz, 1))
    sin_s = jnp.tile(sin_s, (dbsz, 1))
    seg_len = seq // SUBLANES

    y_p = x_prompt.reshape(bsz * seq, D_MODEL)
    y_s = x_sample.reshape(dbsz * dseq, D_MODEL)
    outs = [[] for _ in range(8)]
    for l in range(depth):
        wts = _layer_weights(w_in[l], g_q[l], w_uq[l], w_uk[l], g_kv[l], w_uv[l], w_glu[l], b_glu[l],
                             w_out[l], ln1_g[l], ln1_b[l], w_gate[l], w_up[l], w_down[l], ln2_g[l], ln2_b[l])
        s5w = _s5_weights(ssm_a_re[l], ssm_a_im[l], ssm_log_step[l], ssm_b_re[l], ssm_b_im[l],
                          ssm_c_re[l], ssm_c_im[l], ssm_d[l])

        u, ckv, kr, ckvb, krb, ql, qr = _inproj(y_p, cos_p, sin_p, wts, tm=256)
        u_perm = u.reshape(bsz, SUBLANES, seg_len, SSM_WIDTH).transpose(0, 2, 1, 3)
        g_perm, hl = _s5_prompt(u_perm.reshape(bsz, seq, SSM_WIDTH), s5w)
        g = g_perm.reshape(bsz, seg_len, SUBLANES, SSM_WIDTH).transpose(0, 2, 1, 3)
        y_att = _prompt_attention(ql, qr, ckvb, krb, wts["w_uv"], bsz, seq, tq=256)
        x1 = _mix(g.reshape(bsz * seq, SSM_WIDTH), y_att, y_p, wts, alpha, tm=256)
        y_p = _ffn(x1, wts, alpha, tm=512, tf=512)
        h_last = hl[:, :, SUBLANES - 1, :]
        outs[0].append(ckv.reshape(bsz, seq, KV_LORA))
        outs[1].append(kr.reshape(bsz, seq, ROPE_DIM))
        outs[2].append(h_last[:, :, :SLAB_STATE].reshape(bsz, SSM_GROUPS, SSM_STATE))
        outs[3].append(h_last[:, :, SLAB_STATE:].reshape(bsz, SSM_GROUPS, SSM_STATE))

        u, ckv, kr, _, _, ql, qr = _inproj(y_s, cos_s, sin_s, wts, tm=256)
        u_perm = u.reshape(dbsz, dseq, SSM_WIDTH).transpose(1, 0, 2).reshape(dseq * dbsz, SSM_WIDTH)
        g_perm, hl_re, hl_im = _s5_sample(
            u_perm, state_ssm_re[l].reshape(dbsz, SSM_GROUPS * SSM_STATE),
            state_ssm_im[l].reshape(dbsz, SSM_GROUPS * SSM_STATE), s5w, dseq)
        g = g_perm.reshape(dseq, dbsz, SSM_WIDTH).transpose(1, 0, 2).reshape(dbsz * dseq, SSM_WIDTH)
        per_batch = lambda q: q.reshape(N_HEADS, dbsz, dseq, q.shape[-1]).transpose(1, 0, 2, 3).reshape(
            dbsz, N_HEADS * dseq, q.shape[-1])
        y_att = _sample_attention(page_table, per_batch(ql), per_batch(qr), ckv, kr,
                                  cache_ckv[l], cache_krope[l], wts["w_uv"], dseq, pages_per_step=8)
        x1 = _mix(g, y_att, y_s, wts, alpha, tm=256)
        y_s = _ffn(x1, wts, alpha, tm=512, tf=512)
        outs[4].append(ckv.reshape(dbsz, dseq, KV_LORA))
        outs[5].append(kr.reshape(dbsz, dseq, ROPE_DIM))
        outs[6].append(hl_re.reshape(dbsz, SSM_GROUPS, SSM_STATE))
        outs[7].append(hl_im.reshape(dbsz, SSM_GROUPS, SSM_STATE))

    return (y_p.reshape(bsz, seq, D_MODEL), y_s.reshape(dbsz, dseq, D_MODEL),
            *[jnp.stack(o) for o in outs])
```

```python
import functools
import math

import jax
import jax.numpy as jnp
from jax import lax
from jax.experimental import pallas as pl
from jax.experimental.pallas import tpu as pltpu

F32 = jnp.float32
BF16 = jnp.bfloat16

D_MODEL = 2048
SSM_WIDTH = 1024
SSM_CH = 16
SSM_GROUPS = 64
SSM_STATE = 64
N_HEADS = 8
NOPE_DIM = 128
ROPE_DIM = 64
V_DIM = 128
Q_LORA = 512
KV_LORA = 512
ROPE_BASE = 10000.0
SCALE = (NOPE_DIM + ROPE_DIM) ** -0.5
PAGE_SIZE = 128
LANES = 128
SUBLANES = 8
SLAB_GROUPS = LANES // SSM_CH
N_SLABS = SSM_GROUPS // SLAB_GROUPS
SLAB_STATE = SLAB_GROUPS * SSM_STATE
VMEM_LIMIT = 56 * 1024 * 1024
ATTN_TQ = 256
ATTN_TK = 512
ATTN_HEAD_GROUP = 2
SAMPLE_PAGES_PER_STEP = 16


def _dot(a, b):
    return jnp.dot(a, b, preferred_element_type=F32)


def _dot_nt(a, b):
    return lax.dot_general(a, b, (((1,), (1,)), ((), ())), preferred_element_type=F32)


def _const_spec(shape):
    n = len(shape)
    return pl.BlockSpec(shape, lambda *_: (0,) * n, pipeline_mode=pl.Buffered(1))


def _params(sem):
    return pltpu.CompilerParams(dimension_semantics=sem, vmem_limit_bytes=VMEM_LIMIT)


def _rms(x, g, eps=1e-6):
    return x * lax.rsqrt(jnp.mean(x * x, axis=-1, keepdims=True) + eps) * g


def _layer_norm(x, g, b, eps=1e-5):
    mu = jnp.mean(x, axis=-1, keepdims=True)
    xc = x - mu
    var = jnp.mean(xc * xc, axis=-1, keepdims=True)
    return xc * lax.rsqrt(var + eps) * g + b


def _inproj_kernel(x_ref, wm_ref, wkr_ref, gq_ref, gkv_ref, cos_ref, sin_ref,
                   wqn_ref, wqr_ref, wqs_ref, wuk_ref,
                   u_ref, ckv_ref, kr_ref, ckvb_ref, krb_ref, ql_ref, qr_ref, *, q_transposed):
    xb = x_ref[...].astype(BF16)
    cos = cos_ref[...]
    sin = sin_ref[...]
    proj = _dot(xb, wm_ref[...])
    u_ref[...] = proj[:, :SSM_WIDTH]
    cq = proj[:, SSM_WIDTH:SSM_WIDTH + Q_LORA]
    ckv = _rms(proj[:, SSM_WIDTH + Q_LORA:], gkv_ref[...])
    ckv_ref[...] = ckv
    ckvb_ref[...] = ckv.astype(BF16)
    kr2 = _dot(xb, wkr_ref[...])
    krope = kr2[:, :LANES] * cos + kr2[:, LANES:] * sin
    kr_ref[...] = krope[:, :ROPE_DIM]
    krb_ref[...] = krope.astype(BF16)
    cqn = _rms(cq, gq_ref[...])
    if q_transposed:
        cqt = cqn.T.astype(BF16)
        qn = _dot(wqn_ref[...], cqt)
        qa = _dot(wqr_ref[...], cqt)
        qs = _dot(wqs_ref[...], cqt)
        cos_t = cos.T
        sin_t = sin.T
        tm = cos.shape[0]
        for h in range(N_HEADS):
            sl = slice(h * LANES, (h + 1) * LANES)
            cols = slice(h * tm, (h + 1) * tm)
            ql_ref[:, cols] = _dot(wuk_ref[h], qn[sl].astype(BF16)).astype(BF16)
            qr_ref[:, cols] = (qa[sl] * cos_t + qs[sl] * sin_t).astype(BF16)
    else:
        cqb = cqn.astype(BF16)
        qn = _dot(cqb, wqn_ref[...])
        qa = _dot(cqb, wqr_ref[...])
        qs = _dot(cqb, wqs_ref[...])
        for h in range(N_HEADS):
            sl = slice(h * LANES, (h + 1) * LANES)
            ql_ref[h] = _dot(qn[:, sl].astype(BF16), wuk_ref[h]).astype(BF16)
            qr_ref[h] = (qa[:, sl] * cos + qs[:, sl] * sin).astype(BF16)


def _inproj(x, cos, sin, wts, tm, q_transposed):
    t = x.shape[0]
    row = lambda w: pl.BlockSpec((tm, w), lambda i: (i, 0))
    if q_transposed:
        qspec = lambda w: pl.BlockSpec((None, w, N_HEADS * tm), lambda i: (i, 0, 0))
        qshape = lambda w: jax.ShapeDtypeStruct((t // tm, w, N_HEADS * tm), BF16)
        wq = [wts["w_qnT"], wts["w_qrT"], wts["w_qsT"], wts["w_uk"]]
    else:
        qspec = lambda w: pl.BlockSpec((N_HEADS, tm, w), lambda i: (0, i, 0))
        qshape = lambda w: jax.ShapeDtypeStruct((N_HEADS, t, w), BF16)
        wq = [wts["w_qn"], wts["w_qr"], wts["w_qs"], wts["w_ukT"]]
    return pl.pallas_call(
        functools.partial(_inproj_kernel, q_transposed=q_transposed),
        grid=(t // tm,),
        in_specs=[row(D_MODEL),
                  _const_spec((D_MODEL, 2048)), _const_spec((D_MODEL, 2 * LANES)),
                  _const_spec((1, Q_LORA)), _const_spec((1, KV_LORA)),
                  row(LANES), row(LANES)] + [_const_spec(w.shape) for w in wq],
        out_specs=[row(SSM_WIDTH), row(KV_LORA), row(ROPE_DIM), row(KV_LORA), row(LANES),
                   qspec(KV_LORA), qspec(LANES)],
        out_shape=[jax.ShapeDtypeStruct((t, SSM_WIDTH), F32),
                   jax.ShapeDtypeStruct((t, KV_LORA), F32),
                   jax.ShapeDtypeStruct((t, ROPE_DIM), F32),
                   jax.ShapeDtypeStruct((t, KV_LORA), BF16),
                   jax.ShapeDtypeStruct((t, LANES), BF16),
                   qshape(KV_LORA), qshape(LANES)],
        compiler_params=_params(("parallel",)),
        name="inproj",
    )(x, wts["w_main"], wts["w_kr"], wts["g_q"], wts["g_kv"], cos, sin, *wq)


def _s5prep_kernel(are_ref, aim_ref, ls_ref, bre_ref, bim_ref,
                   abr_ref, abi_ref, bbr_ref, bbi_ref):
    lr = are_ref[...]
    li = aim_ref[...]
    delta = jnp.exp(ls_ref[...])
    mag = jnp.exp(lr * delta)
    ar = mag * jnp.cos(li * delta)
    ai = mag * jnp.sin(li * delta)
    abr_ref[...] = ar
    abi_ref[...] = ai
    den = lr * lr + li * li
    fr = ((ar - 1.0) * lr + ai * li) / den
    fi = (ai * lr - (ar - 1.0) * li) / den
    br = bre_ref[...]
    bi = bim_ref[...]
    bbr_ref[...] = fr * br - fi * bi
    bbi_ref[...] = fr * bi + fi * br


def _s5prep(a_re, a_im, log_step, bt_re, bt_im):
    shp = jax.ShapeDtypeStruct((SSM_GROUPS * SSM_CH, SSM_STATE), F32)
    return pl.pallas_call(_s5prep_kernel, out_shape=[shp] * 4, name="s5prep")(
        a_re, a_im, log_step, bt_re, bt_im)


def _cmul_add(ar, ai, hr, hi, br, bi):
    return ar * hr - ai * hi + br, ar * hi + ai * hr + bi


def _s5p_kernel(u_ref, wb_ref, wc_ref, a_ref, d_ref, g_ref, hl_ref, bu_scr, e_scr, *, seg_len, chunk):
    n_rows = SUBLANES * seg_len
    half = SLAB_STATE
    wb = wb_ref[...]
    for c in range(n_rows // chunk):
        rows = pl.ds(c * chunk, chunk)
        bu_scr[rows, :] = _dot(u_ref[rows, :].astype(BF16), wb)
    ar = jnp.broadcast_to(a_ref[:, :half], (SUBLANES, half))
    ai = jnp.broadcast_to(a_ref[:, half:], (SUBLANES, half))

    def step(t, carry, store):
        hr, hi = carry
        rows = pl.ds(pl.multiple_of(t * SUBLANES, SUBLANES), SUBLANES)
        hr, hi = _cmul_add(ar, ai, hr, hi, bu_scr[rows, :half], bu_scr[rows, half:])
        if store:
            bu_scr[rows, :half] = hr
            bu_scr[rows, half:] = hi
        return hr, hi

    zero = jnp.zeros((SUBLANES, half), F32)
    er, ei = lax.fori_loop(0, seg_len, functools.partial(step, store=False), (zero, zero), unroll=8)
    pr, pi = ar, ai
    for _ in range(int(math.log2(seg_len))):
        pr, pi = pr * pr - pi * pi, 2.0 * pr * pi
    e_scr[0:SUBLANES, :half] = er
    e_scr[0:SUBLANES, half:] = ei
    cr = jnp.zeros((1, half), F32)
    ci = jnp.zeros((1, half), F32)
    e_scr[SUBLANES:SUBLANES + 1, :half] = cr
    e_scr[SUBLANES:SUBLANES + 1, half:] = ci
    for s in range(SUBLANES - 1):
        cr, ci = _cmul_add(pr[0:1], pi[0:1], cr, ci, e_scr[s:s + 1, :half], e_scr[s:s + 1, half:])
        e_scr[SUBLANES + s + 1:SUBLANES + s + 2, :half] = cr
        e_scr[SUBLANES + s + 1:SUBLANES + s + 2, half:] = ci
    init = (e_scr[SUBLANES:2 * SUBLANES, :half], e_scr[SUBLANES:2 * SUBLANES, half:])
    hr, hi = lax.fori_loop(0, seg_len, functools.partial(step, store=True), init, unroll=8)
    hl_ref[:, :half] = hr
    hl_ref[:, half:] = hi
    wc = wc_ref[...]
    d = d_ref[...]
    for c in range(n_rows // chunk):
        rows = pl.ds(c * chunk, chunk)
        y = _dot(bu_scr[rows, :].astype(BF16), wc) + d * u_ref[rows, :]
        g_ref[rows, :] = jax.nn.gelu(y)


def _s5_prompt(u_perm, s5w):
    bsz, n_rows, _ = u_perm.shape
    seg_len = n_rows // SUBLANES
    kern = functools.partial(_s5p_kernel, seg_len=seg_len, chunk=512)
    slab = lambda shape: pl.BlockSpec((None,) + shape, lambda b, s: (s, 0, 0))
    return pl.pallas_call(
        kern,
        grid=(bsz, N_SLABS),
        in_specs=[pl.BlockSpec((None, n_rows, LANES), lambda b, s: (b, 0, s)),
                  slab((LANES, 2 * SLAB_STATE)), slab((2 * SLAB_STATE, LANES)),
                  slab((1, 2 * SLAB_STATE)), slab((1, LANES))],
        out_specs=[pl.BlockSpec((None, n_rows, LANES), lambda b, s: (b, 0, s)),
                   pl.BlockSpec((None, None, SUBLANES, 2 * SLAB_STATE), lambda b, s: (b, s, 0, 0))],
        out_shape=[jax.ShapeDtypeStruct((bsz, n_rows, SSM_WIDTH), F32),
                   jax.ShapeDtypeStruct((bsz, N_SLABS, SUBLANES, 2 * SLAB_STATE), F32)],
        scratch_shapes=[pltpu.VMEM((n_rows, 2 * SLAB_STATE), F32),
                        pltpu.VMEM((2 * SUBLANES, 2 * SLAB_STATE), F32)],
        compiler_params=_params(("parallel", "parallel")),
        name="s5_prompt",
    )(u_perm, s5w["wb"], s5w["wc"], s5w["a"], s5w["d"])


def _s5s_kernel(u_ref, wb_ref, wc_ref, a_ref, d_ref, h0r_ref, h0i_ref,
                g_ref, hlr_ref, hli_ref, h_scr, *, n_steps, bsz):
    half = SLAB_STATE
    u = u_ref[...]
    h_scr[...] = _dot(u.astype(BF16), wb_ref[...])
    ar = a_ref[:, :half]
    ai = a_ref[:, half:]
    hr = h0r_ref[...]
    hi = h0i_ref[...]
    for t in range(n_steps):
        rows = pl.ds(t * bsz, bsz)
        hr, hi = _cmul_add(ar, ai, hr, hi, h_scr[rows, :half], h_scr[rows, half:])
        h_scr[rows, :half] = hr
        h_scr[rows, half:] = hi
    hlr_ref[...] = hr
    hli_ref[...] = hi
    y = _dot(h_scr[...].astype(BF16), wc_ref[...]) + d_ref[...] * u
    g_ref[...] = jax.nn.gelu(y)


def _s5_sample(u_perm, h0_re, h0_im, s5w, n_steps):
    n_rows = u_perm.shape[0]
    bsz = n_rows // n_steps
    kern = functools.partial(_s5s_kernel, n_steps=n_steps, bsz=bsz)
    slab = lambda shape: pl.BlockSpec((None,) + shape, lambda s: (s, 0, 0))
    col = lambda r, w: pl.BlockSpec((r, w), lambda s: (0, s))
    return pl.pallas_call(
        kern,
        grid=(N_SLABS,),
        in_specs=[col(n_rows, LANES),
                  slab((LANES, 2 * SLAB_STATE)), slab((2 * SLAB_STATE, LANES)),
                  slab((1, 2 * SLAB_STATE)), slab((1, LANES)),
                  col(bsz, SLAB_STATE), col(bsz, SLAB_STATE)],
        out_specs=[col(n_rows, LANES), col(bsz, SLAB_STATE), col(bsz, SLAB_STATE)],
        out_shape=[jax.ShapeDtypeStruct((n_rows, SSM_WIDTH), F32),
                   jax.ShapeDtypeStruct((bsz, SSM_GROUPS * SSM_STATE), F32),
                   jax.ShapeDtypeStruct((bsz, SSM_GROUPS * SSM_STATE), F32)],
        scratch_shapes=[pltpu.VMEM((n_rows, 2 * SLAB_STATE), F32)],
        compiler_params=_params(("parallel",)),
        name="s5_sample",
    )(u_perm, s5w["wb"], s5w["wc"], s5w["a"], s5w["d"], h0_re, h0_im)


def _pattn_kernel(ql_ref, qr_ref, kc_ref, kr_ref, vt_ref, wuv_ref, o_ref, m_scr, l_scr, acc_scr, *, tq, tk):
    i = pl.program_id(1)
    gw = ATTN_HEAD_GROUP * tq
    n_groups = N_HEADS // ATTN_HEAD_GROUP
    m_scr[...] = jnp.full(m_scr.shape, -jnp.inf, F32)
    l_scr[...] = jnp.zeros(l_scr.shape, F32)
    acc_scr[...] = jnp.zeros(acc_scr.shape, F32)

    def block(j, masked):
        kc = kc_ref[j]
        kr = kr_ref[j]
        vt = vt_ref[j]
        if masked:
            k_tok = j * tk + lax.broadcasted_iota(jnp.int32, (tk, gw), 0)
            q_tok = i * tq + (lax.broadcasted_iota(jnp.int32, (tk, gw), 1) & (tq - 1))
            keep = k_tok <= q_tok

        def scores(g):
            cols = slice(g * gw, (g + 1) * gw)
            s = (_dot(kc, ql_ref[:, cols]) + _dot(kr, qr_ref[:, cols])) * SCALE
            return jnp.where(keep, s, -jnp.inf) if masked else s

        s_next = scores(0)
        for g in range(n_groups):
            s = s_next
            if g + 1 < n_groups:
                s_next = scores(g + 1)
            cols = slice(g * gw, (g + 1) * gw)
            m_prev = m_scr[:, cols]
            m_new = jnp.maximum(m_prev, jnp.max(s, axis=0, keepdims=True))
            corr = jnp.exp(m_prev - m_new)
            p = jnp.exp(s - m_new)
            l_scr[:, cols] = l_scr[:, cols] * corr + jnp.sum(p, axis=0, keepdims=True)
            acc_scr[:, cols] = acc_scr[:, cols] * corr + _dot(vt, p.astype(BF16))
            m_scr[:, cols] = m_new

    def body(j, carry):
        block(j, False)
        return carry

    j_last = (i * tq + tq - 1) // tk
    lax.fori_loop(0, j_last, body, 0)
    block(j_last, True)
    for h in range(N_HEADS):
        cols = slice(h * tq, (h + 1) * tq)
        o_t = (acc_scr[:, cols] / l_scr[:, cols]).astype(BF16)
        y_t = _dot(wuv_ref[h], o_t)
        o_ref[:, h * V_DIM:(h + 1) * V_DIM] = y_t.T.astype(o_ref.dtype)


def _prompt_attention(ql_t, qr_t, kc, kr, v_t, wuv_t, bsz, seq, tq, tk):
    nq = seq // tq
    nk = seq // tk
    kern = functools.partial(_pattn_kernel, tq=tq, tk=tk)
    per_batch = lambda r, c: pl.BlockSpec((None, nk, r, c), lambda b, i: (b, 0, 0, 0))
    return pl.pallas_call(
        kern,
        grid=(bsz, nq),
        in_specs=[pl.BlockSpec((None, KV_LORA, N_HEADS * tq), lambda b, i: (b * nq + i, 0, 0)),
                  pl.BlockSpec((None, LANES, N_HEADS * tq), lambda b, i: (b * nq + i, 0, 0)),
                  per_batch(tk, KV_LORA), per_batch(tk, LANES), per_batch(KV_LORA, tk),
                  _const_spec((N_HEADS, V_DIM, KV_LORA))],
        out_specs=pl.BlockSpec((tq, N_HEADS * V_DIM), lambda b, i: (b * nq + i, 0)),
        out_shape=jax.ShapeDtypeStruct((bsz * seq, N_HEADS * V_DIM), BF16),
        scratch_shapes=[pltpu.VMEM((1, N_HEADS * tq), F32), pltpu.VMEM((1, N_HEADS * tq), F32),
                        pltpu.VMEM((KV_LORA, N_HEADS * tq), F32)],
        compiler_params=_params(("parallel", "arbitrary")),
        name="prompt_attention",
    )(ql_t, qr_t, kc, kr, v_t, wuv_t)


N_CHAINS = 4


def _sattn_kernel(pt_ref, ql_ref, qr_ref, kn_ref, krn_ref, *rest, n_new, pages_per_step):
    del pt_ref
    ck_refs = rest[:pages_per_step]
    kr_refs = rest[pages_per_step:2 * pages_per_step]
    wuv_ref, o_ref, m_scr, l_scr, acc_scr, kc_scr = rest[2 * pages_per_step:]
    j = pl.program_id(1)
    n_rows = N_HEADS * n_new
    ql = ql_ref[...]
    qr = qr_ref[:, :ROPE_DIM]

    @pl.when(j == 0)
    def _():
        kc = kn_ref[...].astype(BF16)
        s = (_dot_nt(ql, kc) + _dot_nt(qr, krn_ref[...].astype(BF16))) * SCALE
        q_tok = lax.broadcasted_iota(jnp.int32, (n_rows, n_new), 0) & (n_new - 1)
        k_tok = lax.broadcasted_iota(jnp.int32, (n_rows, n_new), 1)
        s = jnp.where(k_tok <= q_tok, s, -jnp.inf)
        m = jnp.max(s, axis=-1, keepdims=True)
        p = jnp.exp(s - m)
        m_scr[0] = m
        l_scr[0] = jnp.sum(p, axis=-1, keepdims=True)
        acc_scr[0] = _dot(p.astype(BF16), kc)
        for c in range(1, N_CHAINS):
            m_scr[c] = jnp.full((n_rows, 1), -jnp.inf, F32)
            l_scr[c] = jnp.zeros((n_rows, 1), F32)
            acc_scr[c] = jnp.zeros((n_rows, KV_LORA), F32)

    per_chain = pages_per_step // N_CHAINS
    chain_rows = per_chain * PAGE_SIZE

    def scores(c):
        pages = range(c * per_chain, (c + 1) * per_chain)
        for k in pages:
            kc_scr[k * PAGE_SIZE:(k + 1) * PAGE_SIZE, :] = ck_refs[k][...].astype(BF16)
        kc = kc_scr[c * chain_rows:(c + 1) * chain_rows, :]
        kr_t = jnp.concatenate([kr_refs[k][...] for k in pages], axis=1).astype(BF16)
        return (_dot_nt(ql, kc) + _dot(qr, kr_t)) * SCALE, kc

    nxt = scores(0)
    for c in range(N_CHAINS):
        s, kc = nxt
        if c + 1 < N_CHAINS:
            nxt = scores(c + 1)
        m_prev = m_scr[c]
        m_new = jnp.maximum(m_prev, jnp.max(s, axis=-1, keepdims=True))
        corr = jnp.exp(m_prev - m_new)
        p = jnp.exp(s - m_new)
        l_scr[c] = l_scr[c] * corr + jnp.sum(p, axis=-1, keepdims=True)
        acc_scr[c] = acc_scr[c] * corr + _dot(p.astype(BF16), kc)
        m_scr[c] = m_new

    @pl.when(j == pl.num_programs(1) - 1)
    def _():
        m = m_scr[0]
        for c in range(1, N_CHAINS):
            m = jnp.maximum(m, m_scr[c])
        l = jnp.zeros((n_rows, 1), F32)
        acc = jnp.zeros((n_rows, KV_LORA), F32)
        for c in range(N_CHAINS):
            w = jnp.exp(m_scr[c] - m)
            l = l + l_scr[c] * w
            acc = acc + acc_scr[c] * w
        o = acc / l
        for h in range(N_HEADS):
            oh = o[h * n_new:(h + 1) * n_new].astype(BF16)
            o_ref[:, h * V_DIM:(h + 1) * V_DIM] = _dot(oh, wuv_ref[h]).astype(o_ref.dtype)


def _sample_attention(page_table, ql, qr, kc_new, kr_new, cache_ckv, cache_kr_t, wuv, n_new, pages_per_step):
    bsz, n_pages = page_table.shape
    n_rows = N_HEADS * n_new
    n_steps = n_pages // pages_per_step
    kern = functools.partial(_sattn_kernel, n_new=n_new, pages_per_step=pages_per_step)

    def page_spec(rows, cols, k):
        return pl.BlockSpec((None, rows, cols), lambda b, j, pt: (pt[b, j * pages_per_step + k], 0, 0))

    in_specs = [pl.BlockSpec((None, n_rows, KV_LORA), lambda b, j, pt: (b, 0, 0)),
                pl.BlockSpec((None, n_rows, LANES), lambda b, j, pt: (b, 0, 0)),
                pl.BlockSpec((n_new, KV_LORA), lambda b, j, pt: (b, 0)),
                pl.BlockSpec((n_new, ROPE_DIM), lambda b, j, pt: (b, 0))]
    in_specs += [page_spec(PAGE_SIZE, KV_LORA, k) for k in range(pages_per_step)]
    in_specs += [page_spec(ROPE_DIM, PAGE_SIZE, k) for k in range(pages_per_step)]
    in_specs += [pl.BlockSpec((N_HEADS, KV_LORA, V_DIM), lambda b, j, pt: (0, 0, 0))]
    return pl.pallas_call(
        kern,
        grid_spec=pltpu.PrefetchScalarGridSpec(
            num_scalar_prefetch=1,
            grid=(bsz, n_steps),
            in_specs=in_specs,
            out_specs=pl.BlockSpec((n_new, N_HEADS * V_DIM), lambda b, j, pt: (b, 0)),
            scratch_shapes=[pltpu.VMEM((N_CHAINS, n_rows, 1), F32), pltpu.VMEM((N_CHAINS, n_rows, 1), F32),
                            pltpu.VMEM((N_CHAINS, n_rows, KV_LORA), F32),
                            pltpu.VMEM((pages_per_step * PAGE_SIZE, KV_LORA), BF16)]),
        out_shape=jax.ShapeDtypeStruct((bsz * n_new, N_HEADS * V_DIM), F32),
        compiler_params=_params(("parallel", "arbitrary")),
        name="sample_attention",
    )(page_table, ql, qr, kc_new, kr_new,
      *([cache_ckv] * pages_per_step), *([cache_kr_t] * pages_per_step), wuv)


def _mix_kernel(g_ref, ya_ref, x_ref, wglu_ref, bglu_ref, wout_ref, lg_ref, lb_ref, o_ref, *, alpha):
    g = g_ref[...]
    z = _dot(g.astype(BF16), wglu_ref[...]) + bglu_ref[...]
    y_ssm = g * jax.nn.sigmoid(z)
    mix = (_dot(y_ssm.astype(BF16), wout_ref[:SSM_WIDTH, :])
           + _dot(ya_ref[...].astype(BF16), wout_ref[SSM_WIDTH:, :]))
    o_ref[...] = _layer_norm(alpha * x_ref[...] + mix, lg_ref[...], lb_ref[...])


def _mix(g, y_att, x, wts, alpha, tm):
    t = x.shape[0]
    row = lambda w: pl.BlockSpec((tm, w), lambda i: (i, 0))
    return pl.pallas_call(
        functools.partial(_mix_kernel, alpha=alpha),
        grid=(t // tm,),
        in_specs=[row(SSM_WIDTH), row(N_HEADS * V_DIM), row(D_MODEL),
                  _const_spec((SSM_WIDTH, SSM_WIDTH)), _const_spec((1, SSM_WIDTH)),
                  _const_spec((D_MODEL, D_MODEL)), _const_spec((1, D_MODEL)), _const_spec((1, D_MODEL))],
        out_specs=row(D_MODEL),
        out_shape=jax.ShapeDtypeStruct((t, D_MODEL), F32),
        compiler_params=_params(("parallel",)),
        name="mix",
    )(g, y_att, x, wts["w_glu"], wts["b_glu"], wts["w_out"], wts["ln1_g"], wts["ln1_b"])


def _ffn_kernel(x_ref, wg_ref, wu_ref, wd_ref, lg_ref, lb_ref, o_ref, xb_scr, acc_scr, *, alpha):
    f = pl.program_id(1)

    @pl.when(f == 0)
    def _():
        xb_scr[...] = x_ref[...].astype(BF16)
        acc_scr[...] = jnp.zeros(acc_scr.shape, F32)

    xb = xb_scr[...]
    gate = _dot(xb, wg_ref[...])
    up = _dot(xb, wu_ref[...])
    acc_scr[...] += _dot((jax.nn.silu(gate) * up).astype(BF16), wd_ref[...])

    @pl.when(f == pl.num_programs(1) - 1)
    def _():
        o_ref[...] = _layer_norm(alpha * x_ref[...] + acc_scr[...], lg_ref[...], lb_ref[...])


def _ffn(x, wts, alpha, tm, tf):
    t = x.shape[0]
    d_ff = wts["w_gate"].shape[1]
    return pl.pallas_call(
        functools.partial(_ffn_kernel, alpha=alpha),
        grid=(t // tm, d_ff // tf),
        in_specs=[pl.BlockSpec((tm, D_MODEL), lambda i, f: (i, 0)),
                  pl.BlockSpec((D_MODEL, tf), lambda i, f: (0, f)),
                  pl.BlockSpec((D_MODEL, tf), lambda i, f: (0, f)),
                  pl.BlockSpec((tf, D_MODEL), lambda i, f: (f, 0)),
                  pl.BlockSpec((1, D_MODEL), lambda i, f: (0, 0)),
                  pl.BlockSpec((1, D_MODEL), lambda i, f: (0, 0))],
        out_specs=pl.BlockSpec((tm, D_MODEL), lambda i, f: (i, 0)),
        out_shape=jax.ShapeDtypeStruct((t, D_MODEL), F32),
        scratch_shapes=[pltpu.VMEM((tm, D_MODEL), BF16), pltpu.VMEM((tm, D_MODEL), F32)],
        compiler_params=_params(("parallel", "arbitrary")),
        name="ffn",
    )(x, wts["w_gate"], wts["w_up"], wts["w_down"], wts["ln2_g"], wts["ln2_b"])


def _swap_halves(w):
    half = w.shape[-1] // 2
    return jnp.concatenate([w[..., half:], w[..., :half]], axis=-1)


def _pad_lanes(w):
    return jnp.concatenate([w, jnp.zeros(w.shape[:-1] + (LANES - w.shape[-1],), w.dtype)], axis=-1)


def _layer_weights(w_in, g_q, w_uq, w_uk, g_kv, w_uv, w_glu, b_glu, w_out, ln1_g, ln1_b,
                   w_gate, w_up, w_down, ln2_g, ln2_b):
    n_main = SSM_WIDTH + Q_LORA + KV_LORA
    w_kr = w_in[:, n_main:]
    uq = w_uq.reshape(Q_LORA, N_HEADS, NOPE_DIM + ROPE_DIM)
    uq_rope = uq[:, :, NOPE_DIM:]
    w_qn = uq[:, :, :NOPE_DIM].reshape(Q_LORA, N_HEADS * NOPE_DIM).astype(BF16)
    w_qr = _pad_lanes(uq_rope).reshape(Q_LORA, N_HEADS * LANES).astype(BF16)
    w_qs = _pad_lanes(_swap_halves(uq_rope)).reshape(Q_LORA, N_HEADS * LANES).astype(BF16)
    return {
        "w_main": w_in[:, :n_main].astype(BF16),
        "w_kr": jnp.concatenate([_pad_lanes(w_kr), _pad_lanes(_swap_halves(w_kr))], axis=1).astype(BF16),
        "g_q": g_q.reshape(1, Q_LORA), "g_kv": g_kv.reshape(1, KV_LORA),
        "w_qn": w_qn, "w_qr": w_qr, "w_qs": w_qs,
        "w_qnT": w_qn.T, "w_qrT": w_qr.T, "w_qsT": w_qs.T,
        "w_ukT": jnp.transpose(w_uk, (1, 2, 0)).astype(BF16),
        "w_uk": jnp.transpose(w_uk, (1, 0, 2)).astype(BF16),
        "w_uv": jnp.transpose(w_uv, (1, 0, 2)).astype(BF16),
        "w_uvT": jnp.transpose(w_uv, (1, 2, 0)).astype(BF16),
        "w_glu": w_glu.astype(BF16), "b_glu": b_glu.reshape(1, SSM_WIDTH),
        "w_out": w_out.astype(BF16),
        "ln1_g": ln1_g.reshape(1, D_MODEL), "ln1_b": ln1_b.reshape(1, D_MODEL),
        "w_gate": w_gate.astype(BF16), "w_up": w_up.astype(BF16), "w_down": w_down.astype(BF16),
        "ln2_g": ln2_g.reshape(1, D_MODEL), "ln2_b": ln2_b.reshape(1, D_MODEL),
    }


def _s5_weights(a_re, a_im, log_step, b_re, b_im, c_re, c_im, d):
    rep = lambda a: jnp.repeat(a, SSM_CH, axis=0)
    bt = lambda b: jnp.transpose(b, (0, 2, 1)).reshape(SSM_GROUPS * SSM_CH, SSM_STATE)
    abr, abi, bbr, bbi = _s5prep(rep(a_re), rep(a_im), rep(log_step.reshape(SSM_GROUPS, 1)),
                                 bt(b_re), bt(b_im))
    abr = abr[::SSM_CH]
    abi = abi[::SSM_CH]
    eye = jnp.eye(SLAB_GROUPS, dtype=F32)

    def blockdiag_in(b):
        b4 = b.reshape(N_SLABS, SLAB_GROUPS, SSM_CH, SSM_STATE)
        return jnp.einsum("sghp,gk->sghkp", b4, eye).reshape(N_SLABS, LANES, SLAB_STATE)

    def blockdiag_out(c):
        c4 = c.reshape(N_SLABS, SLAB_GROUPS, SSM_CH, SSM_STATE)
        return jnp.einsum("sghp,gk->skpgh", c4, eye).reshape(N_SLABS, SLAB_STATE, LANES)

    slab_row = lambda a: a.reshape(N_SLABS, 1, SLAB_STATE)
    return {
        "wb": jnp.concatenate([blockdiag_in(bbr), blockdiag_in(bbi)], axis=2).astype(BF16),
        "wc": jnp.concatenate([blockdiag_out(c_re), -blockdiag_out(c_im)], axis=1).astype(BF16),
        "a": jnp.concatenate([slab_row(abr), slab_row(abi)], axis=2),
        "d": d.reshape(N_SLABS, 1, LANES),
    }


def _rope_tables(pos):
    half = ROPE_DIM // 2
    inv = ROPE_BASE ** (-2.0 * jnp.arange(half, dtype=F32) / ROPE_DIM)
    ang = pos.astype(F32)[:, None] * inv[None, :]
    cos = jnp.cos(ang)
    sin = jnp.sin(ang)
    return (_pad_lanes(jnp.concatenate([cos, cos], axis=1)),
            _pad_lanes(jnp.concatenate([-sin, sin], axis=1)))


def kernel(x_prompt, x_sample, cache_ckv, cache_krope, state_ssm_re, state_ssm_im, page_table, w_in, g_q, w_uq, w_uk, g_kv, w_uv, ssm_a_re, ssm_a_im, ssm_log_step, ssm_b_re, ssm_b_im, ssm_c_re, ssm_c_im, ssm_d, w_glu, b_glu, w_out, ln1_g, ln1_b, w_gate, w_up, w_down, ln2_g, ln2_b):
    depth = w_in.shape[0]
    alpha = (2 * depth) ** 0.25
    bsz, seq, _ = x_prompt.shape
    dbsz, dseq, _ = x_sample.shape
    past = page_table.shape[1] * PAGE_SIZE
    cos_p, sin_p = _rope_tables(jnp.arange(seq, dtype=jnp.int32))
    cos_p = jnp.tile(cos_p, (bsz, 1))
    sin_p = jnp.tile(sin_p, (bsz, 1))
    cos_s, sin_s = _rope_tables(past + jnp.arange(dseq, dtype=jnp.int32))
    cos_s = jnp.tile(cos_s, (dbsz, 1))
    sin_s = jnp.tile(sin_s, (dbsz, 1))
    seg_len = seq // SUBLANES

    y_p = x_prompt.reshape(bsz * seq, D_MODEL)
    y_s = x_sample.reshape(dbsz * dseq, D_MODEL)
    outs = [[] for _ in range(8)]
    for l in range(depth):
        wts = _layer_weights(w_in[l], g_q[l], w_uq[l], w_uk[l], g_kv[l], w_uv[l], w_glu[l], b_glu[l],
                             w_out[l], ln1_g[l], ln1_b[l], w_gate[l], w_up[l], w_down[l], ln2_g[l], ln2_b[l])
        s5w = _s5_weights(ssm_a_re[l], ssm_a_im[l], ssm_log_step[l], ssm_b_re[l], ssm_b_im[l],
                          ssm_c_re[l], ssm_c_im[l], ssm_d[l])

        u, ckv, kr, ckvb, krb, ql_t, qr_t = _inproj(y_p, cos_p, sin_p, wts, tm=ATTN_TQ, q_transposed=True)
        u_perm = u.reshape(bsz, SUBLANES, seg_len, SSM_WIDTH).transpose(0, 2, 1, 3)
        g_perm, hl = _s5_prompt(u_perm.reshape(bsz, seq, SSM_WIDTH), s5w)
        g = g_perm.reshape(bsz, seg_len, SUBLANES, SSM_WIDTH).transpose(0, 2, 1, 3)
        kc4 = ckvb.reshape(bsz, seq // ATTN_TK, ATTN_TK, KV_LORA)
        y_att = _prompt_attention(ql_t, qr_t, kc4, krb.reshape(bsz, seq // ATTN_TK, ATTN_TK, LANES),
                                  kc4.transpose(0, 1, 3, 2), wts["w_uvT"], bsz, seq, tq=ATTN_TQ, tk=ATTN_TK)
        x1 = _mix(g.reshape(bsz * seq, SSM_WIDTH), y_att, y_p, wts, alpha, tm=256)
        y_p = _ffn(x1, wts, alpha, tm=512, tf=512)
        h_last = hl[:, :, SUBLANES - 1, :]
        outs[0].append(ckv.reshape(bsz, seq, KV_LORA))
        outs[1].append(kr.reshape(bsz, seq, ROPE_DIM))
        outs[2].append(h_last[:, :, :SLAB_STATE].reshape(bsz, SSM_GROUPS, SSM_STATE))
        outs[3].append(h_last[:, :, SLAB_STATE:].reshape(bsz, SSM_GROUPS, SSM_STATE))

        u, ckv, kr, _, _, ql, qr = _inproj(y_s, cos_s, sin_s, wts, tm=256, q_transposed=False)
        u_perm = u.reshape(dbsz, dseq, SSM_WIDTH).transpose(1, 0, 2).reshape(dseq * dbsz, SSM_WIDTH)
        g_perm, hl_re, hl_im = _s5_sample(
            u_perm, state_ssm_re[l].reshape(dbsz, SSM_GROUPS * SSM_STATE),
            state_ssm_im[l].reshape(dbsz, SSM_GROUPS * SSM_STATE), s5w, dseq)
        g = g_perm.reshape(dseq, dbsz, SSM_WIDTH).transpose(1, 0, 2).reshape(dbsz * dseq, SSM_WIDTH)
        per_batch = lambda q: q.reshape(N_HEADS, dbsz, dseq, q.shape[-1]).transpose(1, 0, 2, 3).reshape(
            dbsz, N_HEADS * dseq, q.shape[-1])
        y_att = _sample_attention(page_table, per_batch(ql), per_batch(qr), ckv, kr,
                                  cache_ckv[l], jnp.swapaxes(cache_krope[l], 1, 2), wts["w_uv"], dseq,
                                  pages_per_step=SAMPLE_PAGES_PER_STEP)
        x1 = _mix(g, y_att, y_s, wts, alpha, tm=256)
        y_s = _ffn(x1, wts, alpha, tm=512, tf=512)
        outs[4].append(ckv.reshape(dbsz, dseq, KV_LORA))
        outs[5].append(kr.reshape(dbsz, dseq, ROPE_DIM))
        outs[6].append(hl_re.reshape(dbsz, SSM_GROUPS, SSM_STATE))
        outs[7].append(hl_im.reshape(dbsz, SSM_GROUPS, SSM_STATE))

    return (y_p.reshape(bsz, seq, D_MODEL), y_s.reshape(dbsz, dseq, D_MODEL),
            *[jnp.stack(o) for o in outs])
```

```python
import functools
import math

import jax
import jax.numpy as jnp
from jax import lax
from jax.experimental import pallas as pl
from jax.experimental.pallas import tpu as pltpu

F32 = jnp.float32
BF16 = jnp.bfloat16

D_MODEL = 2048
SSM_WIDTH = 1024
SSM_CH = 16
SSM_GROUPS = 64
SSM_STATE = 64
N_HEADS = 8
NOPE_DIM = 128
ROPE_DIM = 64
V_DIM = 128
Q_LORA = 512
KV_LORA = 512
ROPE_BASE = 10000.0
SCALE = (NOPE_DIM + ROPE_DIM) ** -0.5
PAGE_SIZE = 128
LANES = 128
SUBLANES = 8
SLAB_GROUPS = LANES // SSM_CH
N_SLABS = SSM_GROUPS // SLAB_GROUPS
SLAB_STATE = SLAB_GROUPS * SSM_STATE
VMEM_LIMIT = 56 * 1024 * 1024
ATTN_TQ = 256
ATTN_TK = 512
ATTN_HEAD_GROUP = 2
SAMPLE_PAGES_PER_STEP = 16
ROW_TM = 256
FFN_TM = 512
FFN_TF = 512


def _dot(a, b):
    return jnp.dot(a, b, preferred_element_type=F32)


def _dot_nt(a, b):
    return lax.dot_general(a, b, (((1,), (1,)), ((), ())), preferred_element_type=F32)


def _const_spec(shape):
    n = len(shape)
    return pl.BlockSpec(shape, lambda *_: (0,) * n, pipeline_mode=pl.Buffered(1))


def _params(sem):
    return pltpu.CompilerParams(dimension_semantics=sem, vmem_limit_bytes=VMEM_LIMIT)


def _rms(x, g, eps=1e-6):
    return x * lax.rsqrt(jnp.mean(x * x, axis=-1, keepdims=True) + eps) * g


def _layer_norm(x, g, b, eps=1e-5):
    mu = jnp.mean(x, axis=-1, keepdims=True)
    xc = x - mu
    var = jnp.mean(xc * xc, axis=-1, keepdims=True)
    return xc * lax.rsqrt(var + eps) * g + b


def _inproj_kernel(x_ref, wm_ref, wkr_ref, gq_ref, gkv_ref, cos_ref, sin_ref,
                   wqn_ref, wqr_ref, wqs_ref, wuk_ref,
                   u_ref, ckv_ref, kr_ref, ckvb_ref, krb_ref, ql_ref, qr_ref, *, q_transposed):
    xb = x_ref[...].astype(BF16)
    cos = cos_ref[...]
    sin = sin_ref[...]
    proj = _dot(xb, wm_ref[...])
    u_ref[...] = proj[:, :SSM_WIDTH]
    cq = proj[:, SSM_WIDTH:SSM_WIDTH + Q_LORA]
    ckv = _rms(proj[:, SSM_WIDTH + Q_LORA:], gkv_ref[...])
    ckv_ref[...] = ckv
    ckvb_ref[...] = ckv.astype(BF16)
    kr2 = _dot(xb, wkr_ref[...])
    krope = kr2[:, :LANES] * cos + kr2[:, LANES:] * sin
    kr_ref[...] = krope[:, :ROPE_DIM]
    krb_ref[...] = krope.astype(BF16)
    cqn = _rms(cq, gq_ref[...])
    if q_transposed:
        cqt = cqn.T.astype(BF16)
        qn = _dot(wqn_ref[...], cqt)
        qa = _dot(wqr_ref[...], cqt)
        qs = _dot(wqs_ref[...], cqt)
        cos_t = cos.T
        sin_t = sin.T
        tm = cos.shape[0]
        for h in range(N_HEADS):
            sl = slice(h * LANES, (h + 1) * LANES)
            cols = slice(h * tm, (h + 1) * tm)
            ql_ref[:, cols] = _dot(wuk_ref[h], qn[sl].astype(BF16)).astype(BF16)
            qr_ref[:, cols] = (qa[sl] * cos_t + qs[sl] * sin_t).astype(BF16)
    else:
        cqb = cqn.astype(BF16)
        qn = _dot(cqb, wqn_ref[...])
        qa = _dot(cqb, wqr_ref[...])
        qs = _dot(cqb, wqs_ref[...])
        for h in range(N_HEADS):
            sl = slice(h * LANES, (h + 1) * LANES)
            ql_ref[h] = _dot(qn[:, sl].astype(BF16), wuk_ref[h]).astype(BF16)
            qr_ref[h] = (qa[:, sl] * cos + qs[:, sl] * sin).astype(BF16)


def _inproj(x, cos, sin, wts, tm, q_transposed):
    t = x.shape[0]
    row = lambda w: pl.BlockSpec((tm, w), lambda i: (i, 0))
    if q_transposed:
        qspec = lambda w: pl.BlockSpec((None, w, N_HEADS * tm), lambda i: (i, 0, 0))
        qshape = lambda w: jax.ShapeDtypeStruct((t // tm, w, N_HEADS * tm), BF16)
        wq = [wts["w_qnT"], wts["w_qrT"], wts["w_qsT"], wts["w_uk"]]
    else:
        qspec = lambda w: pl.BlockSpec((N_HEADS, tm, w), lambda i: (0, i, 0))
        qshape = lambda w: jax.ShapeDtypeStruct((N_HEADS, t, w), BF16)
        wq = [wts["w_qn"], wts["w_qr"], wts["w_qs"], wts["w_ukT"]]
    return pl.pallas_call(
        functools.partial(_inproj_kernel, q_transposed=q_transposed),
        grid=(t // tm,),
        in_specs=[row(D_MODEL),
                  _const_spec((D_MODEL, 2048)), _const_spec((D_MODEL, 2 * LANES)),
                  _const_spec((1, Q_LORA)), _const_spec((1, KV_LORA)),
                  row(LANES), row(LANES)] + [_const_spec(w.shape) for w in wq],
        out_specs=[row(SSM_WIDTH), row(KV_LORA), row(ROPE_DIM), row(KV_LORA), row(LANES),
                   qspec(KV_LORA), qspec(LANES)],
        out_shape=[jax.ShapeDtypeStruct((t, SSM_WIDTH), F32),
                   jax.ShapeDtypeStruct((t, KV_LORA), F32),
                   jax.ShapeDtypeStruct((t, ROPE_DIM), F32),
                   jax.ShapeDtypeStruct((t, KV_LORA), BF16),
                   jax.ShapeDtypeStruct((t, LANES), BF16),
                   qshape(KV_LORA), qshape(LANES)],
        compiler_params=_params(("parallel",)),
        name="inproj",
    )(x, wts["w_main"], wts["w_kr"], wts["g_q"], wts["g_kv"], cos, sin, *wq)


def _s5prep_kernel(are_ref, aim_ref, ls_ref, bre_ref, bim_ref,
                   abr_ref, abi_ref, bbr_ref, bbi_ref):
    lr = are_ref[...]
    li = aim_ref[...]
    delta = jnp.exp(ls_ref[...])
    mag = jnp.exp(lr * delta)
    ar = mag * jnp.cos(li * delta)
    ai = mag * jnp.sin(li * delta)
    abr_ref[...] = ar
    abi_ref[...] = ai
    den = lr * lr + li * li
    fr = ((ar - 1.0) * lr + ai * li) / den
    fi = (ai * lr - (ar - 1.0) * li) / den
    br = bre_ref[...]
    bi = bim_ref[...]
    bbr_ref[...] = fr * br - fi * bi
    bbi_ref[...] = fr * bi + fi * br


def _s5prep(a_re, a_im, log_step, bt_re, bt_im):
    shp = jax.ShapeDtypeStruct((SSM_GROUPS * SSM_CH, SSM_STATE), F32)
    return pl.pallas_call(_s5prep_kernel, out_shape=[shp] * 4, name="s5prep")(
        a_re, a_im, log_step, bt_re, bt_im)


def _cmul_add(ar, ai, hr, hi, br, bi):
    return ar * hr - ai * hi + br, ar * hi + ai * hr + bi


def _s5p_kernel(u_ref, wb_ref, wc_ref, a_ref, d_ref, g_ref, hl_ref, bu_scr, e_scr, *, seg_len, chunk):
    n_rows = SUBLANES * seg_len
    half = SLAB_STATE
    wb = wb_ref[...]
    for c in range(n_rows // chunk):
        rows = pl.ds(c * chunk, chunk)
        bu_scr[rows, :] = _dot(u_ref[rows, :].astype(BF16), wb)
    ar = jnp.broadcast_to(a_ref[:, :half], (SUBLANES, half))
    ai = jnp.broadcast_to(a_ref[:, half:], (SUBLANES, half))

    def step(t, carry, store):
        hr, hi = carry
        rows = pl.ds(pl.multiple_of(t * SUBLANES, SUBLANES), SUBLANES)
        hr, hi = _cmul_add(ar, ai, hr, hi, bu_scr[rows, :half], bu_scr[rows, half:])
        if store:
            bu_scr[rows, :half] = hr
            bu_scr[rows, half:] = hi
        return hr, hi

    zero = jnp.zeros((SUBLANES, half), F32)
    er, ei = lax.fori_loop(0, seg_len, functools.partial(step, store=False), (zero, zero), unroll=8)
    pr, pi = ar, ai
    for _ in range(int(math.log2(seg_len))):
        pr, pi = pr * pr - pi * pi, 2.0 * pr * pi
    e_scr[0:SUBLANES, :half] = er
    e_scr[0:SUBLANES, half:] = ei
    cr = jnp.zeros((1, half), F32)
    ci = jnp.zeros((1, half), F32)
    e_scr[SUBLANES:SUBLANES + 1, :half] = cr
    e_scr[SUBLANES:SUBLANES + 1, half:] = ci
    for s in range(SUBLANES - 1):
        cr, ci = _cmul_add(pr[0:1], pi[0:1], cr, ci, e_scr[s:s + 1, :half], e_scr[s:s + 1, half:])
        e_scr[SUBLANES + s + 1:SUBLANES + s + 2, :half] = cr
        e_scr[SUBLANES + s + 1:SUBLANES + s + 2, half:] = ci
    init = (e_scr[SUBLANES:2 * SUBLANES, :half], e_scr[SUBLANES:2 * SUBLANES, half:])
    hr, hi = lax.fori_loop(0, seg_len, functools.partial(step, store=True), init, unroll=8)
    hl_ref[:, :half] = hr
    hl_ref[:, half:] = hi
    wc = wc_ref[...]
    d = d_ref[...]
    for c in range(n_rows // chunk):
        rows = pl.ds(c * chunk, chunk)
        y = _dot(bu_scr[rows, :].astype(BF16), wc) + d * u_ref[rows, :]
        g_ref[rows, :] = jax.nn.gelu(y)


def _s5_prompt(u_perm, s5w):
    bsz, n_rows, _ = u_perm.shape
    seg_len = n_rows // SUBLANES
    kern = functools.partial(_s5p_kernel, seg_len=seg_len, chunk=512)
    slab = lambda shape: pl.BlockSpec((None,) + shape, lambda b, s: (s, 0, 0))
    return pl.pallas_call(
        kern,
        grid=(bsz, N_SLABS),
        in_specs=[pl.BlockSpec((None, n_rows, LANES), lambda b, s: (b, 0, s)),
                  slab((LANES, 2 * SLAB_STATE)), slab((2 * SLAB_STATE, LANES)),
                  slab((1, 2 * SLAB_STATE)), slab((1, LANES))],
        out_specs=[pl.BlockSpec((None, n_rows, LANES), lambda b, s: (b, 0, s)),
                   pl.BlockSpec((None, None, SUBLANES, 2 * SLAB_STATE), lambda b, s: (b, s, 0, 0))],
        out_shape=[jax.ShapeDtypeStruct((bsz, n_rows, SSM_WIDTH), F32),
                   jax.ShapeDtypeStruct((bsz, N_SLABS, SUBLANES, 2 * SLAB_STATE), F32)],
        scratch_shapes=[pltpu.VMEM((n_rows, 2 * SLAB_STATE), F32),
                        pltpu.VMEM((2 * SUBLANES, 2 * SLAB_STATE), F32)],
        compiler_params=_params(("parallel", "parallel")),
        name="s5_prompt",
    )(u_perm, s5w["wb"], s5w["wc"], s5w["a"], s5w["d"])


def _s5s_kernel(u_ref, wb_ref, wc_ref, a_ref, d_ref, h0r_ref, h0i_ref,
                g_ref, hlr_ref, hli_ref, h_scr, *, n_steps, bsz):
    half = SLAB_STATE
    u = u_ref[...]
    h_scr[...] = _dot(u.astype(BF16), wb_ref[...])
    ar = a_ref[:, :half]
    ai = a_ref[:, half:]
    hr = h0r_ref[...]
    hi = h0i_ref[...]
    for t in range(n_steps):
        rows = pl.ds(t * bsz, bsz)
        hr, hi = _cmul_add(ar, ai, hr, hi, h_scr[rows, :half], h_scr[rows, half:])
        h_scr[rows, :half] = hr
        h_scr[rows, half:] = hi
    hlr_ref[...] = hr
    hli_ref[...] = hi
    y = _dot(h_scr[...].astype(BF16), wc_ref[...]) + d_ref[...] * u
    g_ref[...] = jax.nn.gelu(y)


def _s5_sample(u_perm, h0_re, h0_im, s5w, n_steps):
    n_rows = u_perm.shape[0]
    bsz = n_rows // n_steps
    kern = functools.partial(_s5s_kernel, n_steps=n_steps, bsz=bsz)
    slab = lambda shape: pl.BlockSpec((None,) + shape, lambda s: (s, 0, 0))
    col = lambda r, w: pl.BlockSpec((r, w), lambda s: (0, s))
    return pl.pallas_call(
        kern,
        grid=(N_SLABS,),
        in_specs=[col(n_rows, LANES),
                  slab((LANES, 2 * SLAB_STATE)), slab((2 * SLAB_STATE, LANES)),
                  slab((1, 2 * SLAB_STATE)), slab((1, LANES)),
                  col(bsz, SLAB_STATE), col(bsz, SLAB_STATE)],
        out_specs=[col(n_rows, LANES), col(bsz, SLAB_STATE), col(bsz, SLAB_STATE)],
        out_shape=[jax.ShapeDtypeStruct((n_rows, SSM_WIDTH), F32),
                   jax.ShapeDtypeStruct((bsz, SSM_GROUPS * SSM_STATE), F32),
                   jax.ShapeDtypeStruct((bsz, SSM_GROUPS * SSM_STATE), F32)],
        scratch_shapes=[pltpu.VMEM((n_rows, 2 * SLAB_STATE), F32)],
        compiler_params=_params(("parallel",)),
        name="s5_sample",
    )(u_perm, s5w["wb"], s5w["wc"], s5w["a"], s5w["d"], h0_re, h0_im)


def _pattn_kernel(ql_ref, qr_ref, kc_ref, kr_ref, vt_ref, wuv_ref, o_ref, m_scr, l_scr, acc_scr, *, tq, tk):
    i = pl.program_id(1)
    gw = ATTN_HEAD_GROUP * tq
    n_groups = N_HEADS // ATTN_HEAD_GROUP
    m_scr[...] = jnp.full(m_scr.shape, -jnp.inf, F32)
    l_scr[...] = jnp.zeros(l_scr.shape, F32)
    acc_scr[...] = jnp.zeros(acc_scr.shape, F32)

    def block(j, masked):
        kc = kc_ref[j]
        kr = kr_ref[j]
        vt = vt_ref[j]
        if masked:
            k_tok = j * tk + lax.broadcasted_iota(jnp.int32, (tk, gw), 0)
            q_tok = i * tq + (lax.broadcasted_iota(jnp.int32, (tk, gw), 1) & (tq - 1))
            keep = k_tok <= q_tok

        def scores(g):
            cols = slice(g * gw, (g + 1) * gw)
            s = (_dot(kc, ql_ref[:, cols]) + _dot(kr, qr_ref[:, cols])) * SCALE
            return jnp.where(keep, s, -jnp.inf) if masked else s

        s_next = scores(0)
        for g in range(n_groups):
            s = s_next
            if g + 1 < n_groups:
                s_next = scores(g + 1)
            cols = slice(g * gw, (g + 1) * gw)
            m_prev = m_scr[:, cols]
            m_new = jnp.maximum(m_prev, jnp.max(s, axis=0, keepdims=True))
            corr = jnp.exp(m_prev - m_new)
            p = jnp.exp(s - m_new)
            l_scr[:, cols] = l_scr[:, cols] * corr + jnp.sum(p, axis=0, keepdims=True)
            acc_scr[:, cols] = acc_scr[:, cols] * corr + _dot(vt, p.astype(BF16))
            m_scr[:, cols] = m_new

    def body(j, carry):
        block(j, False)
        return carry

    j_last = (i * tq + tq - 1) // tk
    lax.fori_loop(0, j_last, body, 0)
    block(j_last, True)
    for h in range(N_HEADS):
        cols = slice(h * tq, (h + 1) * tq)
        o_t = (acc_scr[:, cols] / l_scr[:, cols]).astype(BF16)
        y_t = _dot(wuv_ref[h], o_t)
        o_ref[:, h * V_DIM:(h + 1) * V_DIM] = y_t.T.astype(o_ref.dtype)


def _prompt_attention(ql_t, qr_t, kc, kr, v_t, wuv_t, bsz, seq, tq, tk):
    nq = seq // tq
    nk = seq // tk
    kern = functools.partial(_pattn_kernel, tq=tq, tk=tk)
    per_batch = lambda r, c: pl.BlockSpec((None, nk, r, c), lambda b, i: (b, 0, 0, 0))
    return pl.pallas_call(
        kern,
        grid=(bsz, nq),
        in_specs=[pl.BlockSpec((None, KV_LORA, N_HEADS * tq), lambda b, i: (b * nq + i, 0, 0)),
                  pl.BlockSpec((None, LANES, N_HEADS * tq), lambda b, i: (b * nq + i, 0, 0)),
                  per_batch(tk, KV_LORA), per_batch(tk, LANES), per_batch(KV_LORA, tk),
                  _const_spec((N_HEADS, V_DIM, KV_LORA))],
        out_specs=pl.BlockSpec((tq, N_HEADS * V_DIM), lambda b, i: (b * nq + i, 0)),
        out_shape=jax.ShapeDtypeStruct((bsz * seq, N_HEADS * V_DIM), BF16),
        scratch_shapes=[pltpu.VMEM((1, N_HEADS * tq), F32), pltpu.VMEM((1, N_HEADS * tq), F32),
                        pltpu.VMEM((KV_LORA, N_HEADS * tq), F32)],
        compiler_params=_params(("parallel", "arbitrary")),
        name="prompt_attention",
    )(ql_t, qr_t, kc, kr, v_t, wuv_t)


N_CHAINS = 4


def _sattn_kernel(pt_ref, ql_ref, qr_ref, kn_ref, krn_ref, ck_hbm, krt_hbm, wuv_ref, o_ref,
                  kbuf, rbuf, sems, m_scr, l_scr, acc_scr, kc_scr, *, n_new, n_pages, chunk):
    b = pl.program_id(0)
    n_rows = N_HEADS * n_new
    n_chunks = n_pages // chunk

    def page_copies(pid, slot, k):
        rows = pl.ds(k * PAGE_SIZE, PAGE_SIZE)
        return (pltpu.make_async_copy(ck_hbm.at[pid], kbuf.at[slot, rows, :], sems.at[0, slot]),
                pltpu.make_async_copy(krt_hbm.at[pid], rbuf.at[slot, :, rows], sems.at[1, slot]))

    def start_chunk(row, j, slot):
        for k in range(chunk):
            for cp in page_copies(pt_ref[row, j * chunk + k], slot, k):
                cp.start()

    def wait_chunk(slot):
        for k in range(chunk):
            for cp in page_copies(0, slot, k):
                cp.wait()

    @pl.when(b == 0)
    def _():
        start_chunk(0, 0, 0)

    ql = ql_ref[...]
    qr = qr_ref[:, :ROPE_DIM]

    kc = kn_ref[...].astype(BF16)
    s = (_dot_nt(ql, kc) + _dot_nt(qr, krn_ref[...].astype(BF16))) * SCALE
    q_tok = lax.broadcasted_iota(jnp.int32, (n_rows, n_new), 0) & (n_new - 1)
    k_tok = lax.broadcasted_iota(jnp.int32, (n_rows, n_new), 1)
    s = jnp.where(k_tok <= q_tok, s, -jnp.inf)
    m = jnp.max(s, axis=-1, keepdims=True)
    p = jnp.exp(s - m)
    m_scr[0] = m
    l_scr[0] = jnp.sum(p, axis=-1, keepdims=True)
    acc_scr[0] = _dot(p.astype(BF16), kc)
    for c in range(1, N_CHAINS):
        m_scr[c] = jnp.full((n_rows, 1), -jnp.inf, F32)
        l_scr[c] = jnp.zeros((n_rows, 1), F32)
        acc_scr[c] = jnp.zeros((n_rows, KV_LORA), F32)

    per_chain = chunk // N_CHAINS
    chain_rows = per_chain * PAGE_SIZE

    for j in range(n_chunks):
        slot = j % 2
        if j + 1 < n_chunks:
            start_chunk(b, j + 1, 1 - slot)
        else:
            @pl.when(b + 1 < pl.num_programs(0))
            def _():
                start_chunk(b + 1, 0, 1 - slot)
        wait_chunk(slot)

        def scores(c):
            rows = slice(c * chain_rows, (c + 1) * chain_rows)
            kc_scr[rows, :] = kbuf[slot, rows, :].astype(BF16)
            kc = kc_scr[rows, :]
            kr_t = rbuf[slot, :, rows].astype(BF16)
            return (_dot_nt(ql, kc) + _dot(qr, kr_t)) * SCALE, kc

        nxt = scores(0)
        for c in range(N_CHAINS):
            s, kc = nxt
            if c + 1 < N_CHAINS:
                nxt = scores(c + 1)
            m_prev = m_scr[c]
            m_new = jnp.maximum(m_prev, jnp.max(s, axis=-1, keepdims=True))
            corr = jnp.exp(m_prev - m_new)
            p = jnp.exp(s - m_new)
            l_scr[c] = l_scr[c] * corr + jnp.sum(p, axis=-1, keepdims=True)
            acc_scr[c] = acc_scr[c] * corr + _dot(p.astype(BF16), kc)
            m_scr[c] = m_new

    m = m_scr[0]
    for c in range(1, N_CHAINS):
        m = jnp.maximum(m, m_scr[c])
    l = jnp.zeros((n_rows, 1), F32)
    acc = jnp.zeros((n_rows, KV_LORA), F32)
    for c in range(N_CHAINS):
        w = jnp.exp(m_scr[c] - m)
        l = l + l_scr[c] * w
        acc = acc + acc_scr[c] * w
    o = acc / l
    for h in range(N_HEADS):
        oh = o[h * n_new:(h + 1) * n_new].astype(BF16)
        o_ref[:, h * V_DIM:(h + 1) * V_DIM] = _dot(oh, wuv_ref[h]).astype(o_ref.dtype)


def _sample_attention(page_table, ql, qr, kc_new, kr_new, cache_ckv, cache_kr_t, wuv, n_new, chunk):
    bsz, n_pages = page_table.shape
    n_rows = N_HEADS * n_new
    assert (n_pages // chunk) % 2 == 0 and chunk % N_CHAINS == 0
    kern = functools.partial(_sattn_kernel, n_new=n_new, n_pages=n_pages, chunk=chunk)
    in_specs = [pl.BlockSpec((None, n_rows, KV_LORA), lambda b, pt: (b, 0, 0)),
                pl.BlockSpec((None, n_rows, LANES), lambda b, pt: (b, 0, 0)),
                pl.BlockSpec((n_new, KV_LORA), lambda b, pt: (b, 0)),
                pl.BlockSpec((n_new, ROPE_DIM), lambda b, pt: (b, 0)),
                pl.BlockSpec(memory_space=pl.ANY), pl.BlockSpec(memory_space=pl.ANY),
                pl.BlockSpec((N_HEADS, KV_LORA, V_DIM), lambda b, pt: (0, 0, 0))]
    return pl.pallas_call(
        kern,
        grid_spec=pltpu.PrefetchScalarGridSpec(
            num_scalar_prefetch=1,
            grid=(bsz,),
            in_specs=in_specs,
            out_specs=pl.BlockSpec((n_new, N_HEADS * V_DIM), lambda b, pt: (b, 0)),
            scratch_shapes=[pltpu.VMEM((2, chunk * PAGE_SIZE, KV_LORA), F32),
                            pltpu.VMEM((2, ROPE_DIM, chunk * PAGE_SIZE), F32),
                            pltpu.SemaphoreType.DMA((2, 2)),
                            pltpu.VMEM((N_CHAINS, n_rows, 1), F32), pltpu.VMEM((N_CHAINS, n_rows, 1), F32),
                            pltpu.VMEM((N_CHAINS, n_rows, KV_LORA), F32),
                            pltpu.VMEM((chunk * PAGE_SIZE, KV_LORA), BF16)]),
        out_shape=jax.ShapeDtypeStruct((bsz * n_new, N_HEADS * V_DIM), F32),
        compiler_params=_params(("arbitrary",)),
        name="sample_attention",
    )(page_table, ql, qr, kc_new, kr_new, cache_ckv, cache_kr_t, wuv)


def _mix_kernel(g_ref, ya_ref, x_ref, wglu_ref, bglu_ref, wout_ref, lg_ref, lb_ref, o_ref, *, alpha):
    g = g_ref[...]
    z = _dot(g.astype(BF16), wglu_ref[...]) + bglu_ref[...]
    y_ssm = g * jax.nn.sigmoid(z)
    mix = (_dot(y_ssm.astype(BF16), wout_ref[:SSM_WIDTH, :])
           + _dot(ya_ref[...].astype(BF16), wout_ref[SSM_WIDTH:, :]))
    o_ref[...] = _layer_norm(alpha * x_ref[...] + mix, lg_ref[...], lb_ref[...])


def _mix(g, y_att, x, wts, alpha, tm):
    t = x.shape[0]
    row = lambda w: pl.BlockSpec((tm, w), lambda i: (i, 0))
    return pl.pallas_call(
        functools.partial(_mix_kernel, alpha=alpha),
        grid=(t // tm,),
        in_specs=[row(SSM_WIDTH), row(N_HEADS * V_DIM), row(D_MODEL),
                  _const_spec((SSM_WIDTH, SSM_WIDTH)), _const_spec((1, SSM_WIDTH)),
                  _const_spec((D_MODEL, D_MODEL)), _const_spec((1, D_MODEL)), _const_spec((1, D_MODEL))],
        out_specs=row(D_MODEL),
        out_shape=jax.ShapeDtypeStruct((t, D_MODEL), F32),
        compiler_params=_params(("parallel",)),
        name="mix",
    )(g, y_att, x, wts["w_glu"], wts["b_glu"], wts["w_out"], wts["ln1_g"], wts["ln1_b"])


def _ffn_kernel(x_ref, wg_ref, wu_ref, wd_ref, lg_ref, lb_ref, o_ref, xb_scr, *, alpha):
    f = pl.program_id(1)

    @pl.when(f == 0)
    def _():
        xb_scr[...] = x_ref[...].astype(BF16)
        o_ref[...] = jnp.zeros(o_ref.shape, F32)

    xb = xb_scr[...]
    gate = _dot(xb, wg_ref[...])
    up = _dot(xb, wu_ref[...])
    o_ref[...] += _dot((jax.nn.silu(gate) * up).astype(BF16), wd_ref[...])

    @pl.when(f == pl.num_programs(1) - 1)
    def _():
        o_ref[...] = _layer_norm(alpha * x_ref[...] + o_ref[...], lg_ref[...], lb_ref[...])


def _ffn(x, wts, alpha, tm):
    t = x.shape[0]
    n_f, _, tf = wts["w_gate"].shape
    return pl.pallas_call(
        functools.partial(_ffn_kernel, alpha=alpha),
        grid=(t // tm, n_f),
        in_specs=[pl.BlockSpec((tm, D_MODEL), lambda i, f: (i, 0)),
                  pl.BlockSpec((None, D_MODEL, tf), lambda i, f: (f, 0, 0)),
                  pl.BlockSpec((None, D_MODEL, tf), lambda i, f: (f, 0, 0)),
                  pl.BlockSpec((tf, D_MODEL), lambda i, f: (f, 0)),
                  pl.BlockSpec((1, D_MODEL), lambda i, f: (0, 0)),
                  pl.BlockSpec((1, D_MODEL), lambda i, f: (0, 0))],
        out_specs=pl.BlockSpec((tm, D_MODEL), lambda i, f: (i, 0)),
        out_shape=jax.ShapeDtypeStruct((t, D_MODEL), F32),
        scratch_shapes=[pltpu.VMEM((tm, D_MODEL), BF16)],
        compiler_params=_params(("parallel", "arbitrary")),
        name="ffn",
    )(x, wts["w_gate"], wts["w_up"], wts["w_down"], wts["ln2_g"], wts["ln2_b"])


def _swap_halves(w):
    half = w.shape[-1] // 2
    return jnp.concatenate([w[..., half:], w[..., :half]], axis=-1)


def _pad_lanes(w):
    return jnp.concatenate([w, jnp.zeros(w.shape[:-1] + (LANES - w.shape[-1],), w.dtype)], axis=-1)


def _col_tiles(w, tile):
    rows, cols = w.shape
    return w.reshape(rows, cols // tile, tile).transpose(1, 0, 2)


def _layer_weights(w_in, g_q, w_uq, w_uk, g_kv, w_uv, w_glu, b_glu, w_out, ln1_g, ln1_b,
                   w_gate, w_up, w_down, ln2_g, ln2_b):
    n_main = SSM_WIDTH + Q_LORA + KV_LORA
    w_kr = w_in[:, n_main:]
    uq = w_uq.reshape(Q_LORA, N_HEADS, NOPE_DIM + ROPE_DIM)
    uq_rope = uq[:, :, NOPE_DIM:]
    w_qn = uq[:, :, :NOPE_DIM].reshape(Q_LORA, N_HEADS * NOPE_DIM).astype(BF16)
    w_qr = _pad_lanes(uq_rope).reshape(Q_LORA, N_HEADS * LANES).astype(BF16)
    w_qs = _pad_lanes(_swap_halves(uq_rope)).reshape(Q_LORA, N_HEADS * LANES).astype(BF16)
    return {
        "w_main": w_in[:, :n_main].astype(BF16),
        "w_kr": jnp.concatenate([_pad_lanes(w_kr), _pad_lanes(_swap_halves(w_kr))], axis=1).astype(BF16),
        "g_q": g_q.reshape(1, Q_LORA), "g_kv": g_kv.reshape(1, KV_LORA),
        "w_qn": w_qn, "w_qr": w_qr, "w_qs": w_qs,
        "w_qnT": w_qn.T, "w_qrT": w_qr.T, "w_qsT": w_qs.T,
        "w_ukT": jnp.transpose(w_uk, (1, 2, 0)).astype(BF16),
        "w_uk": jnp.transpose(w_uk, (1, 0, 2)).astype(BF16),
        "w_uv": jnp.transpose(w_uv, (1, 0, 2)).astype(BF16),
        "w_uvT": jnp.transpose(w_uv, (1, 2, 0)).astype(BF16),
        "w_glu": w_glu.astype(BF16), "b_glu": b_glu.reshape(1, SSM_WIDTH),
        "w_out": w_out.astype(BF16),
        "ln1_g": ln1_g.reshape(1, D_MODEL), "ln1_b": ln1_b.reshape(1, D_MODEL),
        "w_gate": _col_tiles(w_gate, FFN_TF).astype(BF16), "w_up": _col_tiles(w_up, FFN_TF).astype(BF16),
        "w_down": w_down.astype(BF16),
        "ln2_g": ln2_g.reshape(1, D_MODEL), "ln2_b": ln2_b.reshape(1, D_MODEL),
    }


def _s5_weights(a_re, a_im, log_step, b_re, b_im, c_re, c_im, d):
    rep = lambda a: jnp.repeat(a, SSM_CH, axis=0)
    bt = lambda b: jnp.transpose(b, (0, 2, 1)).reshape(SSM_GROUPS * SSM_CH, SSM_STATE)
    abr, abi, bbr, bbi = _s5prep(rep(a_re), rep(a_im), rep(log_step.reshape(SSM_GROUPS, 1)),
                                 bt(b_re), bt(b_im))
    abr = abr[::SSM_CH]
    abi = abi[::SSM_CH]
    eye = jnp.eye(SLAB_GROUPS, dtype=F32)

    def blockdiag_in(b):
        b4 = b.reshape(N_SLABS, SLAB_GROUPS, SSM_CH, SSM_STATE)
        return jnp.einsum("sghp,gk->sghkp", b4, eye).reshape(N_SLABS, LANES, SLAB_STATE)

    def blockdiag_out(c):
        c4 = c.reshape(N_SLABS, SLAB_GROUPS, SSM_CH, SSM_STATE)
        return jnp.einsum("sghp,gk->skpgh", c4, eye).reshape(N_SLABS, SLAB_STATE, LANES)

    slab_row = lambda a: a.reshape(N_SLABS, 1, SLAB_STATE)
    return {
        "wb": jnp.concatenate([blockdiag_in(bbr), blockdiag_in(bbi)], axis=2).astype(BF16),
        "wc": jnp.concatenate([blockdiag_out(c_re), -blockdiag_out(c_im)], axis=1).astype(BF16),
        "a": jnp.concatenate([slab_row(abr), slab_row(abi)], axis=2),
        "d": d.reshape(N_SLABS, 1, LANES),
    }


def _rope_tables(pos):
    half = ROPE_DIM // 2
    inv = ROPE_BASE ** (-2.0 * jnp.arange(half, dtype=F32) / ROPE_DIM)
    ang = pos.astype(F32)[:, None] * inv[None, :]
    cos = jnp.cos(ang)
    sin = jnp.sin(ang)
    return (_pad_lanes(jnp.concatenate([cos, cos], axis=1)),
            _pad_lanes(jnp.concatenate([-sin, sin], axis=1)))


def kernel(x_prompt, x_sample, cache_ckv, cache_krope, state_ssm_re, state_ssm_im, page_table, w_in, g_q, w_uq, w_uk, g_kv, w_uv, ssm_a_re, ssm_a_im, ssm_log_step, ssm_b_re, ssm_b_im, ssm_c_re, ssm_c_im, ssm_d, w_glu, b_glu, w_out, ln1_g, ln1_b, w_gate, w_up, w_down, ln2_g, ln2_b):
    depth = w_in.shape[0]
    alpha = (2 * depth) ** 0.25
    bsz, seq, _ = x_prompt.shape
    dbsz, dseq, _ = x_sample.shape
    past = page_table.shape[1] * PAGE_SIZE
    cos_p, sin_p = _rope_tables(jnp.arange(seq, dtype=jnp.int32))
    cos_p = jnp.tile(cos_p, (bsz, 1))
    sin_p = jnp.tile(sin_p, (bsz, 1))
    cos_s, sin_s = _rope_tables(past + jnp.arange(dseq, dtype=jnp.int32))
    cos_s = jnp.tile(cos_s, (dbsz, 1))
    sin_s = jnp.tile(sin_s, (dbsz, 1))
    seg_len = seq // SUBLANES

    y_p = x_prompt.reshape(bsz * seq, D_MODEL)
    y_s = x_sample.reshape(dbsz * dseq, D_MODEL)
    outs = [[] for _ in range(8)]
    for l in range(depth):
        wts = _layer_weights(w_in[l], g_q[l], w_uq[l], w_uk[l], g_kv[l], w_uv[l], w_glu[l], b_glu[l],
                             w_out[l], ln1_g[l], ln1_b[l], w_gate[l], w_up[l], w_down[l], ln2_g[l], ln2_b[l])
        s5w = _s5_weights(ssm_a_re[l], ssm_a_im[l], ssm_log_step[l], ssm_b_re[l], ssm_b_im[l],
                          ssm_c_re[l], ssm_c_im[l], ssm_d[l])

        u, ckv, kr, ckvb, krb, ql_t, qr_t = _inproj(y_p, cos_p, sin_p, wts, tm=ATTN_TQ, q_transposed=True)
        u_perm = u.reshape(bsz, SUBLANES, seg_len, SSM_WIDTH).transpose(0, 2, 1, 3)
        g_perm, hl = _s5_prompt(u_perm.reshape(bsz, seq, SSM_WIDTH), s5w)
        g = g_perm.reshape(bsz, seg_len, SUBLANES, SSM_WIDTH).transpose(0, 2, 1, 3)
        kc4 = ckvb.reshape(bsz, seq // ATTN_TK, ATTN_TK, KV_LORA)
        y_att = _prompt_attention(ql_t, qr_t, kc4, krb.reshape(bsz, seq // ATTN_TK, ATTN_TK, LANES),
                                  kc4.transpose(0, 1, 3, 2), wts["w_uvT"], bsz, seq, tq=ATTN_TQ, tk=ATTN_TK)
        x1 = _mix(g.reshape(bsz * seq, SSM_WIDTH), y_att, y_p, wts, alpha, tm=ROW_TM)
        y_p = _ffn(x1, wts, alpha, tm=FFN_TM)
        h_last = hl[:, :, SUBLANES - 1, :]
        outs[0].append(ckv.reshape(bsz, seq, KV_LORA))
        outs[1].append(kr.reshape(bsz, seq, ROPE_DIM))
        outs[2].append(h_last[:, :, :SLAB_STATE].reshape(bsz, SSM_GROUPS, SSM_STATE))
        outs[3].append(h_last[:, :, SLAB_STATE:].reshape(bsz, SSM_GROUPS, SSM_STATE))

        u, ckv, kr, _, _, ql, qr = _inproj(y_s, cos_s, sin_s, wts, tm=ROW_TM, q_transposed=False)
        u_perm = u.reshape(dbsz, dseq, SSM_WIDTH).transpose(1, 0, 2).reshape(dseq * dbsz, SSM_WIDTH)
        g_perm, hl_re, hl_im = _s5_sample(
            u_perm, state_ssm_re[l].reshape(dbsz, SSM_GROUPS * SSM_STATE),
            state_ssm_im[l].reshape(dbsz, SSM_GROUPS * SSM_STATE), s5w, dseq)
        g = g_perm.reshape(dseq, dbsz, SSM_WIDTH).transpose(1, 0, 2).reshape(dbsz * dseq, SSM_WIDTH)
        per_batch = lambda q: q.reshape(N_HEADS, dbsz, dseq, q.shape[-1]).transpose(1, 0, 2, 3).reshape(
            dbsz, N_HEADS * dseq, q.shape[-1])
        y_att = _sample_attention(page_table, per_batch(ql), per_batch(qr), ckv, kr,
                                  cache_ckv[l], jnp.swapaxes(cache_krope[l], 1, 2), wts["w_uv"], dseq,
                                  chunk=SAMPLE_PAGES_PER_STEP)
        x1 = _mix(g, y_att, y_s, wts, alpha, tm=ROW_TM)
        y_s = _ffn(x1, wts, alpha, tm=FFN_TM)
        outs[4].append(ckv.reshape(dbsz, dseq, KV_LORA))
        outs[5].append(kr.reshape(dbsz, dseq, ROPE_DIM))
        outs[6].append(hl_re.reshape(dbsz, SSM_GROUPS, SSM_STATE))
        outs[7].append(hl_im.reshape(dbsz, SSM_GROUPS, SSM_STATE))

    return (y_p.reshape(bsz, seq, D_MODEL), y_s.reshape(dbsz, dseq, D_MODEL),
            *[jnp.stack(o) for o in outs])
```

```python
import functools
import math

import jax
import jax.numpy as jnp
from jax import lax
from jax.experimental import pallas as pl
from jax.experimental.pallas import tpu as pltpu

F32 = jnp.float32
BF16 = jnp.bfloat16

D_MODEL = 2048
SSM_WIDTH = 1024
SSM_CH = 16
SSM_GROUPS = 64
SSM_STATE = 64
N_HEADS = 8
NOPE_DIM = 128
ROPE_DIM = 64
V_DIM = 128
Q_LORA = 512
KV_LORA = 512
ROPE_BASE = 10000.0
SCALE = (NOPE_DIM + ROPE_DIM) ** -0.5
PAGE_SIZE = 128
LANES = 128
SUBLANES = 8
SLAB_GROUPS = LANES // SSM_CH
N_SLABS = SSM_GROUPS // SLAB_GROUPS
SLAB_STATE = SLAB_GROUPS * SSM_STATE
VMEM_LIMIT = 56 * 1024 * 1024
ATTN_TQ = 256
ATTN_TK = 512
ATTN_HEAD_GROUP = 2
ATTN_LOOKAHEAD = 2
SAMPLE_PAGES_PER_STEP = 16
SAMPLE_LOOKAHEAD = 3
SAMPLE_PREFETCH = 3
ROW_TM = 256
FFN_TM = 512
FFN_TF = 512


def _dot(a, b):
    return jnp.dot(a, b, preferred_element_type=F32)


def _dot_nt(a, b):
    return lax.dot_general(a, b, (((1,), (1,)), ((), ())), preferred_element_type=F32)


def _const_spec(shape):
    n = len(shape)
    return pl.BlockSpec(shape, lambda *_: (0,) * n, pipeline_mode=pl.Buffered(1))


def _params(sem):
    return pltpu.CompilerParams(dimension_semantics=sem, vmem_limit_bytes=VMEM_LIMIT)


def _rms(x, g, eps=1e-6):
    return x * lax.rsqrt(jnp.mean(x * x, axis=-1, keepdims=True) + eps) * g


def _layer_norm(x, g, b, eps=1e-5):
    mu = jnp.mean(x, axis=-1, keepdims=True)
    xc = x - mu
    var = jnp.mean(xc * xc, axis=-1, keepdims=True)
    return xc * lax.rsqrt(var + eps) * g + b


def _inproj_kernel(x_ref, wm_ref, wkr_ref, gq_ref, gkv_ref, cos_ref, sin_ref,
                   wqn_ref, wqr_ref, wqs_ref, wuk_ref,
                   u_ref, ckv_ref, kr_ref, ckvb_ref, krb_ref, ql_ref, qr_ref, *maybe_vt_ref, q_transposed):
    xb = x_ref[...].astype(BF16)
    cos = cos_ref[...]
    sin = sin_ref[...]
    proj = _dot(xb, wm_ref[...])
    u_ref[...] = proj[:, :SSM_WIDTH]
    cq = proj[:, SSM_WIDTH:SSM_WIDTH + Q_LORA]
    ckv = _rms(proj[:, SSM_WIDTH + Q_LORA:], gkv_ref[...])
    ckv_ref[...] = ckv
    ckvb_ref[...] = ckv.astype(BF16)
    kr2 = _dot(xb, wkr_ref[...])
    krope = kr2[:, :LANES] * cos + kr2[:, LANES:] * sin
    kr_ref[...] = krope[:, :ROPE_DIM]
    krb_ref[...] = krope.astype(BF16)
    cqn = _rms(cq, gq_ref[...])
    if q_transposed:
        maybe_vt_ref[0][...] = ckv.T.astype(BF16)
        cqt = cqn.T.astype(BF16)
        qn = _dot(wqn_ref[...], cqt)
        qa = _dot(wqr_ref[...], cqt)
        qs = _dot(wqs_ref[...], cqt)
        cos_t = cos.T
        sin_t = sin.T
        tm = cos.shape[0]
        for h in range(N_HEADS):
            sl = slice(h * LANES, (h + 1) * LANES)
            cols = slice(h * tm, (h + 1) * tm)
            ql_ref[:, cols] = _dot(wuk_ref[h], qn[sl].astype(BF16)).astype(BF16)
            qr_ref[:, cols] = (qa[sl] * cos_t + qs[sl] * sin_t).astype(BF16)
    else:
        cqb = cqn.astype(BF16)
        qn = _dot(cqb, wqn_ref[...])
        qa = _dot(cqb, wqr_ref[...])
        qs = _dot(cqb, wqs_ref[...])
        for h in range(N_HEADS):
            sl = slice(h * LANES, (h + 1) * LANES)
            ql_ref[h] = _dot(qn[:, sl].astype(BF16), wuk_ref[h]).astype(BF16)
            qr_ref[h] = (qa[:, sl] * cos + qs[:, sl] * sin).astype(BF16)


def _segment_major_spec(tm, seq, width):
    tiles_per_batch = seq // tm
    tiles_per_seg = seq // SUBLANES // tm
    return pl.BlockSpec((None, tm, width),
                        lambda i: (i // tiles_per_batch, i % tiles_per_seg, (i % tiles_per_batch) // tiles_per_seg))


def _inproj(x, cos, sin, wts, tm, prompt_seq=None):
    t = x.shape[0]
    row = lambda w: pl.BlockSpec((tm, w), lambda i: (i, 0))
    u_spec = row(SSM_WIDTH)
    u_shape = jax.ShapeDtypeStruct((t, SSM_WIDTH), F32)
    extra_specs, extra_shapes = [], []
    if prompt_seq is not None:
        assert tm == ATTN_TQ and ATTN_TK % tm == 0 and (prompt_seq // SUBLANES) % tm == 0
        u_spec = _segment_major_spec(tm, prompt_seq, SSM_WIDTH)
        u_shape = jax.ShapeDtypeStruct((t // prompt_seq, prompt_seq // SUBLANES, SUBLANES * SSM_WIDTH), F32)
        qspec = lambda w: pl.BlockSpec((None, w, N_HEADS * tm), lambda i: (i, 0, 0))
        qshape = lambda w: jax.ShapeDtypeStruct((t // tm, w, N_HEADS * tm), BF16)
        wq = [wts["w_qnT"], wts["w_qrT"], wts["w_qsT"], wts["w_uk"]]
        per_tk = ATTN_TK // tm
        extra_specs = [pl.BlockSpec((None, KV_LORA, tm), lambda i: (i // per_tk, 0, i % per_tk))]
        extra_shapes = [jax.ShapeDtypeStruct((t // ATTN_TK, KV_LORA, ATTN_TK), BF16)]
    else:
        qspec = lambda w: pl.BlockSpec((N_HEADS, tm, w), lambda i: (0, i, 0))
        qshape = lambda w: jax.ShapeDtypeStruct((N_HEADS, t, w), BF16)
        wq = [wts["w_qn"], wts["w_qr"], wts["w_qs"], wts["w_ukT"]]
    return pl.pallas_call(
        functools.partial(_inproj_kernel, q_transposed=prompt_seq is not None),
        grid=(t // tm,),
        in_specs=[row(D_MODEL),
                  _const_spec((D_MODEL, 2048)), _const_spec((D_MODEL, 2 * LANES)),
                  _const_spec((1, Q_LORA)), _const_spec((1, KV_LORA)),
                  row(LANES), row(LANES)] + [_const_spec(w.shape) for w in wq],
        out_specs=[u_spec, row(KV_LORA), row(ROPE_DIM), row(KV_LORA), row(LANES),
                   qspec(KV_LORA), qspec(LANES)] + extra_specs,
        out_shape=[u_shape,
                   jax.ShapeDtypeStruct((t, KV_LORA), F32),
                   jax.ShapeDtypeStruct((t, ROPE_DIM), F32),
                   jax.ShapeDtypeStruct((t, KV_LORA), BF16),
                   jax.ShapeDtypeStruct((t, LANES), BF16),
                   qshape(KV_LORA), qshape(LANES)] + extra_shapes,
        compiler_params=_params(("parallel",)),
        name="inproj",
    )(x, wts["w_main"], wts["w_kr"], wts["g_q"], wts["g_kv"], cos, sin, *wq)


def _s5prep_kernel(are_ref, aim_ref, ls_ref, bre_ref, bim_ref,
                   abr_ref, abi_ref, bbr_ref, bbi_ref):
    lr = are_ref[...]
    li = aim_ref[...]
    delta = jnp.exp(ls_ref[...])
    mag = jnp.exp(lr * delta)
    ar = mag * jnp.cos(li * delta)
    ai = mag * jnp.sin(li * delta)
    abr_ref[...] = ar
    abi_ref[...] = ai
    den = lr * lr + li * li
    fr = ((ar - 1.0) * lr + ai * li) / den
    fi = (ai * lr - (ar - 1.0) * li) / den
    br = bre_ref[...]
    bi = bim_ref[...]
    bbr_ref[...] = fr * br - fi * bi
    bbi_ref[...] = fr * bi + fi * br


def _s5prep(a_re, a_im, log_step, bt_re, bt_im):
    shp = jax.ShapeDtypeStruct((SSM_GROUPS * SSM_CH, SSM_STATE), F32)
    return pl.pallas_call(_s5prep_kernel, out_shape=[shp] * 4, name="s5prep")(
        a_re, a_im, log_step, bt_re, bt_im)


def _cmul_add(ar, ai, hr, hi, br, bi):
    return ar * hr - ai * hi + br, ar * hi + ai * hr + bi


def _s5p_kernel(u_ref, wb_ref, wc_ref, a_ref, d_ref, g_ref, hl_ref, bu_scr, e_scr, *, seg_len, chunk):
    n_rows = SUBLANES * seg_len
    half = SLAB_STATE
    wb = wb_ref[...]
    for c in range(n_rows // chunk):
        rows = pl.ds(c * chunk, chunk)
        bu_scr[rows, :] = _dot(u_ref[rows, :].astype(BF16), wb)
    ar = jnp.broadcast_to(a_ref[:, :half], (SUBLANES, half))
    ai = jnp.broadcast_to(a_ref[:, half:], (SUBLANES, half))

    def step(t, carry, store):
        hr, hi = carry
        rows = pl.ds(pl.multiple_of(t * SUBLANES, SUBLANES), SUBLANES)
        hr, hi = _cmul_add(ar, ai, hr, hi, bu_scr[rows, :half], bu_scr[rows, half:])
        if store:
            bu_scr[rows, :half] = hr
            bu_scr[rows, half:] = hi
        return hr, hi

    zero = jnp.zeros((SUBLANES, half), F32)
    er, ei = lax.fori_loop(0, seg_len, functools.partial(step, store=False), (zero, zero), unroll=8)
    pr, pi = ar, ai
    for _ in range(int(math.log2(seg_len))):
        pr, pi = pr * pr - pi * pi, 2.0 * pr * pi
    e_scr[0:SUBLANES, :half] = er
    e_scr[0:SUBLANES, half:] = ei
    cr = jnp.zeros((1, half), F32)
    ci = jnp.zeros((1, half), F32)
    e_scr[SUBLANES:SUBLANES + 1, :half] = cr
    e_scr[SUBLANES:SUBLANES + 1, half:] = ci
    for s in range(SUBLANES - 1):
        cr, ci = _cmul_add(pr[0:1], pi[0:1], cr, ci, e_scr[s:s + 1, :half], e_scr[s:s + 1, half:])
        e_scr[SUBLANES + s + 1:SUBLANES + s + 2, :half] = cr
        e_scr[SUBLANES + s + 1:SUBLANES + s + 2, half:] = ci
    init = (e_scr[SUBLANES:2 * SUBLANES, :half], e_scr[SUBLANES:2 * SUBLANES, half:])
    hr, hi = lax.fori_loop(0, seg_len, functools.partial(step, store=True), init, unroll=8)
    hl_ref[:, :half] = hr
    hl_ref[:, half:] = hi
    wc = wc_ref[...]
    d = d_ref[...]
    for c in range(n_rows // chunk):
        rows = pl.ds(c * chunk, chunk)
        y = _dot(bu_scr[rows, :].astype(BF16), wc) + d * u_ref[rows, :]
        g_ref[rows, :] = jax.nn.gelu(y)


def _s5_prompt(u_perm, s5w):
    bsz, n_rows, _ = u_perm.shape
    seg_len = n_rows // SUBLANES
    kern = functools.partial(_s5p_kernel, seg_len=seg_len, chunk=512)
    slab = lambda shape: pl.BlockSpec((None,) + shape, lambda b, s: (s, 0, 0))
    return pl.pallas_call(
        kern,
        grid=(bsz, N_SLABS),
        in_specs=[pl.BlockSpec((None, n_rows, LANES), lambda b, s: (b, 0, s)),
                  slab((LANES, 2 * SLAB_STATE)), slab((2 * SLAB_STATE, LANES)),
                  slab((1, 2 * SLAB_STATE)), slab((1, LANES))],
        out_specs=[pl.BlockSpec((None, n_rows, LANES), lambda b, s: (b, 0, s)),
                   pl.BlockSpec((None, None, SUBLANES, 2 * SLAB_STATE), lambda b, s: (b, s, 0, 0))],
        out_shape=[jax.ShapeDtypeStruct((bsz, n_rows, SSM_WIDTH), F32),
                   jax.ShapeDtypeStruct((bsz, N_SLABS, SUBLANES, 2 * SLAB_STATE), F32)],
        scratch_shapes=[pltpu.VMEM((n_rows, 2 * SLAB_STATE), F32),
                        pltpu.VMEM((2 * SUBLANES, 2 * SLAB_STATE), F32)],
        compiler_params=_params(("parallel", "parallel")),
        name="s5_prompt",
    )(u_perm, s5w["wb"], s5w["wc"], s5w["a"], s5w["d"])


def _s5s_kernel(u_ref, wb_ref, wc_ref, a_ref, d_ref, h0r_ref, h0i_ref,
                g_ref, hlr_ref, hli_ref, h_scr, *, n_steps, bsz):
    half = SLAB_STATE
    u = u_ref[...]
    h_scr[...] = _dot(u.astype(BF16), wb_ref[...])
    ar = a_ref[:, :half]
    ai = a_ref[:, half:]
    hr = h0r_ref[...]
    hi = h0i_ref[...]
    for t in range(n_steps):
        rows = pl.ds(t * bsz, bsz)
        hr, hi = _cmul_add(ar, ai, hr, hi, h_scr[rows, :half], h_scr[rows, half:])
        h_scr[rows, :half] = hr
        h_scr[rows, half:] = hi
    hlr_ref[...] = hr
    hli_ref[...] = hi
    y = _dot(h_scr[...].astype(BF16), wc_ref[...]) + d_ref[...] * u
    g_ref[...] = jax.nn.gelu(y)


def _s5_sample(u_perm, h0_re, h0_im, s5w, n_steps):
    n_rows = u_perm.shape[0]
    bsz = n_rows // n_steps
    kern = functools.partial(_s5s_kernel, n_steps=n_steps, bsz=bsz)
    slab = lambda shape: pl.BlockSpec((None,) + shape, lambda s: (s, 0, 0))
    col = lambda r, w: pl.BlockSpec((r, w), lambda s: (0, s))
    return pl.pallas_call(
        kern,
        grid=(N_SLABS,),
        in_specs=[col(n_rows, LANES),
                  slab((LANES, 2 * SLAB_STATE)), slab((2 * SLAB_STATE, LANES)),
                  slab((1, 2 * SLAB_STATE)), slab((1, LANES)),
                  col(bsz, SLAB_STATE), col(bsz, SLAB_STATE)],
        out_specs=[col(n_rows, LANES), col(bsz, SLAB_STATE), col(bsz, SLAB_STATE)],
        out_shape=[jax.ShapeDtypeStruct((n_rows, SSM_WIDTH), F32),
                   jax.ShapeDtypeStruct((bsz, SSM_GROUPS * SSM_STATE), F32),
                   jax.ShapeDtypeStruct((bsz, SSM_GROUPS * SSM_STATE), F32)],
        scratch_shapes=[pltpu.VMEM((n_rows, 2 * SLAB_STATE), F32)],
        compiler_params=_params(("parallel",)),
        name="s5_sample",
    )(u_perm, s5w["wb"], s5w["wc"], s5w["a"], s5w["d"], h0_re, h0_im)


def _pattn_kernel(ql_ref, qr_ref, kc_ref, kr_ref, vt_ref, wuv_ref, o_ref, m_scr, l_scr, acc_scr, *, tq, tk):
    i = pl.program_id(1)
    gw = ATTN_HEAD_GROUP * tq
    n_groups = N_HEADS // ATTN_HEAD_GROUP
    m_scr[...] = jnp.full(m_scr.shape, -jnp.inf, F32)
    l_scr[...] = jnp.zeros(l_scr.shape, F32)
    acc_scr[...] = jnp.zeros(acc_scr.shape, F32)

    def block(j, masked):
        kc = kc_ref[j]
        kr = kr_ref[j]
        vt = vt_ref[j]
        if masked:
            k_tok = j * tk + lax.broadcasted_iota(jnp.int32, (tk, gw), 0)
            q_tok = i * tq + (lax.broadcasted_iota(jnp.int32, (tk, gw), 1) & (tq - 1))
            keep = k_tok <= q_tok

        def scores(g):
            cols = slice(g * gw, (g + 1) * gw)
            s = (_dot(kc, ql_ref[:, cols]) + _dot(kr, qr_ref[:, cols])) * SCALE
            return jnp.where(keep, s, -jnp.inf) if masked else s

        pending = [scores(g) for g in range(ATTN_LOOKAHEAD)]
        for g in range(n_groups):
            s = pending.pop(0)
            if g + ATTN_LOOKAHEAD < n_groups:
                pending.append(scores(g + ATTN_LOOKAHEAD))
            cols = slice(g * gw, (g + 1) * gw)
            m_prev = m_scr[:, cols]
            m_new = jnp.maximum(m_prev, jnp.max(s, axis=0, keepdims=True))
            corr = jnp.exp(m_prev - m_new)
            p = jnp.exp(s - m_new)
            l_scr[:, cols] = l_scr[:, cols] * corr + jnp.sum(p, axis=0, keepdims=True)
            acc_scr[:, cols] = acc_scr[:, cols] * corr + _dot(vt, p.astype(BF16))
            m_scr[:, cols] = m_new

    def body(j, carry):
        block(j, False)
        return carry

    j_last = (i * tq + tq - 1) // tk
    lax.fori_loop(0, j_last, body, 0)
    block(j_last, True)
    for h in range(N_HEADS):
        cols = slice(h * tq, (h + 1) * tq)
        o_t = (acc_scr[:, cols] / l_scr[:, cols]).astype(BF16)
        y_t = _dot(wuv_ref[h], o_t)
        o_ref[:, h * V_DIM:(h + 1) * V_DIM] = y_t.T.astype(o_ref.dtype)


def _prompt_attention(ql_t, qr_t, kc, kr, v_t, wuv_t, bsz, seq, tq, tk):
    nq = seq // tq
    nk = seq // tk
    kern = functools.partial(_pattn_kernel, tq=tq, tk=tk)
    per_batch = lambda r, c: pl.BlockSpec((None, nk, r, c), lambda b, i: (b, 0, 0, 0))
    return pl.pallas_call(
        kern,
        grid=(bsz, nq),
        in_specs=[pl.BlockSpec((None, KV_LORA, N_HEADS * tq), lambda b, i: (b * nq + i, 0, 0)),
                  pl.BlockSpec((None, LANES, N_HEADS * tq), lambda b, i: (b * nq + i, 0, 0)),
                  per_batch(tk, KV_LORA), per_batch(tk, LANES), per_batch(KV_LORA, tk),
                  _const_spec((N_HEADS, V_DIM, KV_LORA))],
        out_specs=pl.BlockSpec((tq, N_HEADS * V_DIM), lambda b, i: (b * nq + i, 0)),
        out_shape=jax.ShapeDtypeStruct((bsz * seq, N_HEADS * V_DIM), BF16),
        scratch_shapes=[pltpu.VMEM((1, N_HEADS * tq), F32), pltpu.VMEM((1, N_HEADS * tq), F32),
                        pltpu.VMEM((KV_LORA, N_HEADS * tq), F32)],
        compiler_params=_params(("parallel", "arbitrary")),
        name="prompt_attention",
    )(ql_t, qr_t, kc, kr, v_t, wuv_t)


N_CHAINS = 4


def _sattn_kernel(pt_ref, ql_ref, qr_ref, kn_ref, krn_ref, ck_hbm, krt_hbm, wuv_ref, o_ref,
                  kbuf, rbuf, sems, m_scr, l_scr, acc_scr, kc_scr, *, n_new, n_pages, chunk):
    b = pl.program_id(0)
    n_rows = N_HEADS * n_new
    n_chunks = n_pages // chunk

    def page_copies(pid, slot, k):
        rows = pl.ds(k * PAGE_SIZE, PAGE_SIZE)
        return (pltpu.make_async_copy(ck_hbm.at[pid], kbuf.at[slot, rows, :], sems.at[0, slot]),
                pltpu.make_async_copy(krt_hbm.at[pid], rbuf.at[slot, k], sems.at[1, slot]))

    def start_chunk(row, j):
        for k in range(chunk):
            for cp in page_copies(pt_ref[row, j * chunk + k], j, k):
                cp.start()

    def wait_chunk(j):
        for k in range(chunk):
            for cp in page_copies(0, j, k):
                cp.wait()

    def start_ahead(j):
        ahead = j + SAMPLE_PREFETCH
        if ahead < n_chunks:
            start_chunk(b, ahead)
        else:
            @pl.when(b + 1 < pl.num_programs(0))
            def _():
                start_chunk(b + 1, ahead - n_chunks)

    @pl.when(b == 0)
    def _():
        for j in range(SAMPLE_PREFETCH):
            start_chunk(0, j)

    ql = ql_ref[...]
    qr = qr_ref[:, :ROPE_DIM]

    kc = kn_ref[...].astype(BF16)
    s = (_dot_nt(ql, kc) + _dot_nt(qr, krn_ref[...].astype(BF16))) * SCALE
    q_tok = lax.broadcasted_iota(jnp.int32, (n_rows, n_new), 0) & (n_new - 1)
    k_tok = lax.broadcasted_iota(jnp.int32, (n_rows, n_new), 1)
    s = jnp.where(k_tok <= q_tok, s, -jnp.inf)
    m = jnp.max(s, axis=-1, keepdims=True)
    p = jnp.exp(s - m)
    m_scr[0] = m
    l_scr[0] = jnp.sum(p, axis=-1, keepdims=True)
    acc_scr[0] = _dot(p.astype(BF16), kc)
    for c in range(1, N_CHAINS):
        m_scr[c] = jnp.full((n_rows, 1), -jnp.inf, F32)
        l_scr[c] = jnp.zeros((n_rows, 1), F32)
        acc_scr[c] = jnp.zeros((n_rows, KV_LORA), F32)

    per_chain = chunk // N_CHAINS
    chain_rows = per_chain * PAGE_SIZE

    def scores(t):
        j, c = divmod(t, N_CHAINS)
        if c == 0:
            start_ahead(j)
            wait_chunk(j)
        rows = slice(c * chain_rows, (c + 1) * chain_rows)
        kc_scr[rows, :] = kbuf[j, rows, :].astype(BF16)
        kc = kc_scr[rows, :]
        kr_t = jnp.concatenate([rbuf[j, k] for k in range(c * per_chain, (c + 1) * per_chain)],
                               axis=1).astype(BF16)
        return (_dot_nt(ql, kc) + _dot(qr, kr_t)) * SCALE, kc

    n_blocks = n_chunks * N_CHAINS
    pending = [scores(t) for t in range(SAMPLE_LOOKAHEAD)]
    for t in range(n_blocks):
        s, kc = pending.pop(0)
        if t + SAMPLE_LOOKAHEAD < n_blocks:
            pending.append(scores(t + SAMPLE_LOOKAHEAD))
        c = t % N_CHAINS
        m_prev = m_scr[c]
        m_new = jnp.maximum(m_prev, jnp.max(s, axis=-1, keepdims=True))
        corr = jnp.exp(m_prev - m_new)
        p = jnp.exp(s - m_new)
        l_scr[c] = l_scr[c] * corr + jnp.sum(p, axis=-1, keepdims=True)
        acc_scr[c] = acc_scr[c] * corr + _dot(p.astype(BF16), kc)
        m_scr[c] = m_new

    m = m_scr[0]
    for c in range(1, N_CHAINS):
        m = jnp.maximum(m, m_scr[c])
    l = jnp.zeros((n_rows, 1), F32)
    acc = jnp.zeros((n_rows, KV_LORA), F32)
    for c in range(N_CHAINS):
        w = jnp.exp(m_scr[c] - m)
        l = l + l_scr[c] * w
        acc = acc + acc_scr[c] * w
    o = acc / l
    for h in range(N_HEADS):
        oh = o[h * n_new:(h + 1) * n_new].astype(BF16)
        o_ref[:, h * V_DIM:(h + 1) * V_DIM] = _dot(oh, wuv_ref[h]).astype(o_ref.dtype)


def _sample_attention(page_table, ql, qr, kc_new, kr_new, cache_ckv, cache_kr_t, wuv, n_new, chunk):
    bsz, n_pages = page_table.shape
    n_rows = N_HEADS * n_new
    n_chunks = n_pages // chunk
    assert n_pages % chunk == 0 and chunk % N_CHAINS == 0 and 0 < SAMPLE_PREFETCH < n_chunks
    kern = functools.partial(_sattn_kernel, n_new=n_new, n_pages=n_pages, chunk=chunk)
    in_specs = [pl.BlockSpec((None, n_rows, KV_LORA), lambda b, pt: (b, 0, 0)),
                pl.BlockSpec((None, n_rows, LANES), lambda b, pt: (b, 0, 0)),
                pl.BlockSpec((n_new, KV_LORA), lambda b, pt: (b, 0)),
                pl.BlockSpec((n_new, ROPE_DIM), lambda b, pt: (b, 0)),
                pl.BlockSpec(memory_space=pl.ANY), pl.BlockSpec(memory_space=pl.ANY),
                pl.BlockSpec((N_HEADS, KV_LORA, V_DIM), lambda b, pt: (0, 0, 0))]
    return pl.pallas_call(
        kern,
        grid_spec=pltpu.PrefetchScalarGridSpec(
            num_scalar_prefetch=1,
            grid=(bsz,),
            in_specs=in_specs,
            out_specs=pl.BlockSpec((n_new, N_HEADS * V_DIM), lambda b, pt: (b, 0)),
            scratch_shapes=[pltpu.VMEM((n_chunks, chunk * PAGE_SIZE, KV_LORA), F32),
                            pltpu.VMEM((n_chunks, chunk, ROPE_DIM, PAGE_SIZE), F32),
                            pltpu.SemaphoreType.DMA((2, n_chunks)),
                            pltpu.VMEM((N_CHAINS, n_rows, 1), F32), pltpu.VMEM((N_CHAINS, n_rows, 1), F32),
                            pltpu.VMEM((N_CHAINS, n_rows, KV_LORA), F32),
                            pltpu.VMEM((chunk * PAGE_SIZE, KV_LORA), BF16)]),
        out_shape=jax.ShapeDtypeStruct((bsz * n_new, N_HEADS * V_DIM), F32),
        compiler_params=_params(("arbitrary",)),
        name="sample_attention",
    )(page_table, ql, qr, kc_new, kr_new, cache_ckv, cache_kr_t, wuv)


def _mix_kernel(g_ref, ya_ref, x_ref, wglu_ref, bglu_ref, wout_ref, lg_ref, lb_ref, o_ref, *, alpha):
    g = g_ref[...]
    z = _dot(g.astype(BF16), wglu_ref[...]) + bglu_ref[...]
    y_ssm = g * jax.nn.sigmoid(z)
    mix = (_dot(y_ssm.astype(BF16), wout_ref[:SSM_WIDTH, :])
           + _dot(ya_ref[...].astype(BF16), wout_ref[SSM_WIDTH:, :]))
    o_ref[...] = _layer_norm(alpha * x_ref[...] + mix, lg_ref[...], lb_ref[...])


def _mix(g, y_att, x, wts, alpha, tm, prompt_seq=None):
    t = x.shape[0]
    row = lambda w: pl.BlockSpec((tm, w), lambda i: (i, 0))
    g_spec = row(SSM_WIDTH) if prompt_seq is None else _segment_major_spec(tm, prompt_seq, SSM_WIDTH)
    return pl.pallas_call(
        functools.partial(_mix_kernel, alpha=alpha),
        grid=(t // tm,),
        in_specs=[g_spec, row(N_HEADS * V_DIM), row(D_MODEL),
                  _const_spec((SSM_WIDTH, SSM_WIDTH)), _const_spec((1, SSM_WIDTH)),
                  _const_spec((D_MODEL, D_MODEL)), _const_spec((1, D_MODEL)), _const_spec((1, D_MODEL))],
        out_specs=row(D_MODEL),
        out_shape=jax.ShapeDtypeStruct((t, D_MODEL), F32),
        compiler_params=_params(("parallel",)),
        name="mix",
    )(g, y_att, x, wts["w_glu"], wts["b_glu"], wts["w_out"], wts["ln1_g"], wts["ln1_b"])


def _ffn_kernel(x_ref, wg_ref, wu_ref, wd_ref, lg_ref, lb_ref, o_ref, xb_scr, *, alpha):
    f = pl.program_id(1)

    @pl.when(f == 0)
    def _():
        xb_scr[...] = x_ref[...].astype(BF16)
        o_ref[...] = jnp.zeros(o_ref.shape, F32)

    xb = xb_scr[...]
    gate = _dot(xb, wg_ref[...])
    up = _dot(xb, wu_ref[...])
    o_ref[...] += _dot((jax.nn.silu(gate) * up).astype(BF16), wd_ref[...])

    @pl.when(f == pl.num_programs(1) - 1)
    def _():
        o_ref[...] = _layer_norm(alpha * x_ref[...] + o_ref[...], lg_ref[...], lb_ref[...])


def _ffn(x, wts, alpha, tm, tf):
    t = x.shape[0]
    d_ff = wts["w_gate"].shape[1]
    return pl.pallas_call(
        functools.partial(_ffn_kernel, alpha=alpha),
        grid=(t // tm, d_ff // tf),
        in_specs=[pl.BlockSpec((tm, D_MODEL), lambda i, f: (i, 0)),
                  pl.BlockSpec((D_MODEL, tf), lambda i, f: (0, f)),
                  pl.BlockSpec((D_MODEL, tf), lambda i, f: (0, f)),
                  pl.BlockSpec((tf, D_MODEL), lambda i, f: (f, 0)),
                  pl.BlockSpec((1, D_MODEL), lambda i, f: (0, 0)),
                  pl.BlockSpec((1, D_MODEL), lambda i, f: (0, 0))],
        out_specs=pl.BlockSpec((tm, D_MODEL), lambda i, f: (i, 0)),
        out_shape=jax.ShapeDtypeStruct((t, D_MODEL), F32),
        scratch_shapes=[pltpu.VMEM((tm, D_MODEL), BF16)],
        compiler_params=_params(("parallel", "arbitrary")),
        name="ffn",
    )(x, wts["w_gate"], wts["w_up"], wts["w_down"], wts["ln2_g"], wts["ln2_b"])


def _swap_halves(w):
    half = w.shape[-1] // 2
    return jnp.concatenate([w[..., half:], w[..., :half]], axis=-1)


def _pad_lanes(w):
    return jnp.concatenate([w, jnp.zeros(w.shape[:-1] + (LANES - w.shape[-1],), w.dtype)], axis=-1)


def _layer_weights(w_in, g_q, w_uq, w_uk, g_kv, w_uv, w_glu, b_glu, w_out, ln1_g, ln1_b,
                   w_gate, w_up, w_down, ln2_g, ln2_b):
    n_main = SSM_WIDTH + Q_LORA + KV_LORA
    w_kr = w_in[:, n_main:]
    uq = w_uq.reshape(Q_LORA, N_HEADS, NOPE_DIM + ROPE_DIM)
    uq_rope = uq[:, :, NOPE_DIM:]
    w_qn = uq[:, :, :NOPE_DIM].reshape(Q_LORA, N_HEADS * NOPE_DIM).astype(BF16)
    w_qr = _pad_lanes(uq_rope).reshape(Q_LORA, N_HEADS * LANES).astype(BF16)
    w_qs = _pad_lanes(_swap_halves(uq_rope)).reshape(Q_LORA, N_HEADS * LANES).astype(BF16)
    return {
        "w_main": w_in[:, :n_main].astype(BF16),
        "w_kr": jnp.concatenate([_pad_lanes(w_kr), _pad_lanes(_swap_halves(w_kr))], axis=1).astype(BF16),
        "g_q": g_q.reshape(1, Q_LORA), "g_kv": g_kv.reshape(1, KV_LORA),
        "w_qn": w_qn, "w_qr": w_qr, "w_qs": w_qs,
        "w_qnT": w_qn.T, "w_qrT": w_qr.T, "w_qsT": w_qs.T,
        "w_ukT": jnp.transpose(w_uk, (1, 2, 0)).astype(BF16),
        "w_uk": jnp.transpose(w_uk, (1, 0, 2)).astype(BF16),
        "w_uv": jnp.transpose(w_uv, (1, 0, 2)).astype(BF16),
        "w_uvT": jnp.transpose(w_uv, (1, 2, 0)).astype(BF16),
        "w_glu": w_glu.astype(BF16), "b_glu": b_glu.reshape(1, SSM_WIDTH),
        "w_out": w_out.astype(BF16),
        "ln1_g": ln1_g.reshape(1, D_MODEL), "ln1_b": ln1_b.reshape(1, D_MODEL),
        "w_gate": w_gate.astype(BF16), "w_up": w_up.astype(BF16), "w_down": w_down.astype(BF16),
        "ln2_g": ln2_g.reshape(1, D_MODEL), "ln2_b": ln2_b.reshape(1, D_MODEL),
    }


def _s5_weights(a_re, a_im, log_step, b_re, b_im, c_re, c_im, d):
    rep = lambda a: jnp.repeat(a, SSM_CH, axis=0)
    bt = lambda b: jnp.transpose(b, (0, 2, 1)).reshape(SSM_GROUPS * SSM_CH, SSM_STATE)
    abr, abi, bbr, bbi = _s5prep(rep(a_re), rep(a_im), rep(log_step.reshape(SSM_GROUPS, 1)),
                                 bt(b_re), bt(b_im))
    abr = abr[::SSM_CH]
    abi = abi[::SSM_CH]
    eye = jnp.eye(SLAB_GROUPS, dtype=F32)

    def blockdiag_in(b):
        b4 = b.reshape(N_SLABS, SLAB_GROUPS, SSM_CH, SSM_STATE)
        return jnp.einsum("sghp,gk->sghkp", b4, eye).reshape(N_SLABS, LANES, SLAB_STATE)

    def blockdiag_out(c):
        c4 = c.reshape(N_SLABS, SLAB_GROUPS, SSM_CH, SSM_STATE)
        return jnp.einsum("sghp,gk->skpgh", c4, eye).reshape(N_SLABS, SLAB_STATE, LANES)

    slab_row = lambda a: a.reshape(N_SLABS, 1, SLAB_STATE)
    return {
        "wb": jnp.concatenate([blockdiag_in(bbr), blockdiag_in(bbi)], axis=2).astype(BF16),
        "wc": jnp.concatenate([blockdiag_out(c_re), -blockdiag_out(c_im)], axis=1).astype(BF16),
        "a": jnp.concatenate([slab_row(abr), slab_row(abi)], axis=2),
        "d": d.reshape(N_SLABS, 1, LANES),
    }


def _rope_tables(pos):
    half = ROPE_DIM // 2
    inv = ROPE_BASE ** (-2.0 * jnp.arange(half, dtype=F32) / ROPE_DIM)
    ang = pos.astype(F32)[:, None] * inv[None, :]
    cos = jnp.cos(ang)
    sin = jnp.sin(ang)
    return (_pad_lanes(jnp.concatenate([cos, cos], axis=1)),
            _pad_lanes(jnp.concatenate([-sin, sin], axis=1)))


def kernel(x_prompt, x_sample, cache_ckv, cache_krope, state_ssm_re, state_ssm_im, page_table, w_in, g_q, w_uq, w_uk, g_kv, w_uv, ssm_a_re, ssm_a_im, ssm_log_step, ssm_b_re, ssm_b_im, ssm_c_re, ssm_c_im, ssm_d, w_glu, b_glu, w_out, ln1_g, ln1_b, w_gate, w_up, w_down, ln2_g, ln2_b):
    depth = w_in.shape[0]
    alpha = (2 * depth) ** 0.25
    bsz, seq, _ = x_prompt.shape
    dbsz, dseq, _ = x_sample.shape
    past = page_table.shape[1] * PAGE_SIZE
    cos_p, sin_p = _rope_tables(jnp.arange(seq, dtype=jnp.int32))
    cos_p = jnp.tile(cos_p, (bsz, 1))
    sin_p = jnp.tile(sin_p, (bsz, 1))
    cos_s, sin_s = _rope_tables(past + jnp.arange(dseq, dtype=jnp.int32))
    cos_s = jnp.tile(cos_s, (dbsz, 1))
    sin_s = jnp.tile(sin_s, (dbsz, 1))
    seg_len = seq // SUBLANES

    y_p = x_prompt.reshape(bsz * seq, D_MODEL)
    y_s = x_sample.reshape(dbsz * dseq, D_MODEL)
    outs = [[] for _ in range(8)]
    for l in range(depth):
        wts = _layer_weights(w_in[l], g_q[l], w_uq[l], w_uk[l], g_kv[l], w_uv[l], w_glu[l], b_glu[l],
                             w_out[l], ln1_g[l], ln1_b[l], w_gate[l], w_up[l], w_down[l], ln2_g[l], ln2_b[l])
        s5w = _s5_weights(ssm_a_re[l], ssm_a_im[l], ssm_log_step[l], ssm_b_re[l], ssm_b_im[l],
                          ssm_c_re[l], ssm_c_im[l], ssm_d[l])

        u_sm, ckv, kr, ckvb, krb, ql_t, qr_t, v_t = _inproj(y_p, cos_p, sin_p, wts, tm=ATTN_TQ, prompt_seq=seq)
        g_sm, hl = _s5_prompt(u_sm.reshape(bsz, seq, SSM_WIDTH), s5w)
        nk = seq // ATTN_TK
        y_att = _prompt_attention(ql_t, qr_t, ckvb.reshape(bsz, nk, ATTN_TK, KV_LORA),
                                  krb.reshape(bsz, nk, ATTN_TK, LANES), v_t.reshape(bsz, nk, KV_LORA, ATTN_TK),
                                  wts["w_uvT"], bsz, seq, tq=ATTN_TQ, tk=ATTN_TK)
        x1 = _mix(g_sm.reshape(bsz, seg_len, SUBLANES * SSM_WIDTH), y_att, y_p, wts, alpha, tm=ROW_TM,
                  prompt_seq=seq)
        y_p = _ffn(x1, wts, alpha, tm=FFN_TM, tf=FFN_TF)
        h_last = hl[:, :, SUBLANES - 1, :]
        outs[0].append(ckv.reshape(bsz, seq, KV_LORA))
        outs[1].append(kr.reshape(bsz, seq, ROPE_DIM))
        outs[2].append(h_last[:, :, :SLAB_STATE].reshape(bsz, SSM_GROUPS, SSM_STATE))
        outs[3].append(h_last[:, :, SLAB_STATE:].reshape(bsz, SSM_GROUPS, SSM_STATE))

        u, ckv, kr, _, _, ql, qr = _inproj(y_s, cos_s, sin_s, wts, tm=ROW_TM)
        u_perm = u.reshape(dbsz, dseq, SSM_WIDTH).transpose(1, 0, 2).reshape(dseq * dbsz, SSM_WIDTH)
        g_perm, hl_re, hl_im = _s5_sample(
            u_perm, state_ssm_re[l].reshape(dbsz, SSM_GROUPS * SSM_STATE),
            state_ssm_im[l].reshape(dbsz, SSM_GROUPS * SSM_STATE), s5w, dseq)
        g = g_perm.reshape(dseq, dbsz, SSM_WIDTH).transpose(1, 0, 2).reshape(dbsz * dseq, SSM_WIDTH)
        per_batch = lambda q: q.reshape(N_HEADS, dbsz, dseq, q.shape[-1]).transpose(1, 0, 2, 3).reshape(
            dbsz, N_HEADS * dseq, q.shape[-1])
        y_att = _sample_attention(page_table, per_batch(ql), per_batch(qr), ckv, kr,
                                  cache_ckv[l], jnp.swapaxes(cache_krope[l], 1, 2), wts["w_uv"], dseq,
                                  chunk=SAMPLE_PAGES_PER_STEP)
        x1 = _mix(g, y_att, y_s, wts, alpha, tm=ROW_TM)
        y_s = _ffn(x1, wts, alpha, tm=FFN_TM, tf=FFN_TF)
        outs[4].append(ckv.reshape(dbsz, dseq, KV_LORA))
        outs[5].append(kr.reshape(dbsz, dseq, ROPE_DIM))
        outs[6].append(hl_re.reshape(dbsz, SSM_GROUPS, SSM_STATE))
        outs[7].append(hl_im.reshape(dbsz, SSM_GROUPS, SSM_STATE))

    return (y_p.reshape(bsz, seq, D_MODEL), y_s.reshape(dbsz, dseq, D_MODEL),
            *[jnp.stack(o) for o in outs])
```

```python
import functools
import math

import jax
import jax.numpy as jnp
from jax import lax
from jax.experimental import pallas as pl
from jax.experimental.pallas import tpu as pltpu

F32 = jnp.float32
BF16 = jnp.bfloat16

D_MODEL = 2048
SSM_WIDTH = 1024
SSM_CH = 16
SSM_GROUPS = 64
SSM_STATE = 64
N_HEADS = 8
NOPE_DIM = 128
ROPE_DIM = 64
V_DIM = 128
Q_LORA = 512
KV_LORA = 512
ROPE_BASE = 10000.0
SCALE = (NOPE_DIM + ROPE_DIM) ** -0.5
PAGE_SIZE = 128
LANES = 128
SUBLANES = 8
SLAB_GROUPS = LANES // SSM_CH
N_SLABS = SSM_GROUPS // SLAB_GROUPS
SLAB_STATE = SLAB_GROUPS * SSM_STATE
VMEM_LIMIT = 56 * 1024 * 1024
ATTN_TQ = 256
ATTN_TK = 512
ATTN_HEAD_GROUP = 2
ATTN_LOOKAHEAD = 2
SAMPLE_PAGES_PER_STEP = 16
SAMPLE_LOOKAHEAD = 3
SAMPLE_PREFETCH = 3
ROW_TM = 256
FFN_TM = 512
FFN_TF = 512


def _dot(a, b):
    return jnp.dot(a, b, preferred_element_type=F32)


def _dot_nt(a, b):
    return lax.dot_general(a, b, (((1,), (1,)), ((), ())), preferred_element_type=F32)


def _const_spec(shape):
    n = len(shape)
    return pl.BlockSpec(shape, lambda *_: (0,) * n, pipeline_mode=pl.Buffered(1))


def _params(sem):
    return pltpu.CompilerParams(dimension_semantics=sem, vmem_limit_bytes=VMEM_LIMIT)


def _rms(x, g, eps=1e-6):
    return x * lax.rsqrt(jnp.mean(x * x, axis=-1, keepdims=True) + eps) * g


def _layer_norm(x, g, b, eps=1e-5):
    mu = jnp.mean(x, axis=-1, keepdims=True)
    xc = x - mu
    var = jnp.mean(xc * xc, axis=-1, keepdims=True)
    return xc * lax.rsqrt(var + eps) * g + b


def _inproj_kernel(x_ref, wm_ref, wkr_ref, gq_ref, gkv_ref, cos_ref, sin_ref,
                   wqn_ref, wqr_ref, wqs_ref, wuk_ref,
                   u_ref, ckv_ref, kr_ref, ckvb_ref, krb_ref, ql_ref, qr_ref, *maybe_vt_ref, q_transposed):
    xb = x_ref[...].astype(BF16)
    cos = cos_ref[...]
    sin = sin_ref[...]
    proj = _dot(xb, wm_ref[...])
    u_ref[...] = proj[:, :SSM_WIDTH]
    cq = proj[:, SSM_WIDTH:SSM_WIDTH + Q_LORA]
    ckv = _rms(proj[:, SSM_WIDTH + Q_LORA:], gkv_ref[...])
    ckv_ref[...] = ckv
    ckvb_ref[...] = ckv.astype(BF16)
    kr2 = _dot(xb, wkr_ref[...])
    krope = kr2[:, :LANES] * cos + kr2[:, LANES:] * sin
    kr_ref[...] = krope[:, :ROPE_DIM]
    krb_ref[...] = krope.astype(BF16)
    cqn = _rms(cq, gq_ref[...])
    if q_transposed:
        maybe_vt_ref[0][...] = ckv.T.astype(BF16)
        cqt = cqn.T.astype(BF16)
        qn = _dot(wqn_ref[...], cqt)
        qa = _dot(wqr_ref[...], cqt)
        qs = _dot(wqs_ref[...], cqt)
        cos_t = cos.T
        sin_t = sin.T
        tm = cos.shape[0]
        for h in range(N_HEADS):
            sl = slice(h * LANES, (h + 1) * LANES)
            cols = slice(h * tm, (h + 1) * tm)
            ql_ref[:, cols] = _dot(wuk_ref[h], qn[sl].astype(BF16)).astype(BF16)
            qr_ref[:, cols] = (qa[sl] * cos_t + qs[sl] * sin_t).astype(BF16)
    else:
        cqb = cqn.astype(BF16)
        qn = _dot(cqb, wqn_ref[...])
        qa = _dot(cqb, wqr_ref[...])
        qs = _dot(cqb, wqs_ref[...])
        for h in range(N_HEADS):
            sl = slice(h * LANES, (h + 1) * LANES)
            ql_ref[h] = _dot(qn[:, sl].astype(BF16), wuk_ref[h]).astype(BF16)
            qr_ref[h] = (qa[:, sl] * cos + qs[:, sl] * sin).astype(BF16)


def _inproj(x, cos, sin, wts, tm, prompt_layout=False):
    t = x.shape[0]
    row = lambda w: pl.BlockSpec((tm, w), lambda i: (i, 0))
    extra_specs, extra_shapes = [], []
    if prompt_layout:
        assert tm == ATTN_TQ and ATTN_TK % tm == 0
        qspec = lambda w: pl.BlockSpec((None, w, N_HEADS * tm), lambda i: (i, 0, 0))
        qshape = lambda w: jax.ShapeDtypeStruct((t // tm, w, N_HEADS * tm), BF16)
        wq = [wts["w_qnT"], wts["w_qrT"], wts["w_qsT"], wts["w_uk"]]
        per_tk = ATTN_TK // tm
        extra_specs = [pl.BlockSpec((None, KV_LORA, tm), lambda i: (i // per_tk, 0, i % per_tk))]
        extra_shapes = [jax.ShapeDtypeStruct((t // ATTN_TK, KV_LORA, ATTN_TK), BF16)]
    else:
        qspec = lambda w: pl.BlockSpec((N_HEADS, tm, w), lambda i: (0, i, 0))
        qshape = lambda w: jax.ShapeDtypeStruct((N_HEADS, t, w), BF16)
        wq = [wts["w_qn"], wts["w_qr"], wts["w_qs"], wts["w_ukT"]]
    return pl.pallas_call(
        functools.partial(_inproj_kernel, q_transposed=prompt_layout),
        grid=(t // tm,),
        in_specs=[row(D_MODEL),
                  _const_spec((D_MODEL, 2048)), _const_spec((D_MODEL, 2 * LANES)),
                  _const_spec((1, Q_LORA)), _const_spec((1, KV_LORA)),
                  row(LANES), row(LANES)] + [_const_spec(w.shape) for w in wq],
        out_specs=[row(SSM_WIDTH), row(KV_LORA), row(ROPE_DIM), row(KV_LORA), row(LANES),
                   qspec(KV_LORA), qspec(LANES)] + extra_specs,
        out_shape=[jax.ShapeDtypeStruct((t, SSM_WIDTH), F32),
                   jax.ShapeDtypeStruct((t, KV_LORA), F32),
                   jax.ShapeDtypeStruct((t, ROPE_DIM), F32),
                   jax.ShapeDtypeStruct((t, KV_LORA), BF16),
                   jax.ShapeDtypeStruct((t, LANES), BF16),
                   qshape(KV_LORA), qshape(LANES)] + extra_shapes,
        compiler_params=_params(("parallel",)),
        name="inproj",
    )(x, wts["w_main"], wts["w_kr"], wts["g_q"], wts["g_kv"], cos, sin, *wq)


def _s5prep_kernel(are_ref, aim_ref, ls_ref, bre_ref, bim_ref,
                   abr_ref, abi_ref, bbr_ref, bbi_ref):
    lr = are_ref[...]
    li = aim_ref[...]
    delta = jnp.exp(ls_ref[...])
    mag = jnp.exp(lr * delta)
    ar = mag * jnp.cos(li * delta)
    ai = mag * jnp.sin(li * delta)
    abr_ref[...] = ar
    abi_ref[...] = ai
    den = lr * lr + li * li
    fr = ((ar - 1.0) * lr + ai * li) / den
    fi = (ai * lr - (ar - 1.0) * li) / den
    br = bre_ref[...]
    bi = bim_ref[...]
    bbr_ref[...] = fr * br - fi * bi
    bbi_ref[...] = fr * bi + fi * br


def _s5prep(a_re, a_im, log_step, bt_re, bt_im):
    shp = jax.ShapeDtypeStruct((SSM_GROUPS * SSM_CH, SSM_STATE), F32)
    return pl.pallas_call(_s5prep_kernel, out_shape=[shp] * 4, name="s5prep")(
        a_re, a_im, log_step, bt_re, bt_im)


def _cmul_add(ar, ai, hr, hi, br, bi):
    return ar * hr - ai * hi + br, ar * hi + ai * hr + bi


def _s5p_kernel(u_ref, wb_ref, wc_ref, a_ref, d_ref, g_ref, hl_ref, bu_scr, e_scr, *, seg_len, chunk):
    n_rows = SUBLANES * seg_len
    half = SLAB_STATE
    wb = wb_ref[...]
    for c in range(n_rows // chunk):
        rows = pl.ds(c * chunk, chunk)
        bu_scr[rows, :] = _dot(u_ref[rows, :].astype(BF16), wb)
    ar = jnp.broadcast_to(a_ref[:, :half], (SUBLANES, half))
    ai = jnp.broadcast_to(a_ref[:, half:], (SUBLANES, half))

    def step(t, carry, store):
        hr, hi = carry
        rows = pl.ds(pl.multiple_of(t * SUBLANES, SUBLANES), SUBLANES)
        hr, hi = _cmul_add(ar, ai, hr, hi, bu_scr[rows, :half], bu_scr[rows, half:])
        if store:
            bu_scr[rows, :half] = hr
            bu_scr[rows, half:] = hi
        return hr, hi

    zero = jnp.zeros((SUBLANES, half), F32)
    er, ei = lax.fori_loop(0, seg_len, functools.partial(step, store=False), (zero, zero), unroll=8)
    pr, pi = ar, ai
    for _ in range(int(math.log2(seg_len))):
        pr, pi = pr * pr - pi * pi, 2.0 * pr * pi
    e_scr[0:SUBLANES, :half] = er
    e_scr[0:SUBLANES, half:] = ei
    cr = jnp.zeros((1, half), F32)
    ci = jnp.zeros((1, half), F32)
    e_scr[SUBLANES:SUBLANES + 1, :half] = cr
    e_scr[SUBLANES:SUBLANES + 1, half:] = ci
    for s in range(SUBLANES - 1):
        cr, ci = _cmul_add(pr[0:1], pi[0:1], cr, ci, e_scr[s:s + 1, :half], e_scr[s:s + 1, half:])
        e_scr[SUBLANES + s + 1:SUBLANES + s + 2, :half] = cr
        e_scr[SUBLANES + s + 1:SUBLANES + s + 2, half:] = ci
    init = (e_scr[SUBLANES:2 * SUBLANES, :half], e_scr[SUBLANES:2 * SUBLANES, half:])
    hr, hi = lax.fori_loop(0, seg_len, functools.partial(step, store=True), init, unroll=8)
    hl_ref[:, :half] = hr
    hl_ref[:, half:] = hi
    wc = wc_ref[...]
    d = d_ref[...]
    for c in range(n_rows // chunk):
        rows = pl.ds(c * chunk, chunk)
        y = _dot(bu_scr[rows, :].astype(BF16), wc) + d * u_ref[rows, :]
        g_ref[rows, :] = jax.nn.gelu(y)


def _s5_prompt(u_perm, s5w):
    bsz, n_rows, _ = u_perm.shape
    seg_len = n_rows // SUBLANES
    kern = functools.partial(_s5p_kernel, seg_len=seg_len, chunk=512)
    slab = lambda shape: pl.BlockSpec((None,) + shape, lambda b, s: (s, 0, 0))
    return pl.pallas_call(
        kern,
        grid=(bsz, N_SLABS),
        in_specs=[pl.BlockSpec((None, n_rows, LANES), lambda b, s: (b, 0, s)),
                  slab((LANES, 2 * SLAB_STATE)), slab((2 * SLAB_STATE, LANES)),
                  slab((1, 2 * SLAB_STATE)), slab((1, LANES))],
        out_specs=[pl.BlockSpec((None, n_rows, LANES), lambda b, s: (b, 0, s)),
                   pl.BlockSpec((None, None, SUBLANES, 2 * SLAB_STATE), lambda b, s: (b, s, 0, 0))],
        out_shape=[jax.ShapeDtypeStruct((bsz, n_rows, SSM_WIDTH), F32),
                   jax.ShapeDtypeStruct((bsz, N_SLABS, SUBLANES, 2 * SLAB_STATE), F32)],
        scratch_shapes=[pltpu.VMEM((n_rows, 2 * SLAB_STATE), F32),
                        pltpu.VMEM((2 * SUBLANES, 2 * SLAB_STATE), F32)],
        compiler_params=_params(("parallel", "parallel")),
        name="s5_prompt",
    )(u_perm, s5w["wb"], s5w["wc"], s5w["a"], s5w["d"])


def _s5s_kernel(u_ref, wb_ref, wc_ref, a_ref, d_ref, h0r_ref, h0i_ref,
                g_ref, hlr_ref, hli_ref, h_scr, *, n_steps, bsz):
    half = SLAB_STATE
    u = u_ref[...]
    h_scr[...] = _dot(u.astype(BF16), wb_ref[...])
    ar = a_ref[:, :half]
    ai = a_ref[:, half:]
    hr = h0r_ref[...]
    hi = h0i_ref[...]
    for t in range(n_steps):
        rows = pl.ds(t * bsz, bsz)
        hr, hi = _cmul_add(ar, ai, hr, hi, h_scr[rows, :half], h_scr[rows, half:])
        h_scr[rows, :half] = hr
        h_scr[rows, half:] = hi
    hlr_ref[...] = hr
    hli_ref[...] = hi
    y = _dot(h_scr[...].astype(BF16), wc_ref[...]) + d_ref[...] * u
    g_ref[...] = jax.nn.gelu(y)


def _s5_sample(u_perm, h0_re, h0_im, s5w, n_steps):
    n_rows = u_perm.shape[0]
    bsz = n_rows // n_steps
    kern = functools.partial(_s5s_kernel, n_steps=n_steps, bsz=bsz)
    slab = lambda shape: pl.BlockSpec((None,) + shape, lambda s: (s, 0, 0))
    col = lambda r, w: pl.BlockSpec((r, w), lambda s: (0, s))
    return pl.pallas_call(
        kern,
        grid=(N_SLABS,),
        in_specs=[col(n_rows, LANES),
                  slab((LANES, 2 * SLAB_STATE)), slab((2 * SLAB_STATE, LANES)),
                  slab((1, 2 * SLAB_STATE)), slab((1, LANES)),
                  col(bsz, SLAB_STATE), col(bsz, SLAB_STATE)],
        out_specs=[col(n_rows, LANES), col(bsz, SLAB_STATE), col(bsz, SLAB_STATE)],
        out_shape=[jax.ShapeDtypeStruct((n_rows, SSM_WIDTH), F32),
                   jax.ShapeDtypeStruct((bsz, SSM_GROUPS * SSM_STATE), F32),
                   jax.ShapeDtypeStruct((bsz, SSM_GROUPS * SSM_STATE), F32)],
        scratch_shapes=[pltpu.VMEM((n_rows, 2 * SLAB_STATE), F32)],
        compiler_params=_params(("parallel",)),
        name="s5_sample",
    )(u_perm, s5w["wb"], s5w["wc"], s5w["a"], s5w["d"], h0_re, h0_im)


def _pattn_kernel(ql_ref, qr_ref, kc_ref, kr_ref, vt_ref, wuv_ref, o_ref, m_scr, l_scr, acc_scr, *, tq, tk):
    i = pl.program_id(1)
    gw = ATTN_HEAD_GROUP * tq
    n_groups = N_HEADS // ATTN_HEAD_GROUP
    m_scr[...] = jnp.full(m_scr.shape, -jnp.inf, F32)
    l_scr[...] = jnp.zeros(l_scr.shape, F32)
    acc_scr[...] = jnp.zeros(acc_scr.shape, F32)

    def block(j, lo=0, n=tk, diagonal=False):
        kc = kc_ref[j, lo:lo + n, :]
        kr = kr_ref[j, lo:lo + n, :]
        vt = vt_ref[j, :, lo:lo + n]
        if diagonal:
            keep = (lax.broadcasted_iota(jnp.int32, (n, gw), 0)
                    <= (lax.broadcasted_iota(jnp.int32, (n, gw), 1) & (tq - 1)))

        def scores(g):
            cols = slice(g * gw, (g + 1) * gw)
            s = (_dot(kc, ql_ref[:, cols]) + _dot(kr, qr_ref[:, cols])) * SCALE
            return jnp.where(keep, s, -jnp.inf) if diagonal else s

        pending = [scores(g) for g in range(ATTN_LOOKAHEAD)]
        for g in range(n_groups):
            s = pending.pop(0)
            if g + ATTN_LOOKAHEAD < n_groups:
                pending.append(scores(g + ATTN_LOOKAHEAD))
            cols = slice(g * gw, (g + 1) * gw)
            m_prev = m_scr[:, cols]
            m_new = jnp.maximum(m_prev, jnp.max(s, axis=0, keepdims=True))
            corr = jnp.exp(m_prev - m_new)
            p = jnp.exp(s - m_new)
            l_scr[:, cols] = l_scr[:, cols] * corr + jnp.sum(p, axis=0, keepdims=True)
            acc_scr[:, cols] = acc_scr[:, cols] * corr + _dot(vt, p.astype(BF16))
            m_scr[:, cols] = m_new

    def body(j, carry):
        block(j)
        return carry

    j_last = (i * tq) // tk
    lax.fori_loop(0, j_last, body, 0)

    @pl.when(i % 2 == 0)
    def _():
        block(j_last, 0, tq, diagonal=True)

    @pl.when(i % 2 == 1)
    def _():
        block(j_last, 0, tq)
        block(j_last, tq, tq, diagonal=True)

    for h in range(N_HEADS):
        cols = slice(h * tq, (h + 1) * tq)
        o_t = (acc_scr[:, cols] / l_scr[:, cols]).astype(BF16)
        y_t = _dot(wuv_ref[h], o_t)
        o_ref[:, h * V_DIM:(h + 1) * V_DIM] = y_t.T.astype(o_ref.dtype)


def _prompt_attention(ql_t, qr_t, kc, kr, v_t, wuv_t, bsz, seq, tq, tk):
    nq = seq // tq
    nk = seq // tk
    assert tk == 2 * tq and tq & (tq - 1) == 0 and N_HEADS % ATTN_HEAD_GROUP == 0
    kern = functools.partial(_pattn_kernel, tq=tq, tk=tk)
    per_batch = lambda r, c: pl.BlockSpec((None, nk, r, c), lambda b, i: (b, 0, 0, 0))
    return pl.pallas_call(
        kern,
        grid=(bsz, nq),
        in_specs=[pl.BlockSpec((None, KV_LORA, N_HEADS * tq), lambda b, i: (b * nq + i, 0, 0)),
                  pl.BlockSpec((None, LANES, N_HEADS * tq), lambda b, i: (b * nq + i, 0, 0)),
                  per_batch(tk, KV_LORA), per_batch(tk, LANES), per_batch(KV_LORA, tk),
                  _const_spec((N_HEADS, V_DIM, KV_LORA))],
        out_specs=pl.BlockSpec((tq, N_HEADS * V_DIM), lambda b, i: (b * nq + i, 0)),
        out_shape=jax.ShapeDtypeStruct((bsz * seq, N_HEADS * V_DIM), BF16),
        scratch_shapes=[pltpu.VMEM((1, N_HEADS * tq), F32), pltpu.VMEM((1, N_HEADS * tq), F32),
                        pltpu.VMEM((KV_LORA, N_HEADS * tq), F32)],
        compiler_params=_params(("parallel", "arbitrary")),
        name="prompt_attention",
    )(ql_t, qr_t, kc, kr, v_t, wuv_t)


N_CHAINS = 4


def _sattn_kernel(pt_ref, ql_ref, qr_ref, kn_ref, krn_ref, ck_hbm, krt_hbm, wuv_ref, o_ref,
                  kbuf, rbuf, sems, m_scr, l_scr, acc_scr, kc_scr, *, n_new, n_pages, chunk):
    b = pl.program_id(0)
    n_rows = N_HEADS * n_new
    n_chunks = n_pages // chunk

    def page_copies(pid, slot, k):
        rows = pl.ds(k * PAGE_SIZE, PAGE_SIZE)
        return (pltpu.make_async_copy(ck_hbm.at[pid], kbuf.at[slot, rows, :], sems.at[0, slot]),
                pltpu.make_async_copy(krt_hbm.at[pid], rbuf.at[slot, k], sems.at[1, slot]))

    def start_chunk(row, j):
        for k in range(chunk):
            for cp in page_copies(pt_ref[row, j * chunk + k], j, k):
                cp.start()

    def wait_chunk(j):
        for k in range(chunk):
            for cp in page_copies(0, j, k):
                cp.wait()

    def start_ahead(j):
        ahead = j + SAMPLE_PREFETCH
        if ahead < n_chunks:
            start_chunk(b, ahead)
        else:
            @pl.when(b + 1 < pl.num_programs(0))
            def _():
                start_chunk(b + 1, ahead - n_chunks)

    @pl.when(b == 0)
    def _():
        for j in range(SAMPLE_PREFETCH):
            start_chunk(0, j)

    ql = ql_ref[...]
    qr = qr_ref[:, :ROPE_DIM]

    kc = kn_ref[...].astype(BF16)
    s = (_dot_nt(ql, kc) + _dot_nt(qr, krn_ref[...].astype(BF16))) * SCALE
    q_tok = lax.broadcasted_iota(jnp.int32, (n_rows, n_new), 0) & (n_new - 1)
    k_tok = lax.broadcasted_iota(jnp.int32, (n_rows, n_new), 1)
    s = jnp.where(k_tok <= q_tok, s, -jnp.inf)
    m = jnp.max(s, axis=-1, keepdims=True)
    p = jnp.exp(s - m)
    m_scr[0] = m
    l_scr[0] = jnp.sum(p, axis=-1, keepdims=True)
    acc_scr[0] = _dot(p.astype(BF16), kc)
    for c in range(1, N_CHAINS):
        m_scr[c] = jnp.full((n_rows, 1), -jnp.inf, F32)
        l_scr[c] = jnp.zeros((n_rows, 1), F32)
        acc_scr[c] = jnp.zeros((n_rows, KV_LORA), F32)

    per_chain = chunk // N_CHAINS
    chain_rows = per_chain * PAGE_SIZE

    def scores(t):
        j, c = divmod(t, N_CHAINS)
        if c == 0:
            start_ahead(j)
            wait_chunk(j)
        rows = slice(c * chain_rows, (c + 1) * chain_rows)
        kc_scr[rows, :] = kbuf[j, rows, :].astype(BF16)
        kc = kc_scr[rows, :]
        kr_t = jnp.concatenate([rbuf[j, k] for k in range(c * per_chain, (c + 1) * per_chain)],
                               axis=1).astype(BF16)
        return (_dot_nt(ql, kc) + _dot(qr, kr_t)) * SCALE, kc

    n_blocks = n_chunks * N_CHAINS
    pending = [scores(t) for t in range(SAMPLE_LOOKAHEAD)]
    for t in range(n_blocks):
        s, kc = pending.pop(0)
        if t + SAMPLE_LOOKAHEAD < n_blocks:
            pending.append(scores(t + SAMPLE_LOOKAHEAD))
        c = t % N_CHAINS
        m_prev = m_scr[c]
        m_new = jnp.maximum(m_prev, jnp.max(s, axis=-1, keepdims=True))
        corr = jnp.exp(m_prev - m_new)
        p = jnp.exp(s - m_new)
        l_scr[c] = l_scr[c] * corr + jnp.sum(p, axis=-1, keepdims=True)
        acc_scr[c] = acc_scr[c] * corr + _dot(p.astype(BF16), kc)
        m_scr[c] = m_new

    m = m_scr[0]
    for c in range(1, N_CHAINS):
        m = jnp.maximum(m, m_scr[c])
    l = jnp.zeros((n_rows, 1), F32)
    acc = jnp.zeros((n_rows, KV_LORA), F32)
    for c in range(N_CHAINS):
        w = jnp.exp(m_scr[c] - m)
        l = l + l_scr[c] * w
        acc = acc + acc_scr[c] * w
    o = acc / l
    for h in range(N_HEADS):
        oh = o[h * n_new:(h + 1) * n_new].astype(BF16)
        o_ref[:, h * V_DIM:(h + 1) * V_DIM] = _dot(oh, wuv_ref[h]).astype(o_ref.dtype)


def _sample_attention(page_table, ql, qr, kc_new, kr_new, cache_ckv, cache_kr_t, wuv, n_new, chunk):
    bsz, n_pages = page_table.shape
    n_rows = N_HEADS * n_new
    n_chunks = n_pages // chunk
    assert n_pages % chunk == 0 and chunk % N_CHAINS == 0 and 0 < SAMPLE_PREFETCH < n_chunks
    kern = functools.partial(_sattn_kernel, n_new=n_new, n_pages=n_pages, chunk=chunk)
    in_specs = [pl.BlockSpec((None, n_rows, KV_LORA), lambda b, pt: (b, 0, 0)),
                pl.BlockSpec((None, n_rows, LANES), lambda b, pt: (b, 0, 0)),
                pl.BlockSpec((n_new, KV_LORA), lambda b, pt: (b, 0)),
                pl.BlockSpec((n_new, ROPE_DIM), lambda b, pt: (b, 0)),
                pl.BlockSpec(memory_space=pl.ANY), pl.BlockSpec(memory_space=pl.ANY),
                pl.BlockSpec((N_HEADS, KV_LORA, V_DIM), lambda b, pt: (0, 0, 0))]
    return pl.pallas_call(
        kern,
        grid_spec=pltpu.PrefetchScalarGridSpec(
            num_scalar_prefetch=1,
            grid=(bsz,),
            in_specs=in_specs,
            out_specs=pl.BlockSpec((n_new, N_HEADS * V_DIM), lambda b, pt: (b, 0)),
            scratch_shapes=[pltpu.VMEM((n_chunks, chunk * PAGE_SIZE, KV_LORA), F32),
                            pltpu.VMEM((n_chunks, chunk, ROPE_DIM, PAGE_SIZE), F32),
                            pltpu.SemaphoreType.DMA((2, n_chunks)),
                            pltpu.VMEM((N_CHAINS, n_rows, 1), F32), pltpu.VMEM((N_CHAINS, n_rows, 1), F32),
                            pltpu.VMEM((N_CHAINS, n_rows, KV_LORA), F32),
                            pltpu.VMEM((chunk * PAGE_SIZE, KV_LORA), BF16)]),
        out_shape=jax.ShapeDtypeStruct((bsz * n_new, N_HEADS * V_DIM), F32),
        compiler_params=_params(("arbitrary",)),
        name="sample_attention",
    )(page_table, ql, qr, kc_new, kr_new, cache_ckv, cache_kr_t, wuv)


def _mix_kernel(g_ref, ya_ref, x_ref, wglu_ref, bglu_ref, wout_ref, lg_ref, lb_ref, o_ref, *, alpha):
    g = g_ref[...]
    z = _dot(g.astype(BF16), wglu_ref[...]) + bglu_ref[...]
    y_ssm = g * jax.nn.sigmoid(z)
    mix = (_dot(y_ssm.astype(BF16), wout_ref[:SSM_WIDTH, :])
           + _dot(ya_ref[...].astype(BF16), wout_ref[SSM_WIDTH:, :]))
    o_ref[...] = _layer_norm(alpha * x_ref[...] + mix, lg_ref[...], lb_ref[...])


def _mix(g, y_att, x, wts, alpha, tm):
    t = x.shape[0]
    row = lambda w: pl.BlockSpec((tm, w), lambda i: (i, 0))
    return pl.pallas_call(
        functools.partial(_mix_kernel, alpha=alpha),
        grid=(t // tm,),
        in_specs=[row(SSM_WIDTH), row(N_HEADS * V_DIM), row(D_MODEL),
                  _const_spec((SSM_WIDTH, SSM_WIDTH)), _const_spec((1, SSM_WIDTH)),
                  _const_spec((D_MODEL, D_MODEL)), _const_spec((1, D_MODEL)), _const_spec((1, D_MODEL))],
        out_specs=row(D_MODEL),
        out_shape=jax.ShapeDtypeStruct((t, D_MODEL), F32),
        compiler_params=_params(("parallel",)),
        name="mix",
    )(g, y_att, x, wts["w_glu"], wts["b_glu"], wts["w_out"], wts["ln1_g"], wts["ln1_b"])


def _ffn_kernel(x_ref, wg_ref, wu_ref, wd_ref, lg_ref, lb_ref, o_ref, xb_scr, *, alpha):
    f = pl.program_id(1)

    @pl.when(f == 0)
    def _():
        xb_scr[...] = x_ref[...].astype(BF16)
        o_ref[...] = jnp.zeros(o_ref.shape, F32)

    xb = xb_scr[...]
    gate = _dot(xb, wg_ref[...])
    up = _dot(xb, wu_ref[...])
    o_ref[...] += _dot((jax.nn.silu(gate) * up).astype(BF16), wd_ref[...])

    @pl.when(f == pl.num_programs(1) - 1)
    def _():
        o_ref[...] = _layer_norm(alpha * x_ref[...] + o_ref[...], lg_ref[...], lb_ref[...])


def _ffn(x, wts, alpha, tm, tf):
    t = x.shape[0]
    d_ff = wts["w_gate"].shape[1]
    return pl.pallas_call(
        functools.partial(_ffn_kernel, alpha=alpha),
        grid=(t // tm, d_ff // tf),
        in_specs=[pl.BlockSpec((tm, D_MODEL), lambda i, f: (i, 0)),
                  pl.BlockSpec((D_MODEL, tf), lambda i, f: (0, f)),
                  pl.BlockSpec((D_MODEL, tf), lambda i, f: (0, f)),
                  pl.BlockSpec((tf, D_MODEL), lambda i, f: (f, 0)),
                  pl.BlockSpec((1, D_MODEL), lambda i, f: (0, 0)),
                  pl.BlockSpec((1, D_MODEL), lambda i, f: (0, 0))],
        out_specs=pl.BlockSpec((tm, D_MODEL), lambda i, f: (i, 0)),
        out_shape=jax.ShapeDtypeStruct((t, D_MODEL), F32),
        scratch_shapes=[pltpu.VMEM((tm, D_MODEL), BF16)],
        compiler_params=_params(("parallel", "arbitrary")),
        name="ffn",
    )(x, wts["w_gate"], wts["w_up"], wts["w_down"], wts["ln2_g"], wts["ln2_b"])


def _swap_halves(w):
    half = w.shape[-1] // 2
    return jnp.concatenate([w[..., half:], w[..., :half]], axis=-1)


def _pad_lanes(w):
    return jnp.concatenate([w, jnp.zeros(w.shape[:-1] + (LANES - w.shape[-1],), w.dtype)], axis=-1)


def _layer_weights(w_in, g_q, w_uq, w_uk, g_kv, w_uv, w_glu, b_glu, w_out, ln1_g, ln1_b,
                   w_gate, w_up, w_down, ln2_g, ln2_b):
    n_main = SSM_WIDTH + Q_LORA + KV_LORA
    w_kr = w_in[:, n_main:]
    uq = w_uq.reshape(Q_LORA, N_HEADS, NOPE_DIM + ROPE_DIM)
    uq_rope = uq[:, :, NOPE_DIM:]
    w_qn = uq[:, :, :NOPE_DIM].reshape(Q_LORA, N_HEADS * NOPE_DIM).astype(BF16)
    w_qr = _pad_lanes(uq_rope).reshape(Q_LORA, N_HEADS * LANES).astype(BF16)
    w_qs = _pad_lanes(_swap_halves(uq_rope)).reshape(Q_LORA, N_HEADS * LANES).astype(BF16)
    return {
        "w_main": w_in[:, :n_main].astype(BF16),
        "w_kr": jnp.concatenate([_pad_lanes(w_kr), _pad_lanes(_swap_halves(w_kr))], axis=1).astype(BF16),
        "g_q": g_q.reshape(1, Q_LORA), "g_kv": g_kv.reshape(1, KV_LORA),
        "w_qn": w_qn, "w_qr": w_qr, "w_qs": w_qs,
        "w_qnT": w_qn.T, "w_qrT": w_qr.T, "w_qsT": w_qs.T,
        "w_ukT": jnp.transpose(w_uk, (1, 2, 0)).astype(BF16),
        "w_uk": jnp.transpose(w_uk, (1, 0, 2)).astype(BF16),
        "w_uv": jnp.transpose(w_uv, (1, 0, 2)).astype(BF16),
        "w_uvT": jnp.transpose(w_uv, (1, 2, 0)).astype(BF16),
        "w_glu": w_glu.astype(BF16), "b_glu": b_glu.reshape(1, SSM_WIDTH),
        "w_out": w_out.astype(BF16),
        "ln1_g": ln1_g.reshape(1, D_MODEL), "ln1_b": ln1_b.reshape(1, D_MODEL),
        "w_gate": w_gate.astype(BF16), "w_up": w_up.astype(BF16), "w_down": w_down.astype(BF16),
        "ln2_g": ln2_g.reshape(1, D_MODEL), "ln2_b": ln2_b.reshape(1, D_MODEL),
    }


def _s5_weights(a_re, a_im, log_step, b_re, b_im, c_re, c_im, d):
    rep = lambda a: jnp.repeat(a, SSM_CH, axis=0)
    bt = lambda b: jnp.transpose(b, (0, 2, 1)).reshape(SSM_GROUPS * SSM_CH, SSM_STATE)
    abr, abi, bbr, bbi = _s5prep(rep(a_re), rep(a_im), rep(log_step.reshape(SSM_GROUPS, 1)),
                                 bt(b_re), bt(b_im))
    abr = abr[::SSM_CH]
    abi = abi[::SSM_CH]
    eye = jnp.eye(SLAB_GROUPS, dtype=F32)

    def blockdiag_in(b):
        b4 = b.reshape(N_SLABS, SLAB_GROUPS, SSM_CH, SSM_STATE)
        return jnp.einsum("sghp,gk->sghkp", b4, eye).reshape(N_SLABS, LANES, SLAB_STATE)

    def blockdiag_out(c):
        c4 = c.reshape(N_SLABS, SLAB_GROUPS, SSM_CH, SSM_STATE)
        return jnp.einsum("sghp,gk->skpgh", c4, eye).reshape(N_SLABS, SLAB_STATE, LANES)

    slab_row = lambda a: a.reshape(N_SLABS, 1, SLAB_STATE)
    return {
        "wb": jnp.concatenate([blockdiag_in(bbr), blockdiag_in(bbi)], axis=2).astype(BF16),
        "wc": jnp.concatenate([blockdiag_out(c_re), -blockdiag_out(c_im)], axis=1).astype(BF16),
        "a": jnp.concatenate([slab_row(abr), slab_row(abi)], axis=2),
        "d": d.reshape(N_SLABS, 1, LANES),
    }


def _rope_tables(pos):
    half = ROPE_DIM // 2
    inv = ROPE_BASE ** (-2.0 * jnp.arange(half, dtype=F32) / ROPE_DIM)
    ang = pos.astype(F32)[:, None] * inv[None, :]
    cos = jnp.cos(ang)
    sin = jnp.sin(ang)
    return (_pad_lanes(jnp.concatenate([cos, cos], axis=1)),
            _pad_lanes(jnp.concatenate([-sin, sin], axis=1)))


def kernel(x_prompt, x_sample, cache_ckv, cache_krope, state_ssm_re, state_ssm_im, page_table, w_in, g_q, w_uq, w_uk, g_kv, w_uv, ssm_a_re, ssm_a_im, ssm_log_step, ssm_b_re, ssm_b_im, ssm_c_re, ssm_c_im, ssm_d, w_glu, b_glu, w_out, ln1_g, ln1_b, w_gate, w_up, w_down, ln2_g, ln2_b):
    depth = w_in.shape[0]
    alpha = (2 * depth) ** 0.25
    bsz, seq, _ = x_prompt.shape
    dbsz, dseq, _ = x_sample.shape
    past = page_table.shape[1] * PAGE_SIZE
    cos_p, sin_p = _rope_tables(jnp.arange(seq, dtype=jnp.int32))
    cos_p = jnp.tile(cos_p, (bsz, 1))
    sin_p = jnp.tile(sin_p, (bsz, 1))
    cos_s, sin_s = _rope_tables(past + jnp.arange(dseq, dtype=jnp.int32))
    cos_s = jnp.tile(cos_s, (dbsz, 1))
    sin_s = jnp.tile(sin_s, (dbsz, 1))
    seg_len = seq // SUBLANES

    y_p = x_prompt.reshape(bsz * seq, D_MODEL)
    y_s = x_sample.reshape(dbsz * dseq, D_MODEL)
    outs = [[] for _ in range(8)]
    for l in range(depth):
        wts = _layer_weights(w_in[l], g_q[l], w_uq[l], w_uk[l], g_kv[l], w_uv[l], w_glu[l], b_glu[l],
                             w_out[l], ln1_g[l], ln1_b[l], w_gate[l], w_up[l], w_down[l], ln2_g[l], ln2_b[l])
        s5w = _s5_weights(ssm_a_re[l], ssm_a_im[l], ssm_log_step[l], ssm_b_re[l], ssm_b_im[l],
                          ssm_c_re[l], ssm_c_im[l], ssm_d[l])

        u, ckv, kr, ckvb, krb, ql_t, qr_t, v_t = _inproj(y_p, cos_p, sin_p, wts, tm=ATTN_TQ, prompt_layout=True)
        u_sm = u.reshape(bsz, SUBLANES, seg_len, SSM_WIDTH).transpose(0, 2, 1, 3)
        g_sm, hl = _s5_prompt(u_sm.reshape(bsz, seq, SSM_WIDTH), s5w)
        g = g_sm.reshape(bsz, seg_len, SUBLANES, SSM_WIDTH).transpose(0, 2, 1, 3)
        nk = seq // ATTN_TK
        y_att = _prompt_attention(ql_t, qr_t, ckvb.reshape(bsz, nk, ATTN_TK, KV_LORA),
                                  krb.reshape(bsz, nk, ATTN_TK, LANES), v_t.reshape(bsz, nk, KV_LORA, ATTN_TK),
                                  wts["w_uvT"], bsz, seq, tq=ATTN_TQ, tk=ATTN_TK)
        x1 = _mix(g.reshape(bsz * seq, SSM_WIDTH), y_att, y_p, wts, alpha, tm=ROW_TM)
        y_p = _ffn(x1, wts, alpha, tm=FFN_TM, tf=FFN_TF)
        h_last = hl[:, :, SUBLANES - 1, :]
        outs[0].append(ckv.reshape(bsz, seq, KV_LORA))
        outs[1].append(kr.reshape(bsz, seq, ROPE_DIM))
        outs[2].append(h_last[:, :, :SLAB_STATE].reshape(bsz, SSM_GROUPS, SSM_STATE))
        outs[3].append(h_last[:, :, SLAB_STATE:].reshape(bsz, SSM_GROUPS, SSM_STATE))

        u, ckv, kr, _, _, ql, qr = _inproj(y_s, cos_s, sin_s, wts, tm=ROW_TM)
        u_perm = u.reshape(dbsz, dseq, SSM_WIDTH).transpose(1, 0, 2).reshape(dseq * dbsz, SSM_WIDTH)
        g_perm, hl_re, hl_im = _s5_sample(
            u_perm, state_ssm_re[l].reshape(dbsz, SSM_GROUPS * SSM_STATE),
            state_ssm_im[l].reshape(dbsz, SSM_GROUPS * SSM_STATE), s5w, dseq)
        g = g_perm.reshape(dseq, dbsz, SSM_WIDTH).transpose(1, 0, 2).reshape(dbsz * dseq, SSM_WIDTH)
        per_batch = lambda q: q.reshape(N_HEADS, dbsz, dseq, q.shape[-1]).transpose(1, 0, 2, 3).reshape(
            dbsz, N_HEADS * dseq, q.shape[-1])
        y_att = _sample_attention(page_table, per_batch(ql), per_batch(qr), ckv, kr,
                                  cache_ckv[l], jnp.swapaxes(cache_krope[l], 1, 2), wts["w_uv"], dseq,
                                  chunk=SAMPLE_PAGES_PER_STEP)
        x1 = _mix(g, y_att, y_s, wts, alpha, tm=ROW_TM)
        y_s = _ffn(x1, wts, alpha, tm=FFN_TM, tf=FFN_TF)
        outs[4].append(ckv.reshape(dbsz, dseq, KV_LORA))
        outs[5].append(kr.reshape(dbsz, dseq, ROPE_DIM))
        outs[6].append(hl_re.reshape(dbsz, SSM_GROUPS, SSM_STATE))
        outs[7].append(hl_im.reshape(dbsz, SSM_GROUPS, SSM_STATE))

    return (y_p.reshape(bsz, seq, D_MODEL), y_s.reshape(dbsz, dseq, D_MODEL),
            *[jnp.stack(o) for o in outs])
```

```python
import functools
import math

import jax
import jax.numpy as jnp
from jax import lax
from jax.experimental import pallas as pl
from jax.experimental.pallas import tpu as pltpu

F32 = jnp.float32
BF16 = jnp.bfloat16

D_MODEL = 2048
SSM_WIDTH = 1024
SSM_CH = 16
SSM_GROUPS = 64
SSM_STATE = 64
N_HEADS = 8
NOPE_DIM = 128
ROPE_DIM = 64
V_DIM = 128
Q_LORA = 512
KV_LORA = 512
ROPE_BASE = 10000.0
SCALE = (NOPE_DIM + ROPE_DIM) ** -0.5
PAGE_SIZE = 128
LANES = 128
SUBLANES = 8
SLAB_GROUPS = LANES // SSM_CH
N_SLABS = SSM_GROUPS // SLAB_GROUPS
SLAB_STATE = SLAB_GROUPS * SSM_STATE
VMEM_LIMIT = 56 * 1024 * 1024
ATTN_TQ = 256
ATTN_TK = 512
ATTN_HEAD_GROUP = 2
ATTN_LOOKAHEAD = 2
SAMPLE_PAGES_PER_STEP = 16
SAMPLE_LOOKAHEAD = 3
SAMPLE_PREFETCH = 3
ROW_TM = 256
MIX_TM = 512
MIX_PARTS = 2
FFN_TM = 512
FFN_TF = 512


def _dot(a, b):
    return jnp.dot(a, b, preferred_element_type=F32)


def _dot_nt(a, b):
    return lax.dot_general(a, b, (((1,), (1,)), ((), ())), preferred_element_type=F32)


def _const_spec(shape):
    n = len(shape)
    return pl.BlockSpec(shape, lambda *_: (0,) * n, pipeline_mode=pl.Buffered(1))


def _params(sem):
    return pltpu.CompilerParams(dimension_semantics=sem, vmem_limit_bytes=VMEM_LIMIT)


def _rms(x, g, eps=1e-6):
    return x * lax.rsqrt(jnp.mean(x * x, axis=-1, keepdims=True) + eps) * g


def _layer_norm(x, g, b, eps=1e-5):
    mu = jnp.mean(x, axis=-1, keepdims=True)
    xc = x - mu
    var = jnp.mean(xc * xc, axis=-1, keepdims=True)
    return xc * lax.rsqrt(var + eps) * g + b


def _inproj_kernel(x_ref, wm_ref, wkr_ref, gq_ref, gkv_ref, cos_ref, sin_ref,
                   wqn_ref, wqr_ref, wqs_ref, wuk_ref,
                   u_ref, ckv_ref, kr_ref, ckvb_ref, krb_ref, ql_ref, qr_ref, *maybe_vt_ref, q_transposed):
    xb = x_ref[...].astype(BF16)
    cos = cos_ref[...]
    sin = sin_ref[...]
    proj = _dot(xb, wm_ref[...])
    u_ref[...] = proj[:, :SSM_WIDTH]
    cq = proj[:, SSM_WIDTH:SSM_WIDTH + Q_LORA]
    ckv = _rms(proj[:, SSM_WIDTH + Q_LORA:], gkv_ref[...])
    ckv_ref[...] = ckv
    ckvb_ref[...] = ckv.astype(BF16)
    kr2 = _dot(xb, wkr_ref[...])
    krope = kr2[:, :LANES] * cos + kr2[:, LANES:] * sin
    kr_ref[...] = krope[:, :ROPE_DIM]
    krb_ref[...] = krope.astype(BF16)
    cqn = _rms(cq, gq_ref[...])
    if q_transposed:
        maybe_vt_ref[0][...] = ckv.T.astype(BF16)
        cqt = cqn.T.astype(BF16)
        qn = _dot(wqn_ref[...], cqt)
        qa = _dot(wqr_ref[...], cqt)
        qs = _dot(wqs_ref[...], cqt)
        cos_t = cos.T
        sin_t = sin.T
        tm = cos.shape[0]
        for h in range(N_HEADS):
            sl = slice(h * LANES, (h + 1) * LANES)
            cols = slice(h * tm, (h + 1) * tm)
            ql_ref[:, cols] = _dot(wuk_ref[h], qn[sl].astype(BF16)).astype(BF16)
            qr_ref[:, cols] = (qa[sl] * cos_t + qs[sl] * sin_t).astype(BF16)
    else:
        cqb = cqn.astype(BF16)
        qn = _dot(cqb, wqn_ref[...])
        qa = _dot(cqb, wqr_ref[...])
        qs = _dot(cqb, wqs_ref[...])
        for h in range(N_HEADS):
            sl = slice(h * LANES, (h + 1) * LANES)
            ql_ref[h] = _dot(qn[:, sl].astype(BF16), wuk_ref[h]).astype(BF16)
            qr_ref[h] = (qa[:, sl] * cos + qs[:, sl] * sin).astype(BF16)


def _inproj(x, cos, sin, wts, tm, prompt_layout=False):
    t = x.shape[0]
    row = lambda w: pl.BlockSpec((tm, w), lambda i: (i, 0))
    extra_specs, extra_shapes = [], []
    if prompt_layout:
        assert tm == ATTN_TQ and ATTN_TK % tm == 0
        qspec = lambda w: pl.BlockSpec((None, w, N_HEADS * tm), lambda i: (i, 0, 0))
        qshape = lambda w: jax.ShapeDtypeStruct((t // tm, w, N_HEADS * tm), BF16)
        wq = [wts["w_qnT"], wts["w_qrT"], wts["w_qsT"], wts["w_uk"]]
        per_tk = ATTN_TK // tm
        extra_specs = [pl.BlockSpec((None, KV_LORA, tm), lambda i: (i // per_tk, 0, i % per_tk))]
        extra_shapes = [jax.ShapeDtypeStruct((t // ATTN_TK, KV_LORA, ATTN_TK), BF16)]
    else:
        qspec = lambda w: pl.BlockSpec((N_HEADS, tm, w), lambda i: (0, i, 0))
        qshape = lambda w: jax.ShapeDtypeStruct((N_HEADS, t, w), BF16)
        wq = [wts["w_qn"], wts["w_qr"], wts["w_qs"], wts["w_ukT"]]
    return pl.pallas_call(
        functools.partial(_inproj_kernel, q_transposed=prompt_layout),
        grid=(t // tm,),
        in_specs=[row(D_MODEL),
                  _const_spec((D_MODEL, 2048)), _const_spec((D_MODEL, 2 * LANES)),
                  _const_spec((1, Q_LORA)), _const_spec((1, KV_LORA)),
                  row(LANES), row(LANES)] + [_const_spec(w.shape) for w in wq],
        out_specs=[row(SSM_WIDTH), row(KV_LORA), row(ROPE_DIM), row(KV_LORA), row(LANES),
                   qspec(KV_LORA), qspec(LANES)] + extra_specs,
        out_shape=[jax.ShapeDtypeStruct((t, SSM_WIDTH), F32),
                   jax.ShapeDtypeStruct((t, KV_LORA), F32),
                   jax.ShapeDtypeStruct((t, ROPE_DIM), F32),
                   jax.ShapeDtypeStruct((t, KV_LORA), BF16),
                   jax.ShapeDtypeStruct((t, LANES), BF16),
                   qshape(KV_LORA), qshape(LANES)] + extra_shapes,
        compiler_params=_params(("parallel",)),
        name="inproj",
    )(x, wts["w_main"], wts["w_kr"], wts["g_q"], wts["g_kv"], cos, sin, *wq)


def _s5prep_kernel(are_ref, aim_ref, ls_ref, bre_ref, bim_ref,
                   abr_ref, abi_ref, bbr_ref, bbi_ref):
    lr = are_ref[...]
    li = aim_ref[...]
    delta = jnp.exp(ls_ref[...])
    mag = jnp.exp(lr * delta)
    ar = mag * jnp.cos(li * delta)
    ai = mag * jnp.sin(li * delta)
    abr_ref[...] = ar
    abi_ref[...] = ai
    den = lr * lr + li * li
    fr = ((ar - 1.0) * lr + ai * li) / den
    fi = (ai * lr - (ar - 1.0) * li) / den
    br = bre_ref[...]
    bi = bim_ref[...]
    bbr_ref[...] = fr * br - fi * bi
    bbi_ref[...] = fr * bi + fi * br


def _s5prep(a_re, a_im, log_step, bt_re, bt_im):
    shp = jax.ShapeDtypeStruct((SSM_GROUPS * SSM_CH, SSM_STATE), F32)
    return pl.pallas_call(_s5prep_kernel, out_shape=[shp] * 4, name="s5prep")(
        a_re, a_im, log_step, bt_re, bt_im)


def _cmul_add(ar, ai, hr, hi, br, bi):
    return ar * hr - ai * hi + br, ar * hi + ai * hr + bi


def _s5p_kernel(u_ref, wb_ref, wc_ref, a_ref, d_ref, g_ref, hl_ref, bu_scr, e_scr, *, seg_len, chunk):
    n_rows = SUBLANES * seg_len
    n_chunks = n_rows // chunk
    steps = chunk // SUBLANES
    half = SLAB_STATE
    wb = wb_ref[...]
    wc = wc_ref[...]
    d = d_ref[...]
    ar = jnp.broadcast_to(a_ref[:, :half], (SUBLANES, half))
    ai = jnp.broadcast_to(a_ref[:, half:], (SUBLANES, half))

    def project_in(c):
        rows = pl.ds(c * chunk, chunk)
        bu_scr[rows, :] = _dot(u_ref[rows, :].astype(BF16), wb)

    def project_out(c):
        rows = pl.ds(c * chunk, chunk)
        y = _dot(bu_scr[rows, :].astype(BF16), wc) + d * u_ref[rows, :]
        g_ref[rows, :] = jax.nn.gelu(y)

    def scan_chunk(c, carry, store):
        hr, hi = carry
        for t in range(c * steps, (c + 1) * steps):
            rows = pl.ds(t * SUBLANES, SUBLANES)
            hr, hi = _cmul_add(ar, ai, hr, hi, bu_scr[rows, :half], bu_scr[rows, half:])
            if store:
                bu_scr[rows, :half] = hr
                bu_scr[rows, half:] = hi
        return hr, hi

    zero = jnp.zeros((SUBLANES, half), F32)
    carry = (zero, zero)
    project_in(0)
    for c in range(n_chunks):
        if c + 1 < n_chunks:
            project_in(c + 1)
        carry = scan_chunk(c, carry, store=False)
    er, ei = carry
    pr, pi = ar, ai
    for _ in range(int(math.log2(seg_len))):
        pr, pi = pr * pr - pi * pi, 2.0 * pr * pi
    e_scr[0:SUBLANES, :half] = er
    e_scr[0:SUBLANES, half:] = ei
    cr = jnp.zeros((1, half), F32)
    ci = jnp.zeros((1, half), F32)
    e_scr[SUBLANES:SUBLANES + 1, :half] = cr
    e_scr[SUBLANES:SUBLANES + 1, half:] = ci
    for s in range(SUBLANES - 1):
        cr, ci = _cmul_add(pr[0:1], pi[0:1], cr, ci, e_scr[s:s + 1, :half], e_scr[s:s + 1, half:])
        e_scr[SUBLANES + s + 1:SUBLANES + s + 2, :half] = cr
        e_scr[SUBLANES + s + 1:SUBLANES + s + 2, half:] = ci
    carry = (e_scr[SUBLANES:2 * SUBLANES, :half], e_scr[SUBLANES:2 * SUBLANES, half:])
    for c in range(n_chunks):
        if c > 0:
            project_out(c - 1)
        carry = scan_chunk(c, carry, store=True)
    project_out(n_chunks - 1)
    hr, hi = carry
    hl_ref[:, :half] = hr
    hl_ref[:, half:] = hi


def _s5_prompt(u_perm, s5w):
    bsz, n_rows, _ = u_perm.shape
    seg_len = n_rows // SUBLANES
    kern = functools.partial(_s5p_kernel, seg_len=seg_len, chunk=512)
    slab = lambda shape: pl.BlockSpec((None,) + shape, lambda b, s: (s, 0, 0))
    return pl.pallas_call(
        kern,
        grid=(bsz, N_SLABS),
        in_specs=[pl.BlockSpec((None, n_rows, LANES), lambda b, s: (b, 0, s)),
                  slab((LANES, 2 * SLAB_STATE)), slab((2 * SLAB_STATE, LANES)),
                  slab((1, 2 * SLAB_STATE)), slab((1, LANES))],
        out_specs=[pl.BlockSpec((None, n_rows, LANES), lambda b, s: (b, 0, s)),
                   pl.BlockSpec((None, None, SUBLANES, 2 * SLAB_STATE), lambda b, s: (b, s, 0, 0))],
        out_shape=[jax.ShapeDtypeStruct((bsz, n_rows, SSM_WIDTH), F32),
                   jax.ShapeDtypeStruct((bsz, N_SLABS, SUBLANES, 2 * SLAB_STATE), F32)],
        scratch_shapes=[pltpu.VMEM((n_rows, 2 * SLAB_STATE), F32),
                        pltpu.VMEM((2 * SUBLANES, 2 * SLAB_STATE), F32)],
        compiler_params=_params(("parallel", "parallel")),
        name="s5_prompt",
    )(u_perm, s5w["wb"], s5w["wc"], s5w["a"], s5w["d"])


def _s5s_kernel(u_ref, wb_ref, wc_ref, a_ref, d_ref, h0r_ref, h0i_ref,
                g_ref, hlr_ref, hli_ref, h_scr, *, n_steps, bsz):
    half = SLAB_STATE
    u = u_ref[...]
    h_scr[...] = _dot(u.astype(BF16), wb_ref[...])
    ar = a_ref[:, :half]
    ai = a_ref[:, half:]
    hr = h0r_ref[...]
    hi = h0i_ref[...]
    for t in range(n_steps):
        rows = pl.ds(t * bsz, bsz)
        hr, hi = _cmul_add(ar, ai, hr, hi, h_scr[rows, :half], h_scr[rows, half:])
        h_scr[rows, :half] = hr
        h_scr[rows, half:] = hi
    hlr_ref[...] = hr
    hli_ref[...] = hi
    y = _dot(h_scr[...].astype(BF16), wc_ref[...]) + d_ref[...] * u
    g_ref[...] = jax.nn.gelu(y)


def _s5_sample(u_perm, h0_re, h0_im, s5w, n_steps):
    n_rows = u_perm.shape[0]
    bsz = n_rows // n_steps
    kern = functools.partial(_s5s_kernel, n_steps=n_steps, bsz=bsz)
    slab = lambda shape: pl.BlockSpec((None,) + shape, lambda s: (s, 0, 0))
    col = lambda r, w: pl.BlockSpec((r, w), lambda s: (0, s))
    return pl.pallas_call(
        kern,
        grid=(N_SLABS,),
        in_specs=[col(n_rows, LANES),
                  slab((LANES, 2 * SLAB_STATE)), slab((2 * SLAB_STATE, LANES)),
                  slab((1, 2 * SLAB_STATE)), slab((1, LANES)),
                  col(bsz, SLAB_STATE), col(bsz, SLAB_STATE)],
        out_specs=[col(n_rows, LANES), col(bsz, SLAB_STATE), col(bsz, SLAB_STATE)],
        out_shape=[jax.ShapeDtypeStruct((n_rows, SSM_WIDTH), F32),
                   jax.ShapeDtypeStruct((bsz, SSM_GROUPS * SSM_STATE), F32),
                   jax.ShapeDtypeStruct((bsz, SSM_GROUPS * SSM_STATE), F32)],
        scratch_shapes=[pltpu.VMEM((n_rows, 2 * SLAB_STATE), F32)],
        compiler_params=_params(("parallel",)),
        name="s5_sample",
    )(u_perm, s5w["wb"], s5w["wc"], s5w["a"], s5w["d"], h0_re, h0_im)


def _pattn_kernel(ql_ref, qr_ref, kc_ref, kr_ref, vt_ref, wuv_ref, o_ref, m_scr, l_scr, acc_scr, *, tq, tk):
    i = pl.program_id(1)
    gw = ATTN_HEAD_GROUP * tq
    n_groups = N_HEADS // ATTN_HEAD_GROUP
    m_scr[...] = jnp.full(m_scr.shape, -jnp.inf, F32)
    l_scr[...] = jnp.zeros(l_scr.shape, F32)
    acc_scr[...] = jnp.zeros(acc_scr.shape, F32)

    def block(j, lo=0, n=tk, diagonal=False):
        kc = kc_ref[j, lo:lo + n, :]
        kr = kr_ref[j, lo:lo + n, :]
        vt = vt_ref[j, :, lo:lo + n]
        if diagonal:
            keep = (lax.broadcasted_iota(jnp.int32, (n, gw), 0)
                    <= (lax.broadcasted_iota(jnp.int32, (n, gw), 1) & (tq - 1)))

        def scores(g):
            cols = slice(g * gw, (g + 1) * gw)
            s = (_dot(kc, ql_ref[:, cols]) + _dot(kr, qr_ref[:, cols])) * SCALE
            return jnp.where(keep, s, -jnp.inf) if diagonal else s

        pending = [scores(g) for g in range(ATTN_LOOKAHEAD)]
        for g in range(n_groups):
            s = pending.pop(0)
            if g + ATTN_LOOKAHEAD < n_groups:
                pending.append(scores(g + ATTN_LOOKAHEAD))
            cols = slice(g * gw, (g + 1) * gw)
            m_prev = m_scr[:, cols]
            m_new = jnp.maximum(m_prev, jnp.max(s, axis=0, keepdims=True))
            corr = jnp.exp(m_prev - m_new)
            p = jnp.exp(s - m_new)
            l_scr[:, cols] = l_scr[:, cols] * corr + jnp.sum(p, axis=0, keepdims=True)
            acc_scr[:, cols] = acc_scr[:, cols] * corr + _dot(vt, p.astype(BF16))
            m_scr[:, cols] = m_new

    def body(j, carry):
        block(j)
        return carry

    j_last = (i * tq) // tk
    lax.fori_loop(0, j_last, body, 0)

    @pl.when(i % 2 == 0)
    def _():
        block(j_last, 0, tq, diagonal=True)

    @pl.when(i % 2 == 1)
    def _():
        block(j_last, 0, tq)
        block(j_last, tq, tq, diagonal=True)

    for h in range(N_HEADS):
        cols = slice(h * tq, (h + 1) * tq)
        o_t = (acc_scr[:, cols] / l_scr[:, cols]).astype(BF16)
        y_t = _dot(wuv_ref[h], o_t)
        o_ref[:, h * V_DIM:(h + 1) * V_DIM] = y_t.T.astype(o_ref.dtype)


def _prompt_attention(ql_t, qr_t, kc, kr, v_t, wuv_t, bsz, seq, tq, tk):
    nq = seq // tq
    nk = seq // tk
    assert tk == 2 * tq and tq & (tq - 1) == 0 and N_HEADS % ATTN_HEAD_GROUP == 0
    kern = functools.partial(_pattn_kernel, tq=tq, tk=tk)
    per_batch = lambda r, c: pl.BlockSpec((None, nk, r, c), lambda b, i: (b, 0, 0, 0))
    return pl.pallas_call(
        kern,
        grid=(bsz, nq),
        in_specs=[pl.BlockSpec((None, KV_LORA, N_HEADS * tq), lambda b, i: (b * nq + i, 0, 0)),
                  pl.BlockSpec((None, LANES, N_HEADS * tq), lambda b, i: (b * nq + i, 0, 0)),
                  per_batch(tk, KV_LORA), per_batch(tk, LANES), per_batch(KV_LORA, tk),
                  _const_spec((N_HEADS, V_DIM, KV_LORA))],
        out_specs=pl.BlockSpec((tq, N_HEADS * V_DIM), lambda b, i: (b * nq + i, 0)),
        out_shape=jax.ShapeDtypeStruct((bsz * seq, N_HEADS * V_DIM), BF16),
        scratch_shapes=[pltpu.VMEM((1, N_HEADS * tq), F32), pltpu.VMEM((1, N_HEADS * tq), F32),
                        pltpu.VMEM((KV_LORA, N_HEADS * tq), F32)],
        compiler_params=_params(("parallel", "arbitrary")),
        name="prompt_attention",
    )(ql_t, qr_t, kc, kr, v_t, wuv_t)


N_CHAINS = 4


def _sattn_kernel(pt_ref, ql_ref, qr_ref, kn_ref, krn_ref, ck_hbm, krt_hbm, wuv_ref, o_ref,
                  kbuf, rbuf, sems, m_scr, l_scr, acc_scr, kc_scr, *, n_new, n_pages, chunk):
    b = pl.program_id(0)
    n_rows = N_HEADS * n_new
    n_chunks = n_pages // chunk

    def page_copies(pid, slot, k):
        rows = pl.ds(k * PAGE_SIZE, PAGE_SIZE)
        return (pltpu.make_async_copy(ck_hbm.at[pid], kbuf.at[slot, rows, :], sems.at[0, slot]),
                pltpu.make_async_copy(krt_hbm.at[pid], rbuf.at[slot, k], sems.at[1, slot]))

    def start_chunk(row, j):
        for k in range(chunk):
            for cp in page_copies(pt_ref[row, j * chunk + k], j, k):
                cp.start()

    def wait_chunk(j):
        for k in range(chunk):
            for cp in page_copies(0, j, k):
                cp.wait()

    def start_ahead(j):
        ahead = j + SAMPLE_PREFETCH
        if ahead < n_chunks:
            start_chunk(b, ahead)
        else:
            @pl.when(b + 1 < pl.num_programs(0))
            def _():
                start_chunk(b + 1, ahead - n_chunks)

    @pl.when(b == 0)
    def _():
        for j in range(SAMPLE_PREFETCH):
            start_chunk(0, j)

    ql = ql_ref[...]
    qr = qr_ref[:, :ROPE_DIM]

    kc = kn_ref[...].astype(BF16)
    s = (_dot_nt(ql, kc) + _dot_nt(qr, krn_ref[...].astype(BF16))) * SCALE
    q_tok = lax.broadcasted_iota(jnp.int32, (n_rows, n_new), 0) & (n_new - 1)
    k_tok = lax.broadcasted_iota(jnp.int32, (n_rows, n_new), 1)
    s = jnp.where(k_tok <= q_tok, s, -jnp.inf)
    m = jnp.max(s, axis=-1, keepdims=True)
    p = jnp.exp(s - m)
    m_scr[0] = m
    l_scr[0] = jnp.sum(p, axis=-1, keepdims=True)
    acc_scr[0] = _dot(p.astype(BF16), kc)
    for c in range(1, N_CHAINS):
        m_scr[c] = jnp.full((n_rows, 1), -jnp.inf, F32)
        l_scr[c] = jnp.zeros((n_rows, 1), F32)
        acc_scr[c] = jnp.zeros((n_rows, KV_LORA), F32)

    per_chain = chunk // N_CHAINS
    chain_rows = per_chain * PAGE_SIZE

    def scores(t):
        j, c = divmod(t, N_CHAINS)
        if c == 0:
            start_ahead(j)
            wait_chunk(j)
        rows = slice(c * chain_rows, (c + 1) * chain_rows)
        kc_scr[rows, :] = kbuf[j, rows, :].astype(BF16)
        kc = kc_scr[rows, :]
        kr_t = jnp.concatenate([rbuf[j, k] for k in range(c * per_chain, (c + 1) * per_chain)],
                               axis=1).astype(BF16)
        return (_dot_nt(ql, kc) + _dot(qr, kr_t)) * SCALE, kc

    n_blocks = n_chunks * N_CHAINS
    pending = [scores(t) for t in range(SAMPLE_LOOKAHEAD)]
    for t in range(n_blocks):
        s, kc = pending.pop(0)
        if t + SAMPLE_LOOKAHEAD < n_blocks:
            pending.append(scores(t + SAMPLE_LOOKAHEAD))
        c = t % N_CHAINS
        m_prev = m_scr[c]
        m_new = jnp.maximum(m_prev, jnp.max(s, axis=-1, keepdims=True))
        corr = jnp.exp(m_prev - m_new)
        p = jnp.exp(s - m_new)
        l_scr[c] = l_scr[c] * corr + jnp.sum(p, axis=-1, keepdims=True)
        acc_scr[c] = acc_scr[c] * corr + _dot(p.astype(BF16), kc)
        m_scr[c] = m_new

    m = m_scr[0]
    for c in range(1, N_CHAINS):
        m = jnp.maximum(m, m_scr[c])
    l = jnp.zeros((n_rows, 1), F32)
    acc = jnp.zeros((n_rows, KV_LORA), F32)
    for c in range(N_CHAINS):
        w = jnp.exp(m_scr[c] - m)
        l = l + l_scr[c] * w
        acc = acc + acc_scr[c] * w
    o = acc / l
    for h in range(N_HEADS):
        oh = o[h * n_new:(h + 1) * n_new].astype(BF16)
        o_ref[:, h * V_DIM:(h + 1) * V_DIM] = _dot(oh, wuv_ref[h]).astype(o_ref.dtype)


def _sample_attention(page_table, ql, qr, kc_new, kr_new, cache_ckv, cache_kr_t, wuv, n_new, chunk):
    bsz, n_pages = page_table.shape
    n_rows = N_HEADS * n_new
    n_chunks = n_pages // chunk
    assert n_pages % chunk == 0 and chunk % N_CHAINS == 0 and 0 < SAMPLE_PREFETCH < n_chunks
    kern = functools.partial(_sattn_kernel, n_new=n_new, n_pages=n_pages, chunk=chunk)
    in_specs = [pl.BlockSpec((None, n_rows, KV_LORA), lambda b, pt: (b, 0, 0)),
                pl.BlockSpec((None, n_rows, LANES), lambda b, pt: (b, 0, 0)),
                pl.BlockSpec((n_new, KV_LORA), lambda b, pt: (b, 0)),
                pl.BlockSpec((n_new, ROPE_DIM), lambda b, pt: (b, 0)),
                pl.BlockSpec(memory_space=pl.ANY), pl.BlockSpec(memory_space=pl.ANY),
                pl.BlockSpec((N_HEADS, KV_LORA, V_DIM), lambda b, pt: (0, 0, 0))]
    return pl.pallas_call(
        kern,
        grid_spec=pltpu.PrefetchScalarGridSpec(
            num_scalar_prefetch=1,
            grid=(bsz,),
            in_specs=in_specs,
            out_specs=pl.BlockSpec((n_new, N_HEADS * V_DIM), lambda b, pt: (b, 0)),
            scratch_shapes=[pltpu.VMEM((n_chunks, chunk * PAGE_SIZE, KV_LORA), F32),
                            pltpu.VMEM((n_chunks, chunk, ROPE_DIM, PAGE_SIZE), F32),
                            pltpu.SemaphoreType.DMA((2, n_chunks)),
                            pltpu.VMEM((N_CHAINS, n_rows, 1), F32), pltpu.VMEM((N_CHAINS, n_rows, 1), F32),
                            pltpu.VMEM((N_CHAINS, n_rows, KV_LORA), F32),
                            pltpu.VMEM((chunk * PAGE_SIZE, KV_LORA), BF16)]),
        out_shape=jax.ShapeDtypeStruct((bsz * n_new, N_HEADS * V_DIM), F32),
        compiler_params=_params(("arbitrary",)),
        name="sample_attention",
    )(page_table, ql, qr, kc_new, kr_new, cache_ckv, cache_kr_t, wuv)


def _mix_kernel(g_ref, ya_ref, x_ref, wglu_ref, bglu_ref, wout_ref, lg_ref, lb_ref, o_ref, ob_ref, *, alpha):
    part = g_ref.shape[0] // MIX_PARTS
    rows = [pl.ds(r * part, part) for r in range(MIX_PARTS)]

    def gate(r):
        g = g_ref[rows[r], :]
        return g, _dot(g.astype(BF16), wglu_ref[...])

    def glu(g, z):
        return (g * jax.nn.sigmoid(z + bglu_ref[...])).astype(BF16)

    def project(r, y):
        return (_dot(y, wout_ref[:SSM_WIDTH, :])
                + _dot(ya_ref[rows[r], :].astype(BF16), wout_ref[SSM_WIDTH:, :]))

    def finish(r, mix):
        x1 = _layer_norm(alpha * x_ref[rows[r], :] + mix, lg_ref[...], lb_ref[...])
        o_ref[rows[r], :] = x1
        ob_ref[rows[r], :] = x1.astype(BF16)

    gated = [gate(r) for r in range(MIX_PARTS)]
    mixes = []
    for r in range(MIX_PARTS):
        mixes.append(project(r, glu(*gated[r])))
        if r > 0:
            finish(r - 1, mixes[r - 1])
    finish(MIX_PARTS - 1, mixes[-1])


def _mix(g, y_att, x, wts, alpha, tm):
    t = x.shape[0]
    row = lambda w: pl.BlockSpec((tm, w), lambda i: (i, 0))
    return pl.pallas_call(
        functools.partial(_mix_kernel, alpha=alpha),
        grid=(t // tm,),
        in_specs=[row(SSM_WIDTH), row(N_HEADS * V_DIM), row(D_MODEL),
                  _const_spec((SSM_WIDTH, SSM_WIDTH)), _const_spec((1, SSM_WIDTH)),
                  _const_spec((D_MODEL, D_MODEL)), _const_spec((1, D_MODEL)), _const_spec((1, D_MODEL))],
        out_specs=[row(D_MODEL), row(D_MODEL)],
        out_shape=[jax.ShapeDtypeStruct((t, D_MODEL), F32), jax.ShapeDtypeStruct((t, D_MODEL), BF16)],
        compiler_params=_params(("parallel",)),
        name="mix",
    )(g, y_att, x, wts["w_glu"], wts["b_glu"], wts["w_out"], wts["ln1_g"], wts["ln1_b"])


def _ffn_kernel(x_ref, xb_ref, wg_ref, wu_ref, wd_ref, lg_ref, lb_ref, o_ref, *, alpha):
    f = pl.program_id(1)

    @pl.when(f == 0)
    def _():
        o_ref[...] = jnp.zeros(o_ref.shape, F32)

    xb = xb_ref[...]
    gate = _dot(xb, wg_ref[...])
    up = _dot(xb, wu_ref[...])
    o_ref[...] += _dot((jax.nn.silu(gate) * up).astype(BF16), wd_ref[...])

    @pl.when(f == pl.num_programs(1) - 1)
    def _():
        o_ref[...] = _layer_norm(alpha * x_ref[...] + o_ref[...], lg_ref[...], lb_ref[...])


def _ffn(x, xb, wts, alpha, tm, tf):
    t = x.shape[0]
    d_ff = wts["w_gate"].shape[1]
    return pl.pallas_call(
        functools.partial(_ffn_kernel, alpha=alpha),
        grid=(t // tm, d_ff // tf),
        in_specs=[pl.BlockSpec((tm, D_MODEL), lambda i, f: (i, 0)),
                  pl.BlockSpec((tm, D_MODEL), lambda i, f: (i, 0)),
                  pl.BlockSpec((D_MODEL, tf), lambda i, f: (0, f)),
                  pl.BlockSpec((D_MODEL, tf), lambda i, f: (0, f)),
                  pl.BlockSpec((tf, D_MODEL), lambda i, f: (f, 0)),
                  pl.BlockSpec((1, D_MODEL), lambda i, f: (0, 0)),
                  pl.BlockSpec((1, D_MODEL), lambda i, f: (0, 0))],
        out_specs=pl.BlockSpec((tm, D_MODEL), lambda i, f: (i, 0)),
        out_shape=jax.ShapeDtypeStruct((t, D_MODEL), F32),
        compiler_params=_params(("parallel", "arbitrary")),
        name="ffn",
    )(x, xb, wts["w_gate"], wts["w_up"], wts["w_down"], wts["ln2_g"], wts["ln2_b"])


def _swap_halves(w):
    half = w.shape[-1] // 2
    return jnp.concatenate([w[..., half:], w[..., :half]], axis=-1)


def _pad_lanes(w):
    return jnp.concatenate([w, jnp.zeros(w.shape[:-1] + (LANES - w.shape[-1],), w.dtype)], axis=-1)


def _layer_weights(w_in, g_q, w_uq, w_uk, g_kv, w_uv, w_glu, b_glu, w_out, ln1_g, ln1_b,
                   w_gate, w_up, w_down, ln2_g, ln2_b):
    n_main = SSM_WIDTH + Q_LORA + KV_LORA
    w_kr = w_in[:, n_main:]
    uq = w_uq.reshape(Q_LORA, N_HEADS, NOPE_DIM + ROPE_DIM)
    uq_rope = uq[:, :, NOPE_DIM:]
    w_qn = uq[:, :, :NOPE_DIM].reshape(Q_LORA, N_HEADS * NOPE_DIM).astype(BF16)
    w_qr = _pad_lanes(uq_rope).reshape(Q_LORA, N_HEADS * LANES).astype(BF16)
    w_qs = _pad_lanes(_swap_halves(uq_rope)).reshape(Q_LORA, N_HEADS * LANES).astype(BF16)
    return {
        "w_main": w_in[:, :n_main].astype(BF16),
        "w_kr": jnp.concatenate([_pad_lanes(w_kr), _pad_lanes(_swap_halves(w_kr))], axis=1).astype(BF16),
        "g_q": g_q.reshape(1, Q_LORA), "g_kv": g_kv.reshape(1, KV_LORA),
        "w_qn": w_qn, "w_qr": w_qr, "w_qs": w_qs,
        "w_qnT": w_qn.T, "w_qrT": w_qr.T, "w_qsT": w_qs.T,
        "w_ukT": jnp.transpose(w_uk, (1, 2, 0)).astype(BF16),
        "w_uk": jnp.transpose(w_uk, (1, 0, 2)).astype(BF16),
        "w_uv": jnp.transpose(w_uv, (1, 0, 2)).astype(BF16),
        "w_uvT": jnp.transpose(w_uv, (1, 2, 0)).astype(BF16),
        "w_glu": w_glu.astype(BF16), "b_glu": b_glu.reshape(1, SSM_WIDTH),
        "w_out": w_out.astype(BF16),
        "ln1_g": ln1_g.reshape(1, D_MODEL), "ln1_b": ln1_b.reshape(1, D_MODEL),
        "w_gate": w_gate.astype(BF16), "w_up": w_up.astype(BF16), "w_down": w_down.astype(BF16),
        "ln2_g": ln2_g.reshape(1, D_MODEL), "ln2_b": ln2_b.reshape(1, D_MODEL),
    }


def _s5_weights(a_re, a_im, log_step, b_re, b_im, c_re, c_im, d):
    rep = lambda a: jnp.repeat(a, SSM_CH, axis=0)
    bt = lambda b: jnp.transpose(b, (0, 2, 1)).reshape(SSM_GROUPS * SSM_CH, SSM_STATE)
    abr, abi, bbr, bbi = _s5prep(rep(a_re), rep(a_im), rep(log_step.reshape(SSM_GROUPS, 1)),
                                 bt(b_re), bt(b_im))
    abr = abr[::SSM_CH]
    abi = abi[::SSM_CH]
    eye = jnp.eye(SLAB_GROUPS, dtype=F32)

    def blockdiag_in(b):
        b4 = b.reshape(N_SLABS, SLAB_GROUPS, SSM_CH, SSM_STATE)
        return jnp.einsum("sghp,gk->sghkp", b4, eye).reshape(N_SLABS, LANES, SLAB_STATE)

    def blockdiag_out(c):
        c4 = c.reshape(N_SLABS, SLAB_GROUPS, SSM_CH, SSM_STATE)
        return jnp.einsum("sghp,gk->skpgh", c4, eye).reshape(N_SLABS, SLAB_STATE, LANES)

    slab_row = lambda a: a.reshape(N_SLABS, 1, SLAB_STATE)
    return {
        "wb": jnp.concatenate([blockdiag_in(bbr), blockdiag_in(bbi)], axis=2).astype(BF16),
        "wc": jnp.concatenate([blockdiag_out(c_re), -blockdiag_out(c_im)], axis=1).astype(BF16),
        "a": jnp.concatenate([slab_row(abr), slab_row(abi)], axis=2),
        "d": d.reshape(N_SLABS, 1, LANES),
    }


def _rope_tables(pos):
    half = ROPE_DIM // 2
    inv = ROPE_BASE ** (-2.0 * jnp.arange(half, dtype=F32) / ROPE_DIM)
    ang = pos.astype(F32)[:, None] * inv[None, :]
    cos = jnp.cos(ang)
    sin = jnp.sin(ang)
    return (_pad_lanes(jnp.concatenate([cos, cos], axis=1)),
            _pad_lanes(jnp.concatenate([-sin, sin], axis=1)))


def kernel(x_prompt, x_sample, cache_ckv, cache_krope, state_ssm_re, state_ssm_im, page_table, w_in, g_q, w_uq, w_uk, g_kv, w_uv, ssm_a_re, ssm_a_im, ssm_log_step, ssm_b_re, ssm_b_im, ssm_c_re, ssm_c_im, ssm_d, w_glu, b_glu, w_out, ln1_g, ln1_b, w_gate, w_up, w_down, ln2_g, ln2_b):
    depth = w_in.shape[0]
    alpha = (2 * depth) ** 0.25
    bsz, seq, _ = x_prompt.shape
    dbsz, dseq, _ = x_sample.shape
    past = page_table.shape[1] * PAGE_SIZE
    cos_p, sin_p = _rope_tables(jnp.arange(seq, dtype=jnp.int32))
    cos_p = jnp.tile(cos_p, (bsz, 1))
    sin_p = jnp.tile(sin_p, (bsz, 1))
    cos_s, sin_s = _rope_tables(past + jnp.arange(dseq, dtype=jnp.int32))
    cos_s = jnp.tile(cos_s, (dbsz, 1))
    sin_s = jnp.tile(sin_s, (dbsz, 1))
    seg_len = seq // SUBLANES

    y_p = x_prompt.reshape(bsz * seq, D_MODEL)
    y_s = x_sample.reshape(dbsz * dseq, D_MODEL)
    outs = [[] for _ in range(8)]
    for l in range(depth):
        wts = _layer_weights(w_in[l], g_q[l], w_uq[l], w_uk[l], g_kv[l], w_uv[l], w_glu[l], b_glu[l],
                             w_out[l], ln1_g[l], ln1_b[l], w_gate[l], w_up[l], w_down[l], ln2_g[l], ln2_b[l])
        s5w = _s5_weights(ssm_a_re[l], ssm_a_im[l], ssm_log_step[l], ssm_b_re[l], ssm_b_im[l],
                          ssm_c_re[l], ssm_c_im[l], ssm_d[l])

        u, ckv, kr, ckvb, krb, ql_t, qr_t, v_t = _inproj(y_p, cos_p, sin_p, wts, tm=ATTN_TQ, prompt_layout=True)
        u_sm = u.reshape(bsz, SUBLANES, seg_len, SSM_WIDTH).transpose(0, 2, 1, 3)
        g_sm, hl = _s5_prompt(u_sm.reshape(bsz, seq, SSM_WIDTH), s5w)
        g = g_sm.reshape(bsz, seg_len, SUBLANES, SSM_WIDTH).transpose(0, 2, 1, 3)
        nk = seq // ATTN_TK
        y_att = _prompt_attention(ql_t, qr_t, ckvb.reshape(bsz, nk, ATTN_TK, KV_LORA),
                                  krb.reshape(bsz, nk, ATTN_TK, LANES), v_t.reshape(bsz, nk, KV_LORA, ATTN_TK),
                                  wts["w_uvT"], bsz, seq, tq=ATTN_TQ, tk=ATTN_TK)
        x1, x1b = _mix(g.reshape(bsz * seq, SSM_WIDTH), y_att, y_p, wts, alpha, tm=MIX_TM)
        y_p = _ffn(x1, x1b, wts, alpha, tm=FFN_TM, tf=FFN_TF)
        h_last = hl[:, :, SUBLANES - 1, :]
        outs[0].append(ckv.reshape(bsz, seq, KV_LORA))
        outs[1].append(kr.reshape(bsz, seq, ROPE_DIM))
        outs[2].append(h_last[:, :, :SLAB_STATE].reshape(bsz, SSM_GROUPS, SSM_STATE))
        outs[3].append(h_last[:, :, SLAB_STATE:].reshape(bsz, SSM_GROUPS, SSM_STATE))

        u, ckv, kr, _, _, ql, qr = _inproj(y_s, cos_s, sin_s, wts, tm=ROW_TM)
        u_perm = u.reshape(dbsz, dseq, SSM_WIDTH).transpose(1, 0, 2).reshape(dseq * dbsz, SSM_WIDTH)
        g_perm, hl_re, hl_im = _s5_sample(
            u_perm, state_ssm_re[l].reshape(dbsz, SSM_GROUPS * SSM_STATE),
            state_ssm_im[l].reshape(dbsz, SSM_GROUPS * SSM_STATE), s5w, dseq)
        g = g_perm.reshape(dseq, dbsz, SSM_WIDTH).transpose(1, 0, 2).reshape(dbsz * dseq, SSM_WIDTH)
        per_batch = lambda q: q.reshape(N_HEADS, dbsz, dseq, q.shape[-1]).transpose(1, 0, 2, 3).reshape(
            dbsz, N_HEADS * dseq, q.shape[-1])
        y_att = _sample_attention(page_table, per_batch(ql), per_batch(qr), ckv, kr,
                                  cache_ckv[l], jnp.swapaxes(cache_krope[l], 1, 2), wts["w_uv"], dseq,
                                  chunk=SAMPLE_PAGES_PER_STEP)
        x1, x1b = _mix(g, y_att, y_s, wts, alpha, tm=MIX_TM)
        y_s = _ffn(x1, x1b, wts, alpha, tm=FFN_TM, tf=FFN_TF)
        outs[4].append(ckv.reshape(dbsz, dseq, KV_LORA))
        outs[5].append(kr.reshape(dbsz, dseq, ROPE_DIM))
        outs[6].append(hl_re.reshape(dbsz, SSM_GROUPS, SSM_STATE))
        outs[7].append(hl_im.reshape(dbsz, SSM_GROUPS, SSM_STATE))

    return (y_p.reshape(bsz, seq, D_MODEL), y_s.reshape(dbsz, dseq, D_MODEL),
            *[jnp.stack(o) for o in outs])
```

```python
import functools
import math

import jax
import jax.numpy as jnp
from jax import lax
from jax.experimental import pallas as pl
from jax.experimental.pallas import tpu as pltpu

F32 = jnp.float32
BF16 = jnp.bfloat16

D_MODEL = 2048
SSM_WIDTH = 1024
SSM_CH = 16
SSM_GROUPS = 64
SSM_STATE = 64
N_HEADS = 8
NOPE_DIM = 128
ROPE_DIM = 64
V_DIM = 128
Q_LORA = 512
KV_LORA = 512
ROPE_BASE = 10000.0
SCALE = (NOPE_DIM + ROPE_DIM) ** -0.5
PAGE_SIZE = 128
LANES = 128
SUBLANES = 8
SLAB_GROUPS = LANES // SSM_CH
N_SLABS = SSM_GROUPS // SLAB_GROUPS
SLAB_STATE = SLAB_GROUPS * SSM_STATE
VMEM_LIMIT = 56 * 1024 * 1024
ATTN_TQ = 256
ATTN_TK = 512
ATTN_HEAD_GROUP = 2
ATTN_LOOKAHEAD = 2
SAMPLE_PAGES_PER_STEP = 16
SAMPLE_LOOKAHEAD = 3
SAMPLE_PREFETCH = 3
ROW_TM = 256
MIX_TM = 512
MIX_PARTS = 2
FFN_TM = 512
FFN_TF = 512


def _dot(a, b):
    return jnp.dot(a, b, preferred_element_type=F32)


def _dot_nt(a, b):
    return lax.dot_general(a, b, (((1,), (1,)), ((), ())), preferred_element_type=F32)


def _const_spec(shape):
    n = len(shape)
    return pl.BlockSpec(shape, lambda *_: (0,) * n, pipeline_mode=pl.Buffered(1))


def _params(sem):
    return pltpu.CompilerParams(dimension_semantics=sem, vmem_limit_bytes=VMEM_LIMIT)


def _rms(x, g, eps=1e-6):
    return x * lax.rsqrt(jnp.mean(x * x, axis=-1, keepdims=True) + eps) * g


def _layer_norm(x, g, b, eps=1e-5):
    mu = jnp.mean(x, axis=-1, keepdims=True)
    xc = x - mu
    var = jnp.mean(xc * xc, axis=-1, keepdims=True)
    return xc * lax.rsqrt(var + eps) * g + b


def _inproj_kernel(x_ref, wm_ref, wkr_ref, gq_ref, gkv_ref, cos_ref, sin_ref,
                   wqn_ref, wqr_ref, wqs_ref, wuk_ref,
                   u_ref, ckv_ref, kr_ref, ckvb_ref, krb_ref, ql_ref, qr_ref, *maybe_vt_ref, q_transposed):
    xb = x_ref[...].astype(BF16)
    cos = cos_ref[...]
    sin = sin_ref[...]
    proj = _dot(xb, wm_ref[...])
    u_ref[...] = proj[:, :SSM_WIDTH]
    cq = proj[:, SSM_WIDTH:SSM_WIDTH + Q_LORA]
    ckv = _rms(proj[:, SSM_WIDTH + Q_LORA:], gkv_ref[...])
    ckv_ref[...] = ckv
    ckvb_ref[...] = ckv.astype(BF16)
    kr2 = _dot(xb, wkr_ref[...])
    krope = kr2[:, :LANES] * cos + kr2[:, LANES:] * sin
    kr_ref[...] = krope[:, :ROPE_DIM]
    krb_ref[...] = krope.astype(BF16)
    cqn = _rms(cq, gq_ref[...])
    if q_transposed:
        maybe_vt_ref[0][...] = ckv.T.astype(BF16)
        cqt = cqn.T.astype(BF16)
        qn = _dot(wqn_ref[...], cqt)
        qa = _dot(wqr_ref[...], cqt)
        qs = _dot(wqs_ref[...], cqt)
        cos_t = cos.T
        sin_t = sin.T
        tm = cos.shape[0]
        for h in range(N_HEADS):
            sl = slice(h * LANES, (h + 1) * LANES)
            cols = slice(h * tm, (h + 1) * tm)
            ql_ref[:, cols] = _dot(wuk_ref[h], qn[sl].astype(BF16)).astype(BF16)
            qr_ref[:, cols] = (qa[sl] * cos_t + qs[sl] * sin_t).astype(BF16)
    else:
        cqb = cqn.astype(BF16)
        qn = _dot(cqb, wqn_ref[...])
        qa = _dot(cqb, wqr_ref[...])
        qs = _dot(cqb, wqs_ref[...])
        for h in range(N_HEADS):
            sl = slice(h * LANES, (h + 1) * LANES)
            ql_ref[h] = _dot(qn[:, sl].astype(BF16), wuk_ref[h]).astype(BF16)
            qr_ref[h] = (qa[:, sl] * cos + qs[:, sl] * sin).astype(BF16)


def _inproj(x, cos, sin, wts, tm, prompt_layout=False):
    t = x.shape[0]
    row = lambda w: pl.BlockSpec((tm, w), lambda i: (i, 0))
    extra_specs, extra_shapes = [], []
    if prompt_layout:
        assert tm == ATTN_TQ and ATTN_TK % tm == 0
        qspec = lambda w: pl.BlockSpec((None, w, N_HEADS * tm), lambda i: (i, 0, 0))
        qshape = lambda w: jax.ShapeDtypeStruct((t // tm, w, N_HEADS * tm), BF16)
        wq = [wts["w_qnT"], wts["w_qrT"], wts["w_qsT"], wts["w_uk"]]
        per_tk = ATTN_TK // tm
        extra_specs = [pl.BlockSpec((None, KV_LORA, tm), lambda i: (i // per_tk, 0, i % per_tk))]
        extra_shapes = [jax.ShapeDtypeStruct((t // ATTN_TK, KV_LORA, ATTN_TK), BF16)]
    else:
        qspec = lambda w: pl.BlockSpec((N_HEADS, tm, w), lambda i: (0, i, 0))
        qshape = lambda w: jax.ShapeDtypeStruct((N_HEADS, t, w), BF16)
        wq = [wts["w_qn"], wts["w_qr"], wts["w_qs"], wts["w_ukT"]]
    return pl.pallas_call(
        functools.partial(_inproj_kernel, q_transposed=prompt_layout),
        grid=(t // tm,),
        in_specs=[row(D_MODEL),
                  _const_spec((D_MODEL, 2048)), _const_spec((D_MODEL, 2 * LANES)),
                  _const_spec((1, Q_LORA)), _const_spec((1, KV_LORA)),
                  row(LANES), row(LANES)] + [_const_spec(w.shape) for w in wq],
        out_specs=[row(SSM_WIDTH), row(KV_LORA), row(ROPE_DIM), row(KV_LORA), row(LANES),
                   qspec(KV_LORA), qspec(LANES)] + extra_specs,
        out_shape=[jax.ShapeDtypeStruct((t, SSM_WIDTH), F32),
                   jax.ShapeDtypeStruct((t, KV_LORA), F32),
                   jax.ShapeDtypeStruct((t, ROPE_DIM), F32),
                   jax.ShapeDtypeStruct((t, KV_LORA), BF16),
                   jax.ShapeDtypeStruct((t, LANES), BF16),
                   qshape(KV_LORA), qshape(LANES)] + extra_shapes,
        compiler_params=_params(("parallel",)),
        name="inproj",
    )(x, wts["w_main"], wts["w_kr"], wts["g_q"], wts["g_kv"], cos, sin, *wq)


def _s5prep_kernel(are_ref, aim_ref, ls_ref, bre_ref, bim_ref,
                   abr_ref, abi_ref, bbr_ref, bbi_ref):
    lr = are_ref[...]
    li = aim_ref[...]
    delta = jnp.exp(ls_ref[...])
    mag = jnp.exp(lr * delta)
    ar = mag * jnp.cos(li * delta)
    ai = mag * jnp.sin(li * delta)
    abr_ref[...] = ar
    abi_ref[...] = ai
    den = lr * lr + li * li
    fr = ((ar - 1.0) * lr + ai * li) / den
    fi = (ai * lr - (ar - 1.0) * li) / den
    br = bre_ref[...]
    bi = bim_ref[...]
    bbr_ref[...] = fr * br - fi * bi
    bbi_ref[...] = fr * bi + fi * br


def _s5prep(a_re, a_im, log_step, bt_re, bt_im):
    shp = jax.ShapeDtypeStruct((SSM_GROUPS * SSM_CH, SSM_STATE), F32)
    return pl.pallas_call(_s5prep_kernel, out_shape=[shp] * 4, name="s5prep")(
        a_re, a_im, log_step, bt_re, bt_im)


def _cmul_add(ar, ai, hr, hi, br, bi):
    return ar * hr - ai * hi + br, ar * hi + ai * hr + bi


def _s5p_kernel(u_ref, wb_ref, wc_ref, a_ref, d_ref, g_ref, hl_ref, bu_scr, e_scr, u_scr, g_scr,
                *, seg_len, chunk):
    n_rows = SUBLANES * seg_len
    n_chunks = n_rows // chunk
    steps = chunk // SUBLANES
    half = SLAB_STATE
    wb = wb_ref[...]
    wc = wc_ref[...]
    d = d_ref[...]
    ar = jnp.broadcast_to(a_ref[:, :half], (SUBLANES, half))
    ai = jnp.broadcast_to(a_ref[:, half:], (SUBLANES, half))

    for seg in range(SUBLANES):
        u_scr[pl.ds(seg, seg_len, stride=SUBLANES), :] = u_ref[pl.ds(seg * seg_len, seg_len), :]

    def project_in(c):
        rows = pl.ds(c * chunk, chunk)
        bu_scr[rows, :] = _dot(u_scr[rows, :].astype(BF16), wb)

    def project_out(c):
        rows = pl.ds(c * chunk, chunk)
        y = _dot(bu_scr[rows, :].astype(BF16), wc) + d * u_scr[rows, :]
        g_scr[rows, :] = jax.nn.gelu(y)
        for seg in range(SUBLANES):
            g_ref[pl.ds(seg * seg_len + c * steps, steps), :] = (
                g_scr[pl.ds(c * chunk + seg, steps, stride=SUBLANES), :])

    def scan_chunk(c, carry, store):
        hr, hi = carry
        for t in range(c * steps, (c + 1) * steps):
            rows = pl.ds(t * SUBLANES, SUBLANES)
            hr, hi = _cmul_add(ar, ai, hr, hi, bu_scr[rows, :half], bu_scr[rows, half:])
            if store:
                bu_scr[rows, :half] = hr
                bu_scr[rows, half:] = hi
        return hr, hi

    zero = jnp.zeros((SUBLANES, half), F32)
    carry = (zero, zero)
    project_in(0)
    for c in range(n_chunks):
        if c + 1 < n_chunks:
            project_in(c + 1)
        carry = scan_chunk(c, carry, store=False)
    er, ei = carry
    pr, pi = ar, ai
    for _ in range(int(math.log2(seg_len))):
        pr, pi = pr * pr - pi * pi, 2.0 * pr * pi
    e_scr[0:SUBLANES, :half] = er
    e_scr[0:SUBLANES, half:] = ei
    cr = jnp.zeros((1, half), F32)
    ci = jnp.zeros((1, half), F32)
    e_scr[SUBLANES:SUBLANES + 1, :half] = cr
    e_scr[SUBLANES:SUBLANES + 1, half:] = ci
    for s in range(SUBLANES - 1):
        cr, ci = _cmul_add(pr[0:1], pi[0:1], cr, ci, e_scr[s:s + 1, :half], e_scr[s:s + 1, half:])
        e_scr[SUBLANES + s + 1:SUBLANES + s + 2, :half] = cr
        e_scr[SUBLANES + s + 1:SUBLANES + s + 2, half:] = ci
    carry = (e_scr[SUBLANES:2 * SUBLANES, :half], e_scr[SUBLANES:2 * SUBLANES, half:])
    for c in range(n_chunks):
        if c > 0:
            project_out(c - 1)
        carry = scan_chunk(c, carry, store=True)
    project_out(n_chunks - 1)
    hr, hi = carry
    hl_ref[:, :half] = hr
    hl_ref[:, half:] = hi


def _s5_prompt(u, s5w):
    bsz, n_rows, _ = u.shape
    seg_len = n_rows // SUBLANES
    kern = functools.partial(_s5p_kernel, seg_len=seg_len, chunk=512)
    slab = lambda shape: pl.BlockSpec((None,) + shape, lambda b, s: (s, 0, 0))
    return pl.pallas_call(
        kern,
        grid=(bsz, N_SLABS),
        in_specs=[pl.BlockSpec((None, n_rows, LANES), lambda b, s: (b, 0, s)),
                  slab((LANES, 2 * SLAB_STATE)), slab((2 * SLAB_STATE, LANES)),
                  slab((1, 2 * SLAB_STATE)), slab((1, LANES))],
        out_specs=[pl.BlockSpec((None, n_rows, LANES), lambda b, s: (b, 0, s)),
                   pl.BlockSpec((None, None, SUBLANES, 2 * SLAB_STATE), lambda b, s: (b, s, 0, 0))],
        out_shape=[jax.ShapeDtypeStruct((bsz, n_rows, SSM_WIDTH), F32),
                   jax.ShapeDtypeStruct((bsz, N_SLABS, SUBLANES, 2 * SLAB_STATE), F32)],
        scratch_shapes=[pltpu.VMEM((n_rows, 2 * SLAB_STATE), F32),
                        pltpu.VMEM((2 * SUBLANES, 2 * SLAB_STATE), F32),
                        pltpu.VMEM((n_rows, LANES), F32), pltpu.VMEM((n_rows, LANES), F32)],
        compiler_params=_params(("parallel", "parallel")),
        name="s5_prompt",
    )(u, s5w["wb"], s5w["wc"], s5w["a"], s5w["d"])


def _s5s_kernel(u_ref, wb_ref, wc_ref, a_ref, d_ref, h0r_ref, h0i_ref,
                g_ref, hlr_ref, hli_ref, h_scr, *, n_steps, bsz):
    half = SLAB_STATE
    u = u_ref[...]
    h_scr[...] = _dot(u.astype(BF16), wb_ref[...])
    ar = a_ref[:, :half]
    ai = a_ref[:, half:]
    hr = h0r_ref[...]
    hi = h0i_ref[...]
    for t in range(n_steps):
        rows = pl.ds(t * bsz, bsz)
        hr, hi = _cmul_add(ar, ai, hr, hi, h_scr[rows, :half], h_scr[rows, half:])
        h_scr[rows, :half] = hr
        h_scr[rows, half:] = hi
    hlr_ref[...] = hr
    hli_ref[...] = hi
    y = _dot(h_scr[...].astype(BF16), wc_ref[...]) + d_ref[...] * u
    g_ref[...] = jax.nn.gelu(y)


def _s5_sample(u_perm, h0_re, h0_im, s5w, n_steps):
    n_rows = u_perm.shape[0]
    bsz = n_rows // n_steps
    kern = functools.partial(_s5s_kernel, n_steps=n_steps, bsz=bsz)
    slab = lambda shape: pl.BlockSpec((None,) + shape, lambda s: (s, 0, 0))
    col = lambda r, w: pl.BlockSpec((r, w), lambda s: (0, s))
    return pl.pallas_call(
        kern,
        grid=(N_SLABS,),
        in_specs=[col(n_rows, LANES),
                  slab((LANES, 2 * SLAB_STATE)), slab((2 * SLAB_STATE, LANES)),
                  slab((1, 2 * SLAB_STATE)), slab((1, LANES)),
                  col(bsz, SLAB_STATE), col(bsz, SLAB_STATE)],
        out_specs=[col(n_rows, LANES), col(bsz, SLAB_STATE), col(bsz, SLAB_STATE)],
        out_shape=[jax.ShapeDtypeStruct((n_rows, SSM_WIDTH), F32),
                   jax.ShapeDtypeStruct((bsz, SSM_GROUPS * SSM_STATE), F32),
                   jax.ShapeDtypeStruct((bsz, SSM_GROUPS * SSM_STATE), F32)],
        scratch_shapes=[pltpu.VMEM((n_rows, 2 * SLAB_STATE), F32)],
        compiler_params=_params(("parallel",)),
        name="s5_sample",
    )(u_perm, s5w["wb"], s5w["wc"], s5w["a"], s5w["d"], h0_re, h0_im)


def _pattn_kernel(ql_ref, qr_ref, kc_ref, kr_ref, vt_ref, wuv_ref, o_ref, m_scr, l_scr, acc_scr, *, tq, tk):
    i = pl.program_id(1)
    gw = ATTN_HEAD_GROUP * tq
    n_groups = N_HEADS // ATTN_HEAD_GROUP
    m_scr[...] = jnp.full(m_scr.shape, -jnp.inf, F32)
    l_scr[...] = jnp.zeros(l_scr.shape, F32)
    acc_scr[...] = jnp.zeros(acc_scr.shape, F32)

    def block(j, lo=0, n=tk, diagonal=False):
        kc = kc_ref[j, lo:lo + n, :]
        kr = kr_ref[j, lo:lo + n, :]
        vt = vt_ref[j, :, lo:lo + n]
        if diagonal:
            keep = (lax.broadcasted_iota(jnp.int32, (n, gw), 0)
                    <= (lax.broadcasted_iota(jnp.int32, (n, gw), 1) & (tq - 1)))

        def scores(g):
            cols = slice(g * gw, (g + 1) * gw)
            s = (_dot(kc, ql_ref[:, cols]) + _dot(kr, qr_ref[:, cols])) * SCALE
            return jnp.where(keep, s, -jnp.inf) if diagonal else s

        pending = [scores(g) for g in range(ATTN_LOOKAHEAD)]
        for g in range(n_groups):
            s = pending.pop(0)
            if g + ATTN_LOOKAHEAD < n_groups:
                pending.append(scores(g + ATTN_LOOKAHEAD))
            cols = slice(g * gw, (g + 1) * gw)
            m_prev = m_scr[:, cols]
            m_new = jnp.maximum(m_prev, jnp.max(s, axis=0, keepdims=True))
            corr = jnp.exp(m_prev - m_new)
            p = jnp.exp(s - m_new)
            l_scr[:, cols] = l_scr[:, cols] * corr + jnp.sum(p, axis=0, keepdims=True)
            acc_scr[:, cols] = acc_scr[:, cols] * corr + _dot(vt, p.astype(BF16))
            m_scr[:, cols] = m_new

    def body(j, carry):
        block(j)
        return carry

    j_last = (i * tq) // tk
    lax.fori_loop(0, j_last, body, 0)

    @pl.when(i % 2 == 0)
    def _():
        block(j_last, 0, tq, diagonal=True)

    @pl.when(i % 2 == 1)
    def _():
        block(j_last, 0, tq)
        block(j_last, tq, tq, diagonal=True)

    for h in range(N_HEADS):
        cols = slice(h * tq, (h + 1) * tq)
        o_t = (acc_scr[:, cols] / l_scr[:, cols]).astype(BF16)
        y_t = _dot(wuv_ref[h], o_t)
        o_ref[:, h * V_DIM:(h + 1) * V_DIM] = y_t.T.astype(o_ref.dtype)


def _prompt_attention(ql_t, qr_t, kc, kr, v_t, wuv_t, bsz, seq, tq, tk):
    nq = seq // tq
    nk = seq // tk
    assert tk == 2 * tq and tq & (tq - 1) == 0 and N_HEADS % ATTN_HEAD_GROUP == 0
    kern = functools.partial(_pattn_kernel, tq=tq, tk=tk)
    per_batch = lambda r, c: pl.BlockSpec((None, nk, r, c), lambda b, i: (b, 0, 0, 0))
    return pl.pallas_call(
        kern,
        grid=(bsz, nq),
        in_specs=[pl.BlockSpec((None, KV_LORA, N_HEADS * tq), lambda b, i: (b * nq + i, 0, 0)),
                  pl.BlockSpec((None, LANES, N_HEADS * tq), lambda b, i: (b * nq + i, 0, 0)),
                  per_batch(tk, KV_LORA), per_batch(tk, LANES), per_batch(KV_LORA, tk),
                  _const_spec((N_HEADS, V_DIM, KV_LORA))],
        out_specs=pl.BlockSpec((tq, N_HEADS * V_DIM), lambda b, i: (b * nq + i, 0)),
        out_shape=jax.ShapeDtypeStruct((bsz * seq, N_HEADS * V_DIM), BF16),
        scratch_shapes=[pltpu.VMEM((1, N_HEADS * tq), F32), pltpu.VMEM((1, N_HEADS * tq), F32),
                        pltpu.VMEM((KV_LORA, N_HEADS * tq), F32)],
        compiler_params=_params(("parallel", "arbitrary")),
        name="prompt_attention",
    )(ql_t, qr_t, kc, kr, v_t, wuv_t)


N_CHAINS = 4


def _sattn_kernel(pt_ref, ql_ref, qr_ref, kn_ref, krn_ref, ck_hbm, krt_hbm, o_ref,
                  kbuf, rbuf, sems, m_scr, l_scr, acc_scr, kc_scr, *, n_new, n_pages, chunk):
    b = pl.program_id(0)
    n_rows = N_HEADS * n_new
    n_chunks = n_pages // chunk

    def page_copies(pid, slot, k):
        rows = pl.ds(k * PAGE_SIZE, PAGE_SIZE)
        return (pltpu.make_async_copy(ck_hbm.at[pid], kbuf.at[slot, rows, :], sems.at[0, slot]),
                pltpu.make_async_copy(krt_hbm.at[pid], rbuf.at[slot, k], sems.at[1, slot]))

    def start_chunk(row, j):
        for k in range(chunk):
            for cp in page_copies(pt_ref[row, j * chunk + k], j, k):
                cp.start()

    def wait_chunk(j):
        for k in range(chunk):
            for cp in page_copies(0, j, k):
                cp.wait()

    def start_ahead(j):
        ahead = j + SAMPLE_PREFETCH
        if ahead < n_chunks:
            start_chunk(b, ahead)
        else:
            @pl.when(b + 1 < pl.num_programs(0))
            def _():
                start_chunk(b + 1, ahead - n_chunks)

    @pl.when(b == 0)
    def _():
        for j in range(SAMPLE_PREFETCH):
            start_chunk(0, j)

    ql = ql_ref[...]
    qr = qr_ref[:, :ROPE_DIM]

    kc = kn_ref[...].astype(BF16)
    s = (_dot_nt(ql, kc) + _dot_nt(qr, krn_ref[...].astype(BF16))) * SCALE
    q_tok = lax.broadcasted_iota(jnp.int32, (n_rows, n_new), 0) & (n_new - 1)
    k_tok = lax.broadcasted_iota(jnp.int32, (n_rows, n_new), 1)
    s = jnp.where(k_tok <= q_tok, s, -jnp.inf)
    m = jnp.max(s, axis=-1, keepdims=True)
    p = jnp.exp(s - m)
    m_scr[0] = m
    l_scr[0] = jnp.sum(p, axis=-1, keepdims=True)
    acc_scr[0] = _dot(p.astype(BF16), kc)
    for c in range(1, N_CHAINS):
        m_scr[c] = jnp.full((n_rows, 1), -jnp.inf, F32)
        l_scr[c] = jnp.zeros((n_rows, 1), F32)
        acc_scr[c] = jnp.zeros((n_rows, KV_LORA), F32)

    per_chain = chunk // N_CHAINS
    chain_rows = per_chain * PAGE_SIZE

    def scores(t):
        j, c = divmod(t, N_CHAINS)
        if c == 0:
            start_ahead(j)
            wait_chunk(j)
        rows = slice(c * chain_rows, (c + 1) * chain_rows)
        kc_scr[rows, :] = kbuf[j, rows, :].astype(BF16)
        kc = kc_scr[rows, :]
        kr_t = jnp.concatenate([rbuf[j, k] for k in range(c * per_chain, (c + 1) * per_chain)],
                               axis=1).astype(BF16)
        return (_dot_nt(ql, kc) + _dot(qr, kr_t)) * SCALE, kc

    n_blocks = n_chunks * N_CHAINS
    pending = [scores(t) for t in range(SAMPLE_LOOKAHEAD)]
    for t in range(n_blocks):
        s, kc = pending.pop(0)
        if t + SAMPLE_LOOKAHEAD < n_blocks:
            pending.append(scores(t + SAMPLE_LOOKAHEAD))
        c = t % N_CHAINS
        m_prev = m_scr[c]
        m_new = jnp.maximum(m_prev, jnp.max(s, axis=-1, keepdims=True))
        corr = jnp.exp(m_prev - m_new)
        p = jnp.exp(s - m_new)
        l_scr[c] = l_scr[c] * corr + jnp.sum(p, axis=-1, keepdims=True)
        acc_scr[c] = acc_scr[c] * corr + _dot(p.astype(BF16), kc)
        m_scr[c] = m_new

    m = m_scr[0]
    for c in range(1, N_CHAINS):
        m = jnp.maximum(m, m_scr[c])
    l = jnp.zeros((n_rows, 1), F32)
    acc = jnp.zeros((n_rows, KV_LORA), F32)
    for c in range(N_CHAINS):
        w = jnp.exp(m_scr[c] - m)
        l = l + l_scr[c] * w
        acc = acc + acc_scr[c] * w
    o = acc / l
    for h in range(N_HEADS):
        o_ref[h] = o[h * n_new:(h + 1) * n_new]


def _value_up_kernel(o_ref, wuv_ref, y_ref):
    y_ref[...] = _dot(o_ref[...].astype(BF16), wuv_ref[...])


def _value_up(o_lat, wuv):
    _, t, _ = o_lat.shape
    return pl.pallas_call(
        _value_up_kernel,
        grid=(N_HEADS,),
        in_specs=[pl.BlockSpec((None, t, KV_LORA), lambda h: (h, 0, 0)),
                  pl.BlockSpec((None, KV_LORA, V_DIM), lambda h: (h, 0, 0))],
        out_specs=pl.BlockSpec((t, V_DIM), lambda h: (0, h)),
        out_shape=jax.ShapeDtypeStruct((t, N_HEADS * V_DIM), F32),
        compiler_params=_params(("parallel",)),
        name="value_up",
    )(o_lat, wuv)


def _sample_attention(page_table, ql, qr, kc_new, kr_new, cache_ckv, cache_kr_t, n_new, chunk):
    bsz, n_pages = page_table.shape
    n_rows = N_HEADS * n_new
    n_chunks = n_pages // chunk
    assert n_pages % chunk == 0 and chunk % N_CHAINS == 0 and 0 < SAMPLE_PREFETCH < n_chunks
    kern = functools.partial(_sattn_kernel, n_new=n_new, n_pages=n_pages, chunk=chunk)
    in_specs = [pl.BlockSpec((None, n_rows, KV_LORA), lambda b, pt: (b, 0, 0)),
                pl.BlockSpec((None, n_rows, LANES), lambda b, pt: (b, 0, 0)),
                pl.BlockSpec((n_new, KV_LORA), lambda b, pt: (b, 0)),
                pl.BlockSpec((n_new, ROPE_DIM), lambda b, pt: (b, 0)),
                pl.BlockSpec(memory_space=pl.ANY), pl.BlockSpec(memory_space=pl.ANY)]
    return pl.pallas_call(
        kern,
        grid_spec=pltpu.PrefetchScalarGridSpec(
            num_scalar_prefetch=1,
            grid=(bsz,),
            in_specs=in_specs,
            out_specs=pl.BlockSpec((N_HEADS, n_new, KV_LORA), lambda b, pt: (0, b, 0)),
            scratch_shapes=[pltpu.VMEM((n_chunks, chunk * PAGE_SIZE, KV_LORA), F32),
                            pltpu.VMEM((n_chunks, chunk, ROPE_DIM, PAGE_SIZE), F32),
                            pltpu.SemaphoreType.DMA((2, n_chunks)),
                            pltpu.VMEM((N_CHAINS, n_rows, 1), F32), pltpu.VMEM((N_CHAINS, n_rows, 1), F32),
                            pltpu.VMEM((N_CHAINS, n_rows, KV_LORA), F32),
                            pltpu.VMEM((chunk * PAGE_SIZE, KV_LORA), BF16)]),
        out_shape=jax.ShapeDtypeStruct((N_HEADS, bsz * n_new, KV_LORA), F32),
        compiler_params=_params(("arbitrary",)),
        name="sample_attention",
    )(page_table, ql, qr, kc_new, kr_new, cache_ckv, cache_kr_t)


def _mix_kernel(g_ref, ya_ref, x_ref, wglu_ref, bglu_ref, wout_ref, lg_ref, lb_ref, o_ref, ob_ref, *, alpha):
    part = g_ref.shape[0] // MIX_PARTS
    rows = [pl.ds(r * part, part) for r in range(MIX_PARTS)]

    def gate(r):
        g = g_ref[rows[r], :]
        return g, _dot(g.astype(BF16), wglu_ref[...])

    def glu(g, z):
        return (g * jax.nn.sigmoid(z + bglu_ref[...])).astype(BF16)

    def project(r, y):
        return (_dot(y, wout_ref[:SSM_WIDTH, :])
                + _dot(ya_ref[rows[r], :].astype(BF16), wout_ref[SSM_WIDTH:, :]))

    def finish(r, mix):
        x1 = _layer_norm(alpha * x_ref[rows[r], :] + mix, lg_ref[...], lb_ref[...])
        o_ref[rows[r], :] = x1
        ob_ref[rows[r], :] = x1.astype(BF16)

    gated = [gate(r) for r in range(MIX_PARTS)]
    mixes = []
    for r in range(MIX_PARTS):
        mixes.append(project(r, glu(*gated[r])))
        if r > 0:
            finish(r - 1, mixes[r - 1])
    finish(MIX_PARTS - 1, mixes[-1])


def _mix(g, y_att, x, wts, alpha, tm):
    t = x.shape[0]
    row = lambda w: pl.BlockSpec((tm, w), lambda i: (i, 0))
    return pl.pallas_call(
        functools.partial(_mix_kernel, alpha=alpha),
        grid=(t // tm,),
        in_specs=[row(SSM_WIDTH), row(N_HEADS * V_DIM), row(D_MODEL),
                  _const_spec((SSM_WIDTH, SSM_WIDTH)), _const_spec((1, SSM_WIDTH)),
                  _const_spec((D_MODEL, D_MODEL)), _const_spec((1, D_MODEL)), _const_spec((1, D_MODEL))],
        out_specs=[row(D_MODEL), row(D_MODEL)],
        out_shape=[jax.ShapeDtypeStruct((t, D_MODEL), F32), jax.ShapeDtypeStruct((t, D_MODEL), BF16)],
        compiler_params=_params(("parallel",)),
        name="mix",
    )(g, y_att, x, wts["w_glu"], wts["b_glu"], wts["w_out"], wts["ln1_g"], wts["ln1_b"])


def _ffn_kernel(x_ref, xb_ref, wg_ref, wu_ref, wd_ref, lg_ref, lb_ref, o_ref, *, alpha):
    f = pl.program_id(1)

    @pl.when(f == 0)
    def _():
        o_ref[...] = jnp.zeros(o_ref.shape, F32)

    xb = xb_ref[...]
    gate = _dot(xb, wg_ref[...])
    up = _dot(xb, wu_ref[...])
    o_ref[...] += _dot((jax.nn.silu(gate) * up).astype(BF16), wd_ref[...])

    @pl.when(f == pl.num_programs(1) - 1)
    def _():
        o_ref[...] = _layer_norm(alpha * x_ref[...] + o_ref[...], lg_ref[...], lb_ref[...])


def _ffn(x, xb, wts, alpha, tm, tf):
    t = x.shape[0]
    d_ff = wts["w_gate"].shape[1]
    return pl.pallas_call(
        functools.partial(_ffn_kernel, alpha=alpha),
        grid=(t // tm, d_ff // tf),
        in_specs=[pl.BlockSpec((tm, D_MODEL), lambda i, f: (i, 0)),
                  pl.BlockSpec((tm, D_MODEL), lambda i, f: (i, 0)),
                  pl.BlockSpec((D_MODEL, tf), lambda i, f: (0, f)),
                  pl.BlockSpec((D_MODEL, tf), lambda i, f: (0, f)),
                  pl.BlockSpec((tf, D_MODEL), lambda i, f: (f, 0)),
                  pl.BlockSpec((1, D_MODEL), lambda i, f: (0, 0)),
                  pl.BlockSpec((1, D_MODEL), lambda i, f: (0, 0))],
        out_specs=pl.BlockSpec((tm, D_MODEL), lambda i, f: (i, 0)),
        out_shape=jax.ShapeDtypeStruct((t, D_MODEL), F32),
        compiler_params=_params(("parallel", "arbitrary")),
        name="ffn",
    )(x, xb, wts["w_gate"], wts["w_up"], wts["w_down"], wts["ln2_g"], wts["ln2_b"])


def _swap_halves(w):
    half = w.shape[-1] // 2
    return jnp.concatenate([w[..., half:], w[..., :half]], axis=-1)


def _pad_lanes(w):
    return jnp.concatenate([w, jnp.zeros(w.shape[:-1] + (LANES - w.shape[-1],), w.dtype)], axis=-1)


def _layer_weights(w_in, g_q, w_uq, w_uk, g_kv, w_uv, w_glu, b_glu, w_out, ln1_g, ln1_b,
                   w_gate, w_up, w_down, ln2_g, ln2_b):
    n_main = SSM_WIDTH + Q_LORA + KV_LORA
    w_kr = w_in[:, n_main:]
    uq = w_uq.reshape(Q_LORA, N_HEADS, NOPE_DIM + ROPE_DIM)
    uq_rope = uq[:, :, NOPE_DIM:]
    w_qn = uq[:, :, :NOPE_DIM].reshape(Q_LORA, N_HEADS * NOPE_DIM).astype(BF16)
    w_qr = _pad_lanes(uq_rope).reshape(Q_LORA, N_HEADS * LANES).astype(BF16)
    w_qs = _pad_lanes(_swap_halves(uq_rope)).reshape(Q_LORA, N_HEADS * LANES).astype(BF16)
    return {
        "w_main": w_in[:, :n_main].astype(BF16),
        "w_kr": jnp.concatenate([_pad_lanes(w_kr), _pad_lanes(_swap_halves(w_kr))], axis=1).astype(BF16),
        "g_q": g_q.reshape(1, Q_LORA), "g_kv": g_kv.reshape(1, KV_LORA),
        "w_qn": w_qn, "w_qr": w_qr, "w_qs": w_qs,
        "w_qnT": w_qn.T, "w_qrT": w_qr.T, "w_qsT": w_qs.T,
        "w_ukT": jnp.transpose(w_uk, (1, 2, 0)).astype(BF16),
        "w_uk": jnp.transpose(w_uk, (1, 0, 2)).astype(BF16),
        "w_uv": jnp.transpose(w_uv, (1, 0, 2)).astype(BF16),
        "w_uvT": jnp.transpose(w_uv, (1, 2, 0)).astype(BF16),
        "w_glu": w_glu.astype(BF16), "b_glu": b_glu.reshape(1, SSM_WIDTH),
        "w_out": w_out.astype(BF16),
        "ln1_g": ln1_g.reshape(1, D_MODEL), "ln1_b": ln1_b.reshape(1, D_MODEL),
        "w_gate": w_gate.astype(BF16), "w_up": w_up.astype(BF16), "w_down": w_down.astype(BF16),
        "ln2_g": ln2_g.reshape(1, D_MODEL), "ln2_b": ln2_b.reshape(1, D_MODEL),
    }


def _s5_weights(a_re, a_im, log_step, b_re, b_im, c_re, c_im, d):
    rep = lambda a: jnp.repeat(a, SSM_CH, axis=0)
    bt = lambda b: jnp.transpose(b, (0, 2, 1)).reshape(SSM_GROUPS * SSM_CH, SSM_STATE)
    abr, abi, bbr, bbi = _s5prep(rep(a_re), rep(a_im), rep(log_step.reshape(SSM_GROUPS, 1)),
                                 bt(b_re), bt(b_im))
    abr = abr[::SSM_CH]
    abi = abi[::SSM_CH]
    eye = jnp.eye(SLAB_GROUPS, dtype=F32)

    def blockdiag_in(b):
        b4 = b.reshape(N_SLABS, SLAB_GROUPS, SSM_CH, SSM_STATE)
        return jnp.einsum("sghp,gk->sghkp", b4, eye).reshape(N_SLABS, LANES, SLAB_STATE)

    def blockdiag_out(c):
        c4 = c.reshape(N_SLABS, SLAB_GROUPS, SSM_CH, SSM_STATE)
        return jnp.einsum("sghp,gk->skpgh", c4, eye).reshape(N_SLABS, SLAB_STATE, LANES)

    slab_row = lambda a: a.reshape(N_SLABS, 1, SLAB_STATE)
    return {
        "wb": jnp.concatenate([blockdiag_in(bbr), blockdiag_in(bbi)], axis=2).astype(BF16),
        "wc": jnp.concatenate([blockdiag_out(c_re), -blockdiag_out(c_im)], axis=1).astype(BF16),
        "a": jnp.concatenate([slab_row(abr), slab_row(abi)], axis=2),
        "d": d.reshape(N_SLABS, 1, LANES),
    }


def _rope_tables(pos):
    half = ROPE_DIM // 2
    inv = ROPE_BASE ** (-2.0 * jnp.arange(half, dtype=F32) / ROPE_DIM)
    ang = pos.astype(F32)[:, None] * inv[None, :]
    cos = jnp.cos(ang)
    sin = jnp.sin(ang)
    return (_pad_lanes(jnp.concatenate([cos, cos], axis=1)),
            _pad_lanes(jnp.concatenate([-sin, sin], axis=1)))


def kernel(x_prompt, x_sample, cache_ckv, cache_krope, state_ssm_re, state_ssm_im, page_table, w_in, g_q, w_uq, w_uk, g_kv, w_uv, ssm_a_re, ssm_a_im, ssm_log_step, ssm_b_re, ssm_b_im, ssm_c_re, ssm_c_im, ssm_d, w_glu, b_glu, w_out, ln1_g, ln1_b, w_gate, w_up, w_down, ln2_g, ln2_b):
    depth = w_in.shape[0]
    alpha = (2 * depth) ** 0.25
    bsz, seq, _ = x_prompt.shape
    dbsz, dseq, _ = x_sample.shape
    past = page_table.shape[1] * PAGE_SIZE
    cos_p, sin_p = _rope_tables(jnp.arange(seq, dtype=jnp.int32))
    cos_p = jnp.tile(cos_p, (bsz, 1))
    sin_p = jnp.tile(sin_p, (bsz, 1))
    cos_s, sin_s = _rope_tables(past + jnp.arange(dseq, dtype=jnp.int32))
    cos_s = jnp.tile(cos_s, (dbsz, 1))
    sin_s = jnp.tile(sin_s, (dbsz, 1))

    y_p = x_prompt.reshape(bsz * seq, D_MODEL)
    y_s = x_sample.reshape(dbsz * dseq, D_MODEL)
    outs = [[] for _ in range(8)]
    for l in range(depth):
        wts = _layer_weights(w_in[l], g_q[l], w_uq[l], w_uk[l], g_kv[l], w_uv[l], w_glu[l], b_glu[l],
                             w_out[l], ln1_g[l], ln1_b[l], w_gate[l], w_up[l], w_down[l], ln2_g[l], ln2_b[l])
        s5w = _s5_weights(ssm_a_re[l], ssm_a_im[l], ssm_log_step[l], ssm_b_re[l], ssm_b_im[l],
                          ssm_c_re[l], ssm_c_im[l], ssm_d[l])

        u, ckv, kr, ckvb, krb, ql_t, qr_t, v_t = _inproj(y_p, cos_p, sin_p, wts, tm=ATTN_TQ, prompt_layout=True)
        g, hl = _s5_prompt(u.reshape(bsz, seq, SSM_WIDTH), s5w)
        nk = seq // ATTN_TK
        y_att = _prompt_attention(ql_t, qr_t, ckvb.reshape(bsz, nk, ATTN_TK, KV_LORA),
                                  krb.reshape(bsz, nk, ATTN_TK, LANES), v_t.reshape(bsz, nk, KV_LORA, ATTN_TK),
                                  wts["w_uvT"], bsz, seq, tq=ATTN_TQ, tk=ATTN_TK)
        x1, x1b = _mix(g.reshape(bsz * seq, SSM_WIDTH), y_att, y_p, wts, alpha, tm=MIX_TM)
        y_p = _ffn(x1, x1b, wts, alpha, tm=FFN_TM, tf=FFN_TF)
        h_last = hl[:, :, SUBLANES - 1, :]
        outs[0].append(ckv.reshape(bsz, seq, KV_LORA))
        outs[1].append(kr.reshape(bsz, seq, ROPE_DIM))
        outs[2].append(h_last[:, :, :SLAB_STATE].reshape(bsz, SSM_GROUPS, SSM_STATE))
        outs[3].append(h_last[:, :, SLAB_STATE:].reshape(bsz, SSM_GROUPS, SSM_STATE))

        u, ckv, kr, _, _, ql, qr = _inproj(y_s, cos_s, sin_s, wts, tm=ROW_TM)
        u_perm = u.reshape(dbsz, dseq, SSM_WIDTH).transpose(1, 0, 2).reshape(dseq * dbsz, SSM_WIDTH)
        g_perm, hl_re, hl_im = _s5_sample(
            u_perm, state_ssm_re[l].reshape(dbsz, SSM_GROUPS * SSM_STATE),
            state_ssm_im[l].reshape(dbsz, SSM_GROUPS * SSM_STATE), s5w, dseq)
        g = g_perm.reshape(dseq, dbsz, SSM_WIDTH).transpose(1, 0, 2).reshape(dbsz * dseq, SSM_WIDTH)
        per_batch = lambda q: q.reshape(N_HEADS, dbsz, dseq, q.shape[-1]).transpose(1, 0, 2, 3).reshape(
            dbsz, N_HEADS * dseq, q.shape[-1])
        o_lat = _sample_attention(page_table, per_batch(ql), per_batch(qr), ckv, kr,
                                  cache_ckv[l], jnp.swapaxes(cache_krope[l], 1, 2), dseq,
                                  chunk=SAMPLE_PAGES_PER_STEP)
        y_att = _value_up(o_lat, wts["w_uv"])
        x1, x1b = _mix(g, y_att, y_s, wts, alpha, tm=MIX_TM)
        y_s = _ffn(x1, x1b, wts, alpha, tm=FFN_TM, tf=FFN_TF)
        outs[4].append(ckv.reshape(dbsz, dseq, KV_LORA))
        outs[5].append(kr.reshape(dbsz, dseq, ROPE_DIM))
        outs[6].append(hl_re.reshape(dbsz, SSM_GROUPS, SSM_STATE))
        outs[7].append(hl_im.reshape(dbsz, SSM_GROUPS, SSM_STATE))

    return (y_p.reshape(bsz, seq, D_MODEL), y_s.reshape(dbsz, dseq, D_MODEL),
            *[jnp.stack(o) for o in outs])
```

```python
import functools
import math

import jax
import jax.numpy as jnp
from jax import lax
from jax.experimental import pallas as pl
from jax.experimental.pallas import tpu as pltpu

F32 = jnp.float32
BF16 = jnp.bfloat16

D_MODEL = 2048
SSM_WIDTH = 1024
SSM_CH = 16
SSM_GROUPS = 64
SSM_STATE = 64
N_HEADS = 8
NOPE_DIM = 128
ROPE_DIM = 64
V_DIM = 128
Q_LORA = 512
KV_LORA = 512
ROPE_BASE = 10000.0
SCALE = (NOPE_DIM + ROPE_DIM) ** -0.5
PAGE_SIZE = 128
LANES = 128
SUBLANES = 8
SLAB_GROUPS = LANES // SSM_CH
N_SLABS = SSM_GROUPS // SLAB_GROUPS
SLAB_STATE = SLAB_GROUPS * SSM_STATE
VMEM_LIMIT = 56 * 1024 * 1024
ATTN_TQ = 256
ATTN_TK = 512
ATTN_HEAD_GROUP = 2
ATTN_LOOKAHEAD = 2
SAMPLE_PAGES_PER_STEP = 16
SAMPLE_LOOKAHEAD = 3
SAMPLE_PREFETCH = 3
ROW_TM = 256
MIX_TM = 512
MIX_PARTS = 2
FFN_TM = 512
FFN_TF = 512


def _dot(a, b):
    return jnp.dot(a, b, preferred_element_type=F32)


def _dot_nt(a, b):
    return lax.dot_general(a, b, (((1,), (1,)), ((), ())), preferred_element_type=F32)


def _const_spec(shape):
    n = len(shape)
    return pl.BlockSpec(shape, lambda *_: (0,) * n, pipeline_mode=pl.Buffered(1))


def _params(sem):
    return pltpu.CompilerParams(dimension_semantics=sem, vmem_limit_bytes=VMEM_LIMIT)


def _rms(x, g, eps=1e-6):
    return x * lax.rsqrt(jnp.mean(x * x, axis=-1, keepdims=True) + eps) * g


def _layer_norm(x, g, b, eps=1e-5):
    mu = jnp.mean(x, axis=-1, keepdims=True)
    xc = x - mu
    var = jnp.mean(xc * xc, axis=-1, keepdims=True)
    return xc * lax.rsqrt(var + eps) * g + b


def _inproj_kernel(x_ref, wm_ref, wkr_ref, gq_ref, gkv_ref, cos_ref, sin_ref,
                   wqn_ref, wqr_ref, wqs_ref, wuk_ref,
                   u_ref, ckv_ref, kr_ref, ckvb_ref, krb_ref, ql_ref, qr_ref, *maybe_vt_ref, q_transposed):
    xb = x_ref[...].astype(BF16)
    cos = cos_ref[...]
    sin = sin_ref[...]
    proj = _dot(xb, wm_ref[...])
    u_ref[...] = proj[:, :SSM_WIDTH]
    cq = proj[:, SSM_WIDTH:SSM_WIDTH + Q_LORA]
    ckv = _rms(proj[:, SSM_WIDTH + Q_LORA:], gkv_ref[...])
    ckv_ref[...] = ckv
    ckvb_ref[...] = ckv.astype(BF16)
    kr2 = _dot(xb, wkr_ref[...])
    krope = kr2[:, :LANES] * cos + kr2[:, LANES:] * sin
    kr_ref[...] = krope[:, :ROPE_DIM]
    krb_ref[...] = krope.astype(BF16)
    cqn = _rms(cq, gq_ref[...])
    if q_transposed:
        maybe_vt_ref[0][...] = ckv.T.astype(BF16)
        cqt = cqn.T.astype(BF16)
        qn = _dot(wqn_ref[...], cqt)
        qa = _dot(wqr_ref[...], cqt)
        qs = _dot(wqs_ref[...], cqt)
        cos_t = cos.T[:ROPE_DIM]
        sin_t = sin.T[:ROPE_DIM]
        tm = cos.shape[0]
        for h in range(N_HEADS):
            sl = slice(h * LANES, (h + 1) * LANES)
            rl = slice(h * ROPE_DIM, (h + 1) * ROPE_DIM)
            cols = slice(h * tm, (h + 1) * tm)
            ql_ref[:, cols] = _dot(wuk_ref[h], qn[sl].astype(BF16)).astype(BF16)
            qr_ref[:ROPE_DIM, cols] = (qa[rl] * cos_t + qs[rl] * sin_t).astype(BF16)
            qr_ref[ROPE_DIM:, cols] = jnp.zeros((LANES - ROPE_DIM, tm), BF16)
    else:
        cqb = cqn.astype(BF16)
        qn = _dot(cqb, wqn_ref[...])
        qa = _dot(cqb, wqr_ref[...])
        qs = _dot(cqb, wqs_ref[...])
        for h in range(N_HEADS):
            sl = slice(h * LANES, (h + 1) * LANES)
            ql_ref[h] = _dot(qn[:, sl].astype(BF16), wuk_ref[h]).astype(BF16)
            qr_ref[h] = (qa[:, sl] * cos + qs[:, sl] * sin).astype(BF16)


def _inproj(x, cos, sin, wts, tm, prompt_layout=False):
    t = x.shape[0]
    row = lambda w: pl.BlockSpec((tm, w), lambda i: (i, 0))
    extra_specs, extra_shapes = [], []
    if prompt_layout:
        assert tm == ATTN_TQ and ATTN_TK % tm == 0
        qspec = lambda w: pl.BlockSpec((None, w, N_HEADS * tm), lambda i: (i, 0, 0))
        qshape = lambda w: jax.ShapeDtypeStruct((t // tm, w, N_HEADS * tm), BF16)
        wq = [wts["w_qnT"], wts["w_qrT"], wts["w_qsT"], wts["w_uk"]]
        per_tk = ATTN_TK // tm
        extra_specs = [pl.BlockSpec((None, KV_LORA, tm), lambda i: (i // per_tk, 0, i % per_tk))]
        extra_shapes = [jax.ShapeDtypeStruct((t // ATTN_TK, KV_LORA, ATTN_TK), BF16)]
    else:
        qspec = lambda w: pl.BlockSpec((N_HEADS, tm, w), lambda i: (0, i, 0))
        qshape = lambda w: jax.ShapeDtypeStruct((N_HEADS, t, w), BF16)
        wq = [wts["w_qn"], wts["w_qr"], wts["w_qs"], wts["w_ukT"]]
    return pl.pallas_call(
        functools.partial(_inproj_kernel, q_transposed=prompt_layout),
        grid=(t // tm,),
        in_specs=[row(D_MODEL),
                  _const_spec((D_MODEL, 2048)), _const_spec((D_MODEL, 2 * LANES)),
                  _const_spec((1, Q_LORA)), _const_spec((1, KV_LORA)),
                  row(LANES), row(LANES)] + [_const_spec(w.shape) for w in wq],
        out_specs=[row(SSM_WIDTH), row(KV_LORA), row(ROPE_DIM), row(KV_LORA), row(LANES),
                   qspec(KV_LORA), qspec(LANES)] + extra_specs,
        out_shape=[jax.ShapeDtypeStruct((t, SSM_WIDTH), F32),
                   jax.ShapeDtypeStruct((t, KV_LORA), F32),
                   jax.ShapeDtypeStruct((t, ROPE_DIM), F32),
                   jax.ShapeDtypeStruct((t, KV_LORA), BF16),
                   jax.ShapeDtypeStruct((t, LANES), BF16),
                   qshape(KV_LORA), qshape(LANES)] + extra_shapes,
        compiler_params=_params(("parallel",)),
        name="inproj",
    )(x, wts["w_main"], wts["w_kr"], wts["g_q"], wts["g_kv"], cos, sin, *wq)


def _s5prep_kernel(are_ref, aim_ref, ls_ref, bre_ref, bim_ref,
                   abr_ref, abi_ref, bbr_ref, bbi_ref):
    lr = are_ref[...]
    li = aim_ref[...]
    delta = jnp.exp(ls_ref[...])
    mag = jnp.exp(lr * delta)
    ar = mag * jnp.cos(li * delta)
    ai = mag * jnp.sin(li * delta)
    abr_ref[...] = ar
    abi_ref[...] = ai
    den = lr * lr + li * li
    fr = ((ar - 1.0) * lr + ai * li) / den
    fi = (ai * lr - (ar - 1.0) * li) / den
    br = bre_ref[...]
    bi = bim_ref[...]
    bbr_ref[...] = fr * br - fi * bi
    bbi_ref[...] = fr * bi + fi * br


def _s5prep(a_re, a_im, log_step, bt_re, bt_im):
    shp = jax.ShapeDtypeStruct((SSM_GROUPS * SSM_CH, SSM_STATE), F32)
    return pl.pallas_call(_s5prep_kernel, out_shape=[shp] * 4, name="s5prep")(
        a_re, a_im, log_step, bt_re, bt_im)


def _cmul_add(ar, ai, hr, hi, br, bi):
    return ar * hr - ai * hi + br, ar * hi + ai * hr + bi


def _s5p_kernel(u_ref, wb_ref, wc_ref, a_ref, d_ref, g_ref, hl_ref, bu_scr, e_scr, u_scr, g_scr,
                *, seg_len, chunk):
    n_rows = SUBLANES * seg_len
    n_chunks = n_rows // chunk
    steps = chunk // SUBLANES
    half = SLAB_STATE
    wb = wb_ref[...]
    wc = wc_ref[...]
    d = d_ref[...]
    ar = jnp.broadcast_to(a_ref[:, :half], (SUBLANES, half))
    ai = jnp.broadcast_to(a_ref[:, half:], (SUBLANES, half))

    for seg in range(SUBLANES):
        u_scr[pl.ds(seg, seg_len, stride=SUBLANES), :] = u_ref[pl.ds(seg * seg_len, seg_len), :]

    def project_in(c):
        rows = pl.ds(c * chunk, chunk)
        bu_scr[rows, :] = _dot(u_scr[rows, :].astype(BF16), wb)

    def project_out(c):
        rows = pl.ds(c * chunk, chunk)
        y = _dot(bu_scr[rows, :].astype(BF16), wc) + d * u_scr[rows, :]
        g_scr[rows, :] = jax.nn.gelu(y)
        for seg in range(SUBLANES):
            g_ref[pl.ds(seg * seg_len + c * steps, steps), :] = (
                g_scr[pl.ds(c * chunk + seg, steps, stride=SUBLANES), :])

    def scan_chunk(c, carry, store):
        hr, hi = carry
        for t in range(c * steps, (c + 1) * steps):
            rows = pl.ds(t * SUBLANES, SUBLANES)
            hr, hi = _cmul_add(ar, ai, hr, hi, bu_scr[rows, :half], bu_scr[rows, half:])
            if store:
                bu_scr[rows, :half] = hr
                bu_scr[rows, half:] = hi
        return hr, hi

    zero = jnp.zeros((SUBLANES, half), F32)
    carry = (zero, zero)
    project_in(0)
    for c in range(n_chunks):
        if c + 1 < n_chunks:
            project_in(c + 1)
        carry = scan_chunk(c, carry, store=False)
    er, ei = carry
    pr, pi = ar, ai
    for _ in range(int(math.log2(seg_len))):
        pr, pi = pr * pr - pi * pi, 2.0 * pr * pi
    e_scr[0:SUBLANES, :half] = er
    e_scr[0:SUBLANES, half:] = ei
    cr = jnp.zeros((1, half), F32)
    ci = jnp.zeros((1, half), F32)
    e_scr[SUBLANES:SUBLANES + 1, :half] = cr
    e_scr[SUBLANES:SUBLANES + 1, half:] = ci
    for s in range(SUBLANES - 1):
        cr, ci = _cmul_add(pr[0:1], pi[0:1], cr, ci, e_scr[s:s + 1, :half], e_scr[s:s + 1, half:])
        e_scr[SUBLANES + s + 1:SUBLANES + s + 2, :half] = cr
        e_scr[SUBLANES + s + 1:SUBLANES + s + 2, half:] = ci
    carry = (e_scr[SUBLANES:2 * SUBLANES, :half], e_scr[SUBLANES:2 * SUBLANES, half:])
    for c in range(n_chunks):
        if c > 0:
            project_out(c - 1)
        carry = scan_chunk(c, carry, store=True)
    project_out(n_chunks - 1)
    hr, hi = carry
    hl_ref[:, :half] = hr
    hl_ref[:, half:] = hi


def _s5_prompt(u, s5w):
    bsz, n_rows, _ = u.shape
    seg_len = n_rows // SUBLANES
    kern = functools.partial(_s5p_kernel, seg_len=seg_len, chunk=512)
    slab = lambda shape: pl.BlockSpec((None,) + shape, lambda b, s: (s, 0, 0))
    return pl.pallas_call(
        kern,
        grid=(bsz, N_SLABS),
        in_specs=[pl.BlockSpec((None, n_rows, LANES), lambda b, s: (b, 0, s)),
                  slab((LANES, 2 * SLAB_STATE)), slab((2 * SLAB_STATE, LANES)),
                  slab((1, 2 * SLAB_STATE)), slab((1, LANES))],
        out_specs=[pl.BlockSpec((None, n_rows, LANES), lambda b, s: (b, 0, s)),
                   pl.BlockSpec((None, None, SUBLANES, 2 * SLAB_STATE), lambda b, s: (b, s, 0, 0))],
        out_shape=[jax.ShapeDtypeStruct((bsz, n_rows, SSM_WIDTH), F32),
                   jax.ShapeDtypeStruct((bsz, N_SLABS, SUBLANES, 2 * SLAB_STATE), F32)],
        scratch_shapes=[pltpu.VMEM((n_rows, 2 * SLAB_STATE), F32),
                        pltpu.VMEM((2 * SUBLANES, 2 * SLAB_STATE), F32),
                        pltpu.VMEM((n_rows, LANES), F32), pltpu.VMEM((n_rows, LANES), F32)],
        compiler_params=_params(("parallel", "parallel")),
        name="s5_prompt",
    )(u, s5w["wb"], s5w["wc"], s5w["a"], s5w["d"])


def _s5s_kernel(u_ref, wb_ref, wc_ref, a_ref, d_ref, h0r_ref, h0i_ref,
                g_ref, hlr_ref, hli_ref, h_scr, u_scr, *, n_steps, bsz):
    half = SLAB_STATE
    for t in range(n_steps):
        u_scr[pl.ds(t * bsz, bsz), :] = u_ref[pl.ds(t, bsz, stride=n_steps), :]
    u = u_scr[...]
    h_scr[...] = _dot(u.astype(BF16), wb_ref[...])
    ar = a_ref[:, :half]
    ai = a_ref[:, half:]
    hr = h0r_ref[...]
    hi = h0i_ref[...]
    for t in range(n_steps):
        rows = pl.ds(t * bsz, bsz)
        hr, hi = _cmul_add(ar, ai, hr, hi, h_scr[rows, :half], h_scr[rows, half:])
        h_scr[rows, :half] = hr
        h_scr[rows, half:] = hi
    hlr_ref[...] = hr
    hli_ref[...] = hi
    y = _dot(h_scr[...].astype(BF16), wc_ref[...]) + d_ref[...] * u
    u_scr[...] = jax.nn.gelu(y)
    for t in range(n_steps):
        g_ref[pl.ds(t, bsz, stride=n_steps), :] = u_scr[pl.ds(t * bsz, bsz), :]


def _s5_sample(u, h0_re, h0_im, s5w, n_steps):
    n_rows = u.shape[0]
    bsz = n_rows // n_steps
    kern = functools.partial(_s5s_kernel, n_steps=n_steps, bsz=bsz)
    slab = lambda shape: pl.BlockSpec((None,) + shape, lambda s: (s, 0, 0))
    col = lambda r, w: pl.BlockSpec((r, w), lambda s: (0, s))
    return pl.pallas_call(
        kern,
        grid=(N_SLABS,),
        in_specs=[col(n_rows, LANES),
                  slab((LANES, 2 * SLAB_STATE)), slab((2 * SLAB_STATE, LANES)),
                  slab((1, 2 * SLAB_STATE)), slab((1, LANES)),
                  col(bsz, SLAB_STATE), col(bsz, SLAB_STATE)],
        out_specs=[col(n_rows, LANES), col(bsz, SLAB_STATE), col(bsz, SLAB_STATE)],
        out_shape=[jax.ShapeDtypeStruct((n_rows, SSM_WIDTH), F32),
                   jax.ShapeDtypeStruct((bsz, SSM_GROUPS * SSM_STATE), F32),
                   jax.ShapeDtypeStruct((bsz, SSM_GROUPS * SSM_STATE), F32)],
        scratch_shapes=[pltpu.VMEM((n_rows, 2 * SLAB_STATE), F32), pltpu.VMEM((n_rows, LANES), F32)],
        compiler_params=_params(("parallel",)),
        name="s5_sample",
    )(u, s5w["wb"], s5w["wc"], s5w["a"], s5w["d"], h0_re, h0_im)


def _pattn_kernel(ql_ref, qr_ref, kc_ref, kr_ref, vt_ref, wuv_ref, o_ref, m_scr, l_scr, acc_scr, *, tq, tk):
    i = pl.program_id(1)
    gw = ATTN_HEAD_GROUP * tq
    n_groups = N_HEADS // ATTN_HEAD_GROUP
    m_scr[...] = jnp.full(m_scr.shape, -jnp.inf, F32)
    l_scr[...] = jnp.zeros(l_scr.shape, F32)
    acc_scr[...] = jnp.zeros(acc_scr.shape, F32)

    def block(j, lo=0, n=tk, diagonal=False):
        kc = kc_ref[j, lo:lo + n, :]
        kr = kr_ref[j, lo:lo + n, :]
        vt = vt_ref[j, :, lo:lo + n]
        if diagonal:
            keep = (lax.broadcasted_iota(jnp.int32, (n, gw), 0)
                    <= (lax.broadcasted_iota(jnp.int32, (n, gw), 1) & (tq - 1)))

        def scores(g):
            cols = slice(g * gw, (g + 1) * gw)
            s = (_dot(kc, ql_ref[:, cols]) + _dot(kr, qr_ref[:, cols])) * SCALE
            return jnp.where(keep, s, -jnp.inf) if diagonal else s

        pending = [scores(g) for g in range(ATTN_LOOKAHEAD)]
        for g in range(n_groups):
            s = pending.pop(0)
            if g + ATTN_LOOKAHEAD < n_groups:
                pending.append(scores(g + ATTN_LOOKAHEAD))
            cols = slice(g * gw, (g + 1) * gw)
            m_prev = m_scr[:, cols]
            m_new = jnp.maximum(m_prev, jnp.max(s, axis=0, keepdims=True))
            corr = jnp.exp(m_prev - m_new)
            p = jnp.exp(s - m_new)
            l_scr[:, cols] = l_scr[:, cols] * corr + jnp.sum(p, axis=0, keepdims=True)
            acc_scr[:, cols] = acc_scr[:, cols] * corr + _dot(vt, p.astype(BF16))
            m_scr[:, cols] = m_new

    def body(j, carry):
        block(j)
        return carry

    j_last = (i * tq) // tk
    lax.fori_loop(0, j_last, body, 0)

    @pl.when(i % 2 == 0)
    def _():
        block(j_last, 0, tq, diagonal=True)

    @pl.when(i % 2 == 1)
    def _():
        block(j_last, 0, tq)
        block(j_last, tq, tq, diagonal=True)

    for h in range(N_HEADS):
        cols = slice(h * tq, (h + 1) * tq)
        o_t = (acc_scr[:, cols] / l_scr[:, cols]).astype(BF16)
        y_t = _dot(wuv_ref[h], o_t)
        o_ref[:, h * V_DIM:(h + 1) * V_DIM] = y_t.T.astype(o_ref.dtype)


def _prompt_attention(ql_t, qr_t, kc, kr, v_t, wuv_t, bsz, seq, tq, tk):
    nq = seq // tq
    nk = seq // tk
    assert tk == 2 * tq and tq & (tq - 1) == 0 and N_HEADS % ATTN_HEAD_GROUP == 0
    kern = functools.partial(_pattn_kernel, tq=tq, tk=tk)
    per_batch = lambda r, c: pl.BlockSpec((None, nk, r, c), lambda b, i: (b, 0, 0, 0))
    return pl.pallas_call(
        kern,
        grid=(bsz, nq),
        in_specs=[pl.BlockSpec((None, KV_LORA, N_HEADS * tq), lambda b, i: (b * nq + i, 0, 0)),
                  pl.BlockSpec((None, LANES, N_HEADS * tq), lambda b, i: (b * nq + i, 0, 0)),
                  per_batch(tk, KV_LORA), per_batch(tk, LANES), per_batch(KV_LORA, tk),
                  _const_spec((N_HEADS, V_DIM, KV_LORA))],
        out_specs=pl.BlockSpec((tq, N_HEADS * V_DIM), lambda b, i: (b * nq + i, 0)),
        out_shape=jax.ShapeDtypeStruct((bsz * seq, N_HEADS * V_DIM), BF16),
        scratch_shapes=[pltpu.VMEM((1, N_HEADS * tq), F32), pltpu.VMEM((1, N_HEADS * tq), F32),
                        pltpu.VMEM((KV_LORA, N_HEADS * tq), F32)],
        compiler_params=_params(("parallel", "arbitrary")),
        name="prompt_attention",
    )(ql_t, qr_t, kc, kr, v_t, wuv_t)


N_CHAINS = 4


def _sattn_kernel(pt_ref, ql_ref, qr_ref, kn_ref, krn_ref, ck_hbm, krt_hbm, o_ref,
                  kbuf, rbuf, sems, m_scr, l_scr, acc_scr, kc_scr, *, n_new, n_pages, chunk):
    b = pl.program_id(0)
    n_rows = N_HEADS * n_new
    n_chunks = n_pages // chunk

    def page_copies(pid, slot, k):
        rows = pl.ds(k * PAGE_SIZE, PAGE_SIZE)
        return (pltpu.make_async_copy(ck_hbm.at[pid], kbuf.at[slot, rows, :], sems.at[0, slot]),
                pltpu.make_async_copy(krt_hbm.at[pid], rbuf.at[slot, k], sems.at[1, slot]))

    def start_chunk(row, j):
        for k in range(chunk):
            for cp in page_copies(pt_ref[row, j * chunk + k], j, k):
                cp.start()

    def wait_chunk(j):
        for k in range(chunk):
            for cp in page_copies(0, j, k):
                cp.wait()

    def start_ahead(j):
        ahead = j + SAMPLE_PREFETCH
        if ahead < n_chunks:
            start_chunk(b, ahead)
        else:
            @pl.when(b + 1 < pl.num_programs(0))
            def _():
                start_chunk(b + 1, ahead - n_chunks)

    @pl.when(b == 0)
    def _():
        for j in range(SAMPLE_PREFETCH):
            start_chunk(0, j)

    ql = ql_ref[...]
    qr = qr_ref[:, :ROPE_DIM]

    kc = kn_ref[...].astype(BF16)
    s = (_dot_nt(ql, kc) + _dot_nt(qr, krn_ref[...].astype(BF16))) * SCALE
    q_tok = lax.broadcasted_iota(jnp.int32, (n_rows, n_new), 0) & (n_new - 1)
    k_tok = lax.broadcasted_iota(jnp.int32, (n_rows, n_new), 1)
    s = jnp.where(k_tok <= q_tok, s, -jnp.inf)
    m = jnp.max(s, axis=-1, keepdims=True)
    p = jnp.exp(s - m)
    m_scr[0] = m
    l_scr[0] = jnp.sum(p, axis=-1, keepdims=True)
    acc_scr[0] = _dot(p.astype(BF16), kc)
    for c in range(1, N_CHAINS):
        m_scr[c] = jnp.full((n_rows, 1), -jnp.inf, F32)
        l_scr[c] = jnp.zeros((n_rows, 1), F32)
        acc_scr[c] = jnp.zeros((n_rows, KV_LORA), F32)

    per_chain = chunk // N_CHAINS
    chain_rows = per_chain * PAGE_SIZE

    def scores(t):
        j, c = divmod(t, N_CHAINS)
        if c == 0:
            start_ahead(j)
            wait_chunk(j)
        rows = slice(c * chain_rows, (c + 1) * chain_rows)
        kc_scr[rows, :] = kbuf[j, rows, :].astype(BF16)
        kc = kc_scr[rows, :]
        kr_t = jnp.concatenate([rbuf[j, k] for k in range(c * per_chain, (c + 1) * per_chain)],
                               axis=1).astype(BF16)
        return (_dot_nt(ql, kc) + _dot(qr, kr_t)) * SCALE, kc

    n_blocks = n_chunks * N_CHAINS
    pending = [scores(t) for t in range(SAMPLE_LOOKAHEAD)]
    for t in range(n_blocks):
        s, kc = pending.pop(0)
        if t + SAMPLE_LOOKAHEAD < n_blocks:
            pending.append(scores(t + SAMPLE_LOOKAHEAD))
        c = t % N_CHAINS
        m_prev = m_scr[c]
        m_new = jnp.maximum(m_prev, jnp.max(s, axis=-1, keepdims=True))
        corr = jnp.exp(m_prev - m_new)
        p = jnp.exp(s - m_new)
        l_scr[c] = l_scr[c] * corr + jnp.sum(p, axis=-1, keepdims=True)
        acc_scr[c] = acc_scr[c] * corr + _dot(p.astype(BF16), kc)
        m_scr[c] = m_new

    m = m_scr[0]
    for c in range(1, N_CHAINS):
        m = jnp.maximum(m, m_scr[c])
    l = jnp.zeros((n_rows, 1), F32)
    acc = jnp.zeros((n_rows, KV_LORA), F32)
    for c in range(N_CHAINS):
        w = jnp.exp(m_scr[c] - m)
        l = l + l_scr[c] * w
        acc = acc + acc_scr[c] * w
    o = acc / l
    for h in range(N_HEADS):
        o_ref[h] = o[h * n_new:(h + 1) * n_new]


def _value_up_kernel(o_ref, wuv_ref, y_ref):
    y_ref[...] = _dot(o_ref[...].astype(BF16), wuv_ref[...])


def _value_up(o_lat, wuv):
    _, t, _ = o_lat.shape
    return pl.pallas_call(
        _value_up_kernel,
        grid=(N_HEADS,),
        in_specs=[pl.BlockSpec((None, t, KV_LORA), lambda h: (h, 0, 0)),
                  pl.BlockSpec((None, KV_LORA, V_DIM), lambda h: (h, 0, 0))],
        out_specs=pl.BlockSpec((t, V_DIM), lambda h: (0, h)),
        out_shape=jax.ShapeDtypeStruct((t, N_HEADS * V_DIM), F32),
        compiler_params=_params(("parallel",)),
        name="value_up",
    )(o_lat, wuv)


def _sample_attention(page_table, ql, qr, kc_new, kr_new, cache_ckv, cache_kr_t, n_new, chunk):
    bsz, n_pages = page_table.shape
    n_rows = N_HEADS * n_new
    n_chunks = n_pages // chunk
    assert n_pages % chunk == 0 and chunk % N_CHAINS == 0 and 0 < SAMPLE_PREFETCH < n_chunks
    kern = functools.partial(_sattn_kernel, n_new=n_new, n_pages=n_pages, chunk=chunk)
    in_specs = [pl.BlockSpec((None, n_rows, KV_LORA), lambda b, pt: (b, 0, 0)),
                pl.BlockSpec((None, n_rows, LANES), lambda b, pt: (b, 0, 0)),
                pl.BlockSpec((n_new, KV_LORA), lambda b, pt: (b, 0)),
                pl.BlockSpec((n_new, ROPE_DIM), lambda b, pt: (b, 0)),
                pl.BlockSpec(memory_space=pl.ANY), pl.BlockSpec(memory_space=pl.ANY)]
    return pl.pallas_call(
        kern,
        grid_spec=pltpu.PrefetchScalarGridSpec(
            num_scalar_prefetch=1,
            grid=(bsz,),
            in_specs=in_specs,
            out_specs=pl.BlockSpec((N_HEADS, n_new, KV_LORA), lambda b, pt: (0, b, 0)),
            scratch_shapes=[pltpu.VMEM((n_chunks, chunk * PAGE_SIZE, KV_LORA), F32),
                            pltpu.VMEM((n_chunks, chunk, ROPE_DIM, PAGE_SIZE), F32),
                            pltpu.SemaphoreType.DMA((2, n_chunks)),
                            pltpu.VMEM((N_CHAINS, n_rows, 1), F32), pltpu.VMEM((N_CHAINS, n_rows, 1), F32),
                            pltpu.VMEM((N_CHAINS, n_rows, KV_LORA), F32),
                            pltpu.VMEM((chunk * PAGE_SIZE, KV_LORA), BF16)]),
        out_shape=jax.ShapeDtypeStruct((N_HEADS, bsz * n_new, KV_LORA), F32),
        compiler_params=_params(("arbitrary",)),
        name="sample_attention",
    )(page_table, ql, qr, kc_new, kr_new, cache_ckv, cache_kr_t)


def _mix_kernel(g_ref, ya_ref, x_ref, wglu_ref, bglu_ref, wout_ref, lg_ref, lb_ref, o_ref, ob_ref, *, alpha):
    part = g_ref.shape[0] // MIX_PARTS
    rows = [pl.ds(r * part, part) for r in range(MIX_PARTS)]

    def gate(r):
        g = g_ref[rows[r], :]
        return g, _dot(g.astype(BF16), wglu_ref[...])

    def glu(g, z):
        return (g * jax.nn.sigmoid(z + bglu_ref[...])).astype(BF16)

    def project(r, y):
        return (_dot(y, wout_ref[:SSM_WIDTH, :])
                + _dot(ya_ref[rows[r], :].astype(BF16), wout_ref[SSM_WIDTH:, :]))

    def finish(r, mix):
        x1 = _layer_norm(alpha * x_ref[rows[r], :] + mix, lg_ref[...], lb_ref[...])
        o_ref[rows[r], :] = x1
        ob_ref[rows[r], :] = x1.astype(BF16)

    gated = [gate(r) for r in range(MIX_PARTS)]
    mixes = []
    for r in range(MIX_PARTS):
        mixes.append(project(r, glu(*gated[r])))
        if r > 0:
            finish(r - 1, mixes[r - 1])
    finish(MIX_PARTS - 1, mixes[-1])


def _mix(g, y_att, x, wts, alpha, tm):
    t = x.shape[0]
    row = lambda w: pl.BlockSpec((tm, w), lambda i: (i, 0))
    return pl.pallas_call(
        functools.partial(_mix_kernel, alpha=alpha),
        grid=(t // tm,),
        in_specs=[row(SSM_WIDTH), row(N_HEADS * V_DIM), row(D_MODEL),
                  _const_spec((SSM_WIDTH, SSM_WIDTH)), _const_spec((1, SSM_WIDTH)),
                  _const_spec((D_MODEL, D_MODEL)), _const_spec((1, D_MODEL)), _const_spec((1, D_MODEL))],
        out_specs=[row(D_MODEL), row(D_MODEL)],
        out_shape=[jax.ShapeDtypeStruct((t, D_MODEL), F32), jax.ShapeDtypeStruct((t, D_MODEL), BF16)],
        compiler_params=_params(("parallel",)),
        name="mix",
    )(g, y_att, x, wts["w_glu"], wts["b_glu"], wts["w_out"], wts["ln1_g"], wts["ln1_b"])


def _ffn_kernel(x_ref, xb_ref, wg_ref, wu_ref, wd_ref, lg_ref, lb_ref, o_ref, *, alpha):
    f = pl.program_id(1)

    @pl.when(f == 0)
    def _():
        o_ref[...] = jnp.zeros(o_ref.shape, F32)

    xb = xb_ref[...]
    gate = _dot(xb, wg_ref[...])
    up = _dot(xb, wu_ref[...])
    o_ref[...] += _dot((jax.nn.silu(gate) * up).astype(BF16), wd_ref[...])

    @pl.when(f == pl.num_programs(1) - 1)
    def _():
        o_ref[...] = _layer_norm(alpha * x_ref[...] + o_ref[...], lg_ref[...], lb_ref[...])


def _ffn(x, xb, wts, alpha, tm, tf):
    t = x.shape[0]
    d_ff = wts["w_gate"].shape[1]
    return pl.pallas_call(
        functools.partial(_ffn_kernel, alpha=alpha),
        grid=(t // tm, d_ff // tf),
        in_specs=[pl.BlockSpec((tm, D_MODEL), lambda i, f: (i, 0)),
                  pl.BlockSpec((tm, D_MODEL), lambda i, f: (i, 0)),
                  pl.BlockSpec((D_MODEL, tf), lambda i, f: (0, f)),
                  pl.BlockSpec((D_MODEL, tf), lambda i, f: (0, f)),
                  pl.BlockSpec((tf, D_MODEL), lambda i, f: (f, 0)),
                  pl.BlockSpec((1, D_MODEL), lambda i, f: (0, 0)),
                  pl.BlockSpec((1, D_MODEL), lambda i, f: (0, 0))],
        out_specs=pl.BlockSpec((tm, D_MODEL), lambda i, f: (i, 0)),
        out_shape=jax.ShapeDtypeStruct((t, D_MODEL), F32),
        compiler_params=_params(("parallel", "arbitrary")),
        name="ffn",
    )(x, xb, wts["w_gate"], wts["w_up"], wts["w_down"], wts["ln2_g"], wts["ln2_b"])


def _swap_halves(w):
    half = w.shape[-1] // 2
    return jnp.concatenate([w[..., half:], w[..., :half]], axis=-1)


def _pad_lanes(w):
    return jnp.concatenate([w, jnp.zeros(w.shape[:-1] + (LANES - w.shape[-1],), w.dtype)], axis=-1)


def _layer_weights(w_in, g_q, w_uq, w_uk, g_kv, w_uv, w_glu, b_glu, w_out, ln1_g, ln1_b,
                   w_gate, w_up, w_down, ln2_g, ln2_b):
    n_main = SSM_WIDTH + Q_LORA + KV_LORA
    w_kr = w_in[:, n_main:]
    uq = w_uq.reshape(Q_LORA, N_HEADS, NOPE_DIM + ROPE_DIM)
    uq_rope = uq[:, :, NOPE_DIM:]
    w_qn = uq[:, :, :NOPE_DIM].reshape(Q_LORA, N_HEADS * NOPE_DIM).astype(BF16)
    w_qr = _pad_lanes(uq_rope).reshape(Q_LORA, N_HEADS * LANES).astype(BF16)
    w_qs = _pad_lanes(_swap_halves(uq_rope)).reshape(Q_LORA, N_HEADS * LANES).astype(BF16)
    return {
        "w_main": w_in[:, :n_main].astype(BF16),
        "w_kr": jnp.concatenate([_pad_lanes(w_kr), _pad_lanes(_swap_halves(w_kr))], axis=1).astype(BF16),
        "g_q": g_q.reshape(1, Q_LORA), "g_kv": g_kv.reshape(1, KV_LORA),
        "w_qn": w_qn, "w_qr": w_qr, "w_qs": w_qs,
        "w_qnT": w_qn.T,
        "w_qrT": uq_rope.reshape(Q_LORA, N_HEADS * ROPE_DIM).T.astype(BF16),
        "w_qsT": _swap_halves(uq_rope).reshape(Q_LORA, N_HEADS * ROPE_DIM).T.astype(BF16),
        "w_ukT": jnp.transpose(w_uk, (1, 2, 0)).astype(BF16),
        "w_uk": jnp.transpose(w_uk, (1, 0, 2)).astype(BF16),
        "w_uv": jnp.transpose(w_uv, (1, 0, 2)).astype(BF16),
        "w_uvT": jnp.transpose(w_uv, (1, 2, 0)).astype(BF16),
        "w_glu": w_glu.astype(BF16), "b_glu": b_glu.reshape(1, SSM_WIDTH),
        "w_out": w_out.astype(BF16),
        "ln1_g": ln1_g.reshape(1, D_MODEL), "ln1_b": ln1_b.reshape(1, D_MODEL),
        "w_gate": w_gate.astype(BF16), "w_up": w_up.astype(BF16), "w_down": w_down.astype(BF16),
        "ln2_g": ln2_g.reshape(1, D_MODEL), "ln2_b": ln2_b.reshape(1, D_MODEL),
    }


def _s5_weights(a_re, a_im, log_step, b_re, b_im, c_re, c_im, d):
    rep = lambda a: jnp.repeat(a, SSM_CH, axis=0)
    bt = lambda b: jnp.transpose(b, (0, 2, 1)).reshape(SSM_GROUPS * SSM_CH, SSM_STATE)
    abr, abi, bbr, bbi = _s5prep(rep(a_re), rep(a_im), rep(log_step.reshape(SSM_GROUPS, 1)),
                                 bt(b_re), bt(b_im))
    abr = abr[::SSM_CH]
    abi = abi[::SSM_CH]
    eye = jnp.eye(SLAB_GROUPS, dtype=F32)

    def blockdiag_in(b):
        b4 = b.reshape(N_SLABS, SLAB_GROUPS, SSM_CH, SSM_STATE)
        return jnp.einsum("sghp,gk->sghkp", b4, eye).reshape(N_SLABS, LANES, SLAB_STATE)

    def blockdiag_out(c):
        c4 = c.reshape(N_SLABS, SLAB_GROUPS, SSM_CH, SSM_STATE)
        return jnp.einsum("sghp,gk->skpgh", c4, eye).reshape(N_SLABS, SLAB_STATE, LANES)

    slab_row = lambda a: a.reshape(N_SLABS, 1, SLAB_STATE)
    return {
        "wb": jnp.concatenate([blockdiag_in(bbr), blockdiag_in(bbi)], axis=2).astype(BF16),
        "wc": jnp.concatenate([blockdiag_out(c_re), -blockdiag_out(c_im)], axis=1).astype(BF16),
        "a": jnp.concatenate([slab_row(abr), slab_row(abi)], axis=2),
        "d": d.reshape(N_SLABS, 1, LANES),
    }


def _rope_tables(pos):
    half = ROPE_DIM // 2
    inv = ROPE_BASE ** (-2.0 * jnp.arange(half, dtype=F32) / ROPE_DIM)
    ang = pos.astype(F32)[:, None] * inv[None, :]
    cos = jnp.cos(ang)
    sin = jnp.sin(ang)
    return (_pad_lanes(jnp.concatenate([cos, cos], axis=1)),
            _pad_lanes(jnp.concatenate([-sin, sin], axis=1)))


def kernel(x_prompt, x_sample, cache_ckv, cache_krope, state_ssm_re, state_ssm_im, page_table, w_in, g_q, w_uq, w_uk, g_kv, w_uv, ssm_a_re, ssm_a_im, ssm_log_step, ssm_b_re, ssm_b_im, ssm_c_re, ssm_c_im, ssm_d, w_glu, b_glu, w_out, ln1_g, ln1_b, w_gate, w_up, w_down, ln2_g, ln2_b):
    depth = w_in.shape[0]
    alpha = (2 * depth) ** 0.25
    bsz, seq, _ = x_prompt.shape
    dbsz, dseq, _ = x_sample.shape
    past = page_table.shape[1] * PAGE_SIZE
    cos_p, sin_p = _rope_tables(jnp.arange(seq, dtype=jnp.int32))
    cos_p = jnp.tile(cos_p, (bsz, 1))
    sin_p = jnp.tile(sin_p, (bsz, 1))
    cos_s, sin_s = _rope_tables(past + jnp.arange(dseq, dtype=jnp.int32))
    cos_s = jnp.tile(cos_s, (dbsz, 1))
    sin_s = jnp.tile(sin_s, (dbsz, 1))

    y_p = x_prompt.reshape(bsz * seq, D_MODEL)
    y_s = x_sample.reshape(dbsz * dseq, D_MODEL)
    outs = [[] for _ in range(8)]
    for l in range(depth):
        wts = _layer_weights(w_in[l], g_q[l], w_uq[l], w_uk[l], g_kv[l], w_uv[l], w_glu[l], b_glu[l],
                             w_out[l], ln1_g[l], ln1_b[l], w_gate[l], w_up[l], w_down[l], ln2_g[l], ln2_b[l])
        s5w = _s5_weights(ssm_a_re[l], ssm_a_im[l], ssm_log_step[l], ssm_b_re[l], ssm_b_im[l],
                          ssm_c_re[l], ssm_c_im[l], ssm_d[l])

        u, ckv, kr, ckvb, krb, ql_t, qr_t, v_t = _inproj(y_p, cos_p, sin_p, wts, tm=ATTN_TQ, prompt_layout=True)
        g, hl = _s5_prompt(u.reshape(bsz, seq, SSM_WIDTH), s5w)
        nk = seq // ATTN_TK
        y_att = _prompt_attention(ql_t, qr_t, ckvb.reshape(bsz, nk, ATTN_TK, KV_LORA),
                                  krb.reshape(bsz, nk, ATTN_TK, LANES), v_t.reshape(bsz, nk, KV_LORA, ATTN_TK),
                                  wts["w_uvT"], bsz, seq, tq=ATTN_TQ, tk=ATTN_TK)
        x1, x1b = _mix(g.reshape(bsz * seq, SSM_WIDTH), y_att, y_p, wts, alpha, tm=MIX_TM)
        y_p = _ffn(x1, x1b, wts, alpha, tm=FFN_TM, tf=FFN_TF)
        h_last = hl[:, :, SUBLANES - 1, :]
        outs[0].append(ckv.reshape(bsz, seq, KV_LORA))
        outs[1].append(kr.reshape(bsz, seq, ROPE_DIM))
        outs[2].append(h_last[:, :, :SLAB_STATE].reshape(bsz, SSM_GROUPS, SSM_STATE))
        outs[3].append(h_last[:, :, SLAB_STATE:].reshape(bsz, SSM_GROUPS, SSM_STATE))

        u, ckv, kr, _, _, ql, qr = _inproj(y_s, cos_s, sin_s, wts, tm=ROW_TM)
        g, hl_re, hl_im = _s5_sample(
            u, state_ssm_re[l].reshape(dbsz, SSM_GROUPS * SSM_STATE),
            state_ssm_im[l].reshape(dbsz, SSM_GROUPS * SSM_STATE), s5w, dseq)
        per_batch = lambda q: q.reshape(N_HEADS, dbsz, dseq, q.shape[-1]).transpose(1, 0, 2, 3).reshape(
            dbsz, N_HEADS * dseq, q.shape[-1])
        o_lat = _sample_attention(page_table, per_batch(ql), per_batch(qr), ckv, kr,
                                  cache_ckv[l], jnp.swapaxes(cache_krope[l], 1, 2), dseq,
                                  chunk=SAMPLE_PAGES_PER_STEP)
        y_att = _value_up(o_lat, wts["w_uv"])
        x1, x1b = _mix(g, y_att, y_s, wts, alpha, tm=MIX_TM)
        y_s = _ffn(x1, x1b, wts, alpha, tm=FFN_TM, tf=FFN_TF)
        outs[4].append(ckv.reshape(dbsz, dseq, KV_LORA))
        outs[5].append(kr.reshape(dbsz, dseq, ROPE_DIM))
        outs[6].append(hl_re.reshape(dbsz, SSM_GROUPS, SSM_STATE))
        outs[7].append(hl_im.reshape(dbsz, SSM_GROUPS, SSM_STATE))

    return (y_p.reshape(bsz, seq, D_MODEL), y_s.reshape(dbsz, dseq, D_MODEL),
            *[jnp.stack(o) for o in outs])
```

```python
import functools
import math

import jax
import jax.numpy as jnp
from jax import lax
from jax.experimental import pallas as pl
from jax.experimental.pallas import tpu as pltpu

F32 = jnp.float32
BF16 = jnp.bfloat16

D_MODEL = 2048
SSM_WIDTH = 1024
SSM_CH = 16
SSM_GROUPS = 64
SSM_STATE = 64
N_HEADS = 8
NOPE_DIM = 128
ROPE_DIM = 64
V_DIM = 128
Q_LORA = 512
KV_LORA = 512
ROPE_BASE = 10000.0
SCALE = (NOPE_DIM + ROPE_DIM) ** -0.5
PAGE_SIZE = 128
LANES = 128
SUBLANES = 8
SLAB_GROUPS = LANES // SSM_CH
N_SLABS = SSM_GROUPS // SLAB_GROUPS
SLAB_STATE = SLAB_GROUPS * SSM_STATE
VMEM_LIMIT = 56 * 1024 * 1024
ATTN_TQ = 256
ATTN_TK = 512
ATTN_HEAD_GROUP = 2
ATTN_LOOKAHEAD = 2
SAMPLE_PAGES_PER_STEP = 16
SAMPLE_LOOKAHEAD = 3
SAMPLE_PREFETCH = 3
ROW_TM = 256
MIX_TM = 512
MIX_PARTS = 2
FFN_TM = 512
FFN_TF = 512


def _dot(a, b):
    return jnp.dot(a, b, preferred_element_type=F32)


def _dot_nt(a, b):
    return lax.dot_general(a, b, (((1,), (1,)), ((), ())), preferred_element_type=F32)


def _const_spec(shape):
    n = len(shape)
    return pl.BlockSpec(shape, lambda *_: (0,) * n, pipeline_mode=pl.Buffered(1))


def _params(sem):
    return pltpu.CompilerParams(dimension_semantics=sem, vmem_limit_bytes=VMEM_LIMIT)


def _rms(x, g, eps=1e-6):
    return x * lax.rsqrt(jnp.mean(x * x, axis=-1, keepdims=True) + eps) * g


def _layer_norm(x, g, b, eps=1e-5):
    mu = jnp.mean(x, axis=-1, keepdims=True)
    xc = x - mu
    var = jnp.mean(xc * xc, axis=-1, keepdims=True)
    return xc * lax.rsqrt(var + eps) * g + b


def _rope_table_kernel(inv_ref, sgn_ref, cos_ref, sin_ref, *, pos_offset, pos_period):
    row = lax.broadcasted_iota(jnp.int32, cos_ref.shape, 0)
    ang = (pos_offset + (row & (pos_period - 1))).astype(F32) * inv_ref[...]
    cos_ref[...] = jnp.cos(ang)
    sin_ref[...] = jnp.sin(ang) * sgn_ref[...]


def _rope_table(rope_rows, n_rows, pos_offset, pos_period):
    assert pos_period & (pos_period - 1) == 0
    shape = jax.ShapeDtypeStruct((n_rows, LANES), F32)
    return pl.pallas_call(
        functools.partial(_rope_table_kernel, pos_offset=pos_offset, pos_period=pos_period),
        out_shape=[shape, shape], name="rope_table")(*rope_rows)


def _inproj_kernel(x_ref, wm_ref, wkr_ref, gq_ref, gkv_ref, cos_ref, sin_ref,
                   wqn_ref, wqr_ref, wqs_ref, wuk_ref,
                   u_ref, ckv_ref, kr_ref, ckvb_ref, krb_ref, ql_ref, qr_ref, *maybe_vt_ref, q_transposed):
    xb = x_ref[...].astype(BF16)
    cos = cos_ref[...]
    sin = sin_ref[...]
    proj = _dot(xb, wm_ref[...])
    u_ref[...] = proj[:, :SSM_WIDTH]
    cq = proj[:, SSM_WIDTH:SSM_WIDTH + Q_LORA]
    ckv = _rms(proj[:, SSM_WIDTH + Q_LORA:], gkv_ref[...])
    ckv_ref[...] = ckv
    ckvb_ref[...] = ckv.astype(BF16)
    kr2 = _dot(xb, wkr_ref[...])
    krope = kr2[:, :LANES] * cos + kr2[:, LANES:] * sin
    kr_ref[...] = krope[:, :ROPE_DIM]
    krb_ref[...] = krope.astype(BF16)
    cqn = _rms(cq, gq_ref[...])
    if q_transposed:
        maybe_vt_ref[0][...] = ckv.T.astype(BF16)
        cqt = cqn.T.astype(BF16)
        qn = _dot(wqn_ref[...], cqt)
        qa = _dot(wqr_ref[...], cqt)
        qs = _dot(wqs_ref[...], cqt)
        cos_t = cos.T[:ROPE_DIM]
        sin_t = sin.T[:ROPE_DIM]
        tm = cos.shape[0]
        for h in range(N_HEADS):
            sl = slice(h * LANES, (h + 1) * LANES)
            rl = slice(h * ROPE_DIM, (h + 1) * ROPE_DIM)
            cols = slice(h * tm, (h + 1) * tm)
            ql_ref[:, cols] = _dot(wuk_ref[h], qn[sl].astype(BF16)).astype(BF16)
            qr_ref[:ROPE_DIM, cols] = (qa[rl] * cos_t + qs[rl] * sin_t).astype(BF16)
            qr_ref[ROPE_DIM:, cols] = jnp.zeros((LANES - ROPE_DIM, tm), BF16)
    else:
        cqb = cqn.astype(BF16)
        qn = _dot(cqb, wqn_ref[...])
        qa = _dot(cqb, wqr_ref[...])
        qs = _dot(cqb, wqs_ref[...])
        for h in range(N_HEADS):
            sl = slice(h * LANES, (h + 1) * LANES)
            ql_ref[h] = _dot(qn[:, sl].astype(BF16), wuk_ref[h]).astype(BF16)
            qr_ref[h] = (qa[:, sl] * cos + qs[:, sl] * sin).astype(BF16)


def _inproj(x, cos, sin, wts, tm, prompt_layout=False):
    t = x.shape[0]
    table_tiles = cos.shape[0] // tm
    table = pl.BlockSpec((tm, LANES), lambda i: (i % table_tiles, 0))
    row = lambda w: pl.BlockSpec((tm, w), lambda i: (i, 0))
    extra_specs, extra_shapes = [], []
    if prompt_layout:
        assert tm == ATTN_TQ and ATTN_TK % tm == 0
        qspec = lambda w: pl.BlockSpec((None, w, N_HEADS * tm), lambda i: (i, 0, 0))
        qshape = lambda w: jax.ShapeDtypeStruct((t // tm, w, N_HEADS * tm), BF16)
        wq = [wts["w_qnT"], wts["w_qrT"], wts["w_qsT"], wts["w_uk"]]
        per_tk = ATTN_TK // tm
        extra_specs = [pl.BlockSpec((None, KV_LORA, tm), lambda i: (i // per_tk, 0, i % per_tk))]
        extra_shapes = [jax.ShapeDtypeStruct((t // ATTN_TK, KV_LORA, ATTN_TK), BF16)]
    else:
        qspec = lambda w: pl.BlockSpec((N_HEADS, tm, w), lambda i: (0, i, 0))
        qshape = lambda w: jax.ShapeDtypeStruct((N_HEADS, t, w), BF16)
        wq = [wts["w_qn"], wts["w_qr"], wts["w_qs"], wts["w_ukT"]]
    return pl.pallas_call(
        functools.partial(_inproj_kernel, q_transposed=prompt_layout),
        grid=(t // tm,),
        in_specs=[row(D_MODEL),
                  _const_spec((D_MODEL, 2048)), _const_spec((D_MODEL, 2 * LANES)),
                  _const_spec((1, Q_LORA)), _const_spec((1, KV_LORA)),
                  table, table] + [_const_spec(w.shape) for w in wq],
        out_specs=[row(SSM_WIDTH), row(KV_LORA), row(ROPE_DIM), row(KV_LORA), row(LANES),
                   qspec(KV_LORA), qspec(LANES)] + extra_specs,
        out_shape=[jax.ShapeDtypeStruct((t, SSM_WIDTH), F32),
                   jax.ShapeDtypeStruct((t, KV_LORA), F32),
                   jax.ShapeDtypeStruct((t, ROPE_DIM), F32),
                   jax.ShapeDtypeStruct((t, KV_LORA), BF16),
                   jax.ShapeDtypeStruct((t, LANES), BF16),
                   qshape(KV_LORA), qshape(LANES)] + extra_shapes,
        compiler_params=_params(("parallel",)),
        name="inproj",
    )(x, wts["w_main"], wts["w_kr"], wts["g_q"], wts["g_kv"], cos, sin, *wq)


def _s5prep_kernel(are_ref, aim_ref, ls_ref, bre_ref, bim_ref,
                   abr_ref, abi_ref, bbr_ref, bbi_ref):
    lr = are_ref[...]
    li = aim_ref[...]
    delta = jnp.exp(ls_ref[...])
    mag = jnp.exp(lr * delta)
    ar = mag * jnp.cos(li * delta)
    ai = mag * jnp.sin(li * delta)
    abr_ref[...] = ar
    abi_ref[...] = ai
    den = lr * lr + li * li
    fr = ((ar - 1.0) * lr + ai * li) / den
    fi = (ai * lr - (ar - 1.0) * li) / den
    br = bre_ref[...]
    bi = bim_ref[...]
    bbr_ref[...] = fr * br - fi * bi
    bbi_ref[...] = fr * bi + fi * br


def _s5prep(a_re, a_im, log_step, bt_re, bt_im):
    shp = jax.ShapeDtypeStruct((SSM_GROUPS * SSM_CH, SSM_STATE), F32)
    return pl.pallas_call(_s5prep_kernel, out_shape=[shp] * 4, name="s5prep")(
        a_re, a_im, log_step, bt_re, bt_im)


def _cmul_add(ar, ai, hr, hi, br, bi):
    return ar * hr - ai * hi + br, ar * hi + ai * hr + bi


def _s5p_kernel(u_ref, wb_ref, wc_ref, a_ref, d_ref, g_ref, hl_ref, bu_scr, e_scr, u_scr, g_scr,
                *, seg_len, chunk):
    n_rows = SUBLANES * seg_len
    n_chunks = n_rows // chunk
    steps = chunk // SUBLANES
    half = SLAB_STATE
    wb = wb_ref[...]
    wc = wc_ref[...]
    d = d_ref[...]
    ar = jnp.broadcast_to(a_ref[:, :half], (SUBLANES, half))
    ai = jnp.broadcast_to(a_ref[:, half:], (SUBLANES, half))

    for seg in range(SUBLANES):
        u_scr[pl.ds(seg, seg_len, stride=SUBLANES), :] = u_ref[pl.ds(seg * seg_len, seg_len), :]

    def project_in(c):
        rows = pl.ds(c * chunk, chunk)
        bu_scr[rows, :] = _dot(u_scr[rows, :].astype(BF16), wb)

    def project_out(c):
        rows = pl.ds(c * chunk, chunk)
        y = _dot(bu_scr[rows, :].astype(BF16), wc) + d * u_scr[rows, :]
        g_scr[rows, :] = jax.nn.gelu(y)
        for seg in range(SUBLANES):
            g_ref[pl.ds(seg * seg_len + c * steps, steps), :] = (
                g_scr[pl.ds(c * chunk + seg, steps, stride=SUBLANES), :])

    def scan_chunk(c, carry, store):
        hr, hi = carry
        for t in range(c * steps, (c + 1) * steps):
            rows = pl.ds(t * SUBLANES, SUBLANES)
            hr, hi = _cmul_add(ar, ai, hr, hi, bu_scr[rows, :half], bu_scr[rows, half:])
            if store:
                bu_scr[rows, :half] = hr
                bu_scr[rows, half:] = hi
        return hr, hi

    zero = jnp.zeros((SUBLANES, half), F32)
    carry = (zero, zero)
    project_in(0)
    for c in range(n_chunks):
        if c + 1 < n_chunks:
            project_in(c + 1)
        carry = scan_chunk(c, carry, store=False)
    er, ei = carry
    pr, pi = ar, ai
    for _ in range(int(math.log2(seg_len))):
        pr, pi = pr * pr - pi * pi, 2.0 * pr * pi
    e_scr[0:SUBLANES, :half] = er
    e_scr[0:SUBLANES, half:] = ei
    cr = jnp.zeros((1, half), F32)
    ci = jnp.zeros((1, half), F32)
    e_scr[SUBLANES:SUBLANES + 1, :half] = cr
    e_scr[SUBLANES:SUBLANES + 1, half:] = ci
    for s in range(SUBLANES - 1):
        cr, ci = _cmul_add(pr[0:1], pi[0:1], cr, ci, e_scr[s:s + 1, :half], e_scr[s:s + 1, half:])
        e_scr[SUBLANES + s + 1:SUBLANES + s + 2, :half] = cr
        e_scr[SUBLANES + s + 1:SUBLANES + s + 2, half:] = ci
    carry = (e_scr[SUBLANES:2 * SUBLANES, :half], e_scr[SUBLANES:2 * SUBLANES, half:])
    for c in range(n_chunks):
        if c > 0:
            project_out(c - 1)
        carry = scan_chunk(c, carry, store=True)
    project_out(n_chunks - 1)
    hr, hi = carry
    hl_ref[:, :half] = hr
    hl_ref[:, half:] = hi


def _s5_prompt(u, s5w):
    bsz, n_rows, _ = u.shape
    seg_len = n_rows // SUBLANES
    kern = functools.partial(_s5p_kernel, seg_len=seg_len, chunk=512)
    slab = lambda shape: pl.BlockSpec((None,) + shape, lambda b, s: (s, 0, 0))
    return pl.pallas_call(
        kern,
        grid=(bsz, N_SLABS),
        in_specs=[pl.BlockSpec((None, n_rows, LANES), lambda b, s: (b, 0, s)),
                  slab((LANES, 2 * SLAB_STATE)), slab((2 * SLAB_STATE, LANES)),
                  slab((1, 2 * SLAB_STATE)), slab((1, LANES))],
        out_specs=[pl.BlockSpec((None, n_rows, LANES), lambda b, s: (b, 0, s)),
                   pl.BlockSpec((None, None, SUBLANES, 2 * SLAB_STATE), lambda b, s: (b, s, 0, 0))],
        out_shape=[jax.ShapeDtypeStruct((bsz, n_rows, SSM_WIDTH), F32),
                   jax.ShapeDtypeStruct((bsz, N_SLABS, SUBLANES, 2 * SLAB_STATE), F32)],
        scratch_shapes=[pltpu.VMEM((n_rows, 2 * SLAB_STATE), F32),
                        pltpu.VMEM((2 * SUBLANES, 2 * SLAB_STATE), F32),
                        pltpu.VMEM((n_rows, LANES), F32), pltpu.VMEM((n_rows, LANES), F32)],
        compiler_params=_params(("parallel", "parallel")),
        name="s5_prompt",
    )(u, s5w["wb"], s5w["wc"], s5w["a"], s5w["d"])


def _s5s_kernel(u_ref, wb_ref, wc_ref, a_ref, d_ref, h0r_ref, h0i_ref,
                g_ref, hlr_ref, hli_ref, h_scr, u_scr, *, n_steps, bsz):
    half = SLAB_STATE
    for t in range(n_steps):
        u_scr[pl.ds(t * bsz, bsz), :] = u_ref[pl.ds(t, bsz, stride=n_steps), :]
    u = u_scr[...]
    h_scr[...] = _dot(u.astype(BF16), wb_ref[...])
    ar = a_ref[:, :half]
    ai = a_ref[:, half:]
    hr = h0r_ref[...]
    hi = h0i_ref[...]
    for t in range(n_steps):
        rows = pl.ds(t * bsz, bsz)
        hr, hi = _cmul_add(ar, ai, hr, hi, h_scr[rows, :half], h_scr[rows, half:])
        h_scr[rows, :half] = hr
        h_scr[rows, half:] = hi
    hlr_ref[...] = hr
    hli_ref[...] = hi
    y = _dot(h_scr[...].astype(BF16), wc_ref[...]) + d_ref[...] * u
    u_scr[...] = jax.nn.gelu(y)
    for t in range(n_steps):
        g_ref[pl.ds(t, bsz, stride=n_steps), :] = u_scr[pl.ds(t * bsz, bsz), :]


def _s5_sample(u, h0_re, h0_im, s5w, n_steps):
    n_rows = u.shape[0]
    bsz = n_rows // n_steps
    kern = functools.partial(_s5s_kernel, n_steps=n_steps, bsz=bsz)
    slab = lambda shape: pl.BlockSpec((None,) + shape, lambda s: (s, 0, 0))
    col = lambda r, w: pl.BlockSpec((r, w), lambda s: (0, s))
    return pl.pallas_call(
        kern,
        grid=(N_SLABS,),
        in_specs=[col(n_rows, LANES),
                  slab((LANES, 2 * SLAB_STATE)), slab((2 * SLAB_STATE, LANES)),
                  slab((1, 2 * SLAB_STATE)), slab((1, LANES)),
                  col(bsz, SLAB_STATE), col(bsz, SLAB_STATE)],
        out_specs=[col(n_rows, LANES), col(bsz, SLAB_STATE), col(bsz, SLAB_STATE)],
        out_shape=[jax.ShapeDtypeStruct((n_rows, SSM_WIDTH), F32),
                   jax.ShapeDtypeStruct((bsz, SSM_GROUPS * SSM_STATE), F32),
                   jax.ShapeDtypeStruct((bsz, SSM_GROUPS * SSM_STATE), F32)],
        scratch_shapes=[pltpu.VMEM((n_rows, 2 * SLAB_STATE), F32), pltpu.VMEM((n_rows, LANES), F32)],
        compiler_params=_params(("parallel",)),
        name="s5_sample",
    )(u, s5w["wb"], s5w["wc"], s5w["a"], s5w["d"], h0_re, h0_im)


def _pattn_kernel(ql_ref, qr_ref, kc_ref, kr_ref, vt_ref, wuv_ref, o_ref, m_scr, l_scr, acc_scr, *, tq, tk):
    i = pl.program_id(1)
    gw = ATTN_HEAD_GROUP * tq
    n_groups = N_HEADS // ATTN_HEAD_GROUP
    m_scr[...] = jnp.full(m_scr.shape, -jnp.inf, F32)
    l_scr[...] = jnp.zeros(l_scr.shape, F32)
    acc_scr[...] = jnp.zeros(acc_scr.shape, F32)

    def block(j, lo=0, n=tk, diagonal=False):
        kc = kc_ref[j, lo:lo + n, :]
        kr = kr_ref[j, lo:lo + n, :]
        vt = vt_ref[j, :, lo:lo + n]
        if diagonal:
            keep = (lax.broadcasted_iota(jnp.int32, (n, gw), 0)
                    <= (lax.broadcasted_iota(jnp.int32, (n, gw), 1) & (tq - 1)))

        def scores(g):
            cols = slice(g * gw, (g + 1) * gw)
            s = (_dot(kc, ql_ref[:, cols]) + _dot(kr, qr_ref[:, cols])) * SCALE
            return jnp.where(keep, s, -jnp.inf) if diagonal else s

        pending = [scores(g) for g in range(ATTN_LOOKAHEAD)]
        for g in range(n_groups):
            s = pending.pop(0)
            if g + ATTN_LOOKAHEAD < n_groups:
                pending.append(scores(g + ATTN_LOOKAHEAD))
            cols = slice(g * gw, (g + 1) * gw)
            m_prev = m_scr[:, cols]
            m_new = jnp.maximum(m_prev, jnp.max(s, axis=0, keepdims=True))
            corr = jnp.exp(m_prev - m_new)
            p = jnp.exp(s - m_new)
            l_scr[:, cols] = l_scr[:, cols] * corr + jnp.sum(p, axis=0, keepdims=True)
            acc_scr[:, cols] = acc_scr[:, cols] * corr + _dot(vt, p.astype(BF16))
            m_scr[:, cols] = m_new

    def body(j, carry):
        block(j)
        return carry

    j_last = (i * tq) // tk
    lax.fori_loop(0, j_last, body, 0)

    @pl.when(i % 2 == 0)
    def _():
        block(j_last, 0, tq, diagonal=True)

    @pl.when(i % 2 == 1)
    def _():
        block(j_last, 0, tq)
        block(j_last, tq, tq, diagonal=True)

    for h in range(N_HEADS):
        cols = slice(h * tq, (h + 1) * tq)
        y_t = _dot(wuv_ref[h], acc_scr[:, cols].astype(BF16)) / l_scr[:, cols]
        o_ref[:, h * V_DIM:(h + 1) * V_DIM] = y_t.T.astype(o_ref.dtype)


def _prompt_attention(ql_t, qr_t, kc, kr, v_t, wuv_t, bsz, seq, tq, tk):
    nq = seq // tq
    nk = seq // tk
    assert tk == 2 * tq and tq & (tq - 1) == 0 and N_HEADS % ATTN_HEAD_GROUP == 0
    kern = functools.partial(_pattn_kernel, tq=tq, tk=tk)
    per_batch = lambda r, c: pl.BlockSpec((None, nk, r, c), lambda b, i: (b, 0, 0, 0))
    return pl.pallas_call(
        kern,
        grid=(bsz, nq),
        in_specs=[pl.BlockSpec((None, KV_LORA, N_HEADS * tq), lambda b, i: (b * nq + i, 0, 0)),
                  pl.BlockSpec((None, LANES, N_HEADS * tq), lambda b, i: (b * nq + i, 0, 0)),
                  per_batch(tk, KV_LORA), per_batch(tk, LANES), per_batch(KV_LORA, tk),
                  _const_spec((N_HEADS, V_DIM, KV_LORA))],
        out_specs=pl.BlockSpec((tq, N_HEADS * V_DIM), lambda b, i: (b * nq + i, 0)),
        out_shape=jax.ShapeDtypeStruct((bsz * seq, N_HEADS * V_DIM), BF16),
        scratch_shapes=[pltpu.VMEM((1, N_HEADS * tq), F32), pltpu.VMEM((1, N_HEADS * tq), F32),
                        pltpu.VMEM((KV_LORA, N_HEADS * tq), F32)],
        compiler_params=_params(("parallel", "arbitrary")),
        name="prompt_attention",
    )(ql_t, qr_t, kc, kr, v_t, wuv_t)


N_CHAINS = 4


def _sattn_kernel(pt_ref, ql_ref, qr_ref, kn_ref, krn_ref, ck_hbm, krt_hbm, o_ref,
                  kbuf, rbuf, sems, m_scr, l_scr, acc_scr, kc_scr, *, n_new, n_pages, chunk):
    b = pl.program_id(0)
    n_rows = N_HEADS * n_new
    n_chunks = n_pages // chunk

    def page_copies(pid, slot, k):
        rows = pl.ds(k * PAGE_SIZE, PAGE_SIZE)
        return (pltpu.make_async_copy(ck_hbm.at[pid], kbuf.at[slot, rows, :], sems.at[0, slot]),
                pltpu.make_async_copy(krt_hbm.at[pid], rbuf.at[slot, k], sems.at[1, slot]))

    def start_chunk(row, j):
        for k in range(chunk):
            for cp in page_copies(pt_ref[row, j * chunk + k], j, k):
                cp.start()

    def wait_chunk(j):
        for k in range(chunk):
            for cp in page_copies(0, j, k):
                cp.wait()

    def start_ahead(j):
        ahead = j + SAMPLE_PREFETCH
        if ahead < n_chunks:
            start_chunk(b, ahead)
        else:
            @pl.when(b + 1 < pl.num_programs(0))
            def _():
                start_chunk(b + 1, ahead - n_chunks)

    @pl.when(b == 0)
    def _():
        for j in range(SAMPLE_PREFETCH):
            start_chunk(0, j)

    ql = ql_ref[...]
    qr = qr_ref[:, :ROPE_DIM]

    kc = kn_ref[...].astype(BF16)
    s = (_dot_nt(ql, kc) + _dot_nt(qr, krn_ref[...].astype(BF16))) * SCALE
    q_tok = lax.broadcasted_iota(jnp.int32, (n_rows, n_new), 0) & (n_new - 1)
    k_tok = lax.broadcasted_iota(jnp.int32, (n_rows, n_new), 1)
    s = jnp.where(k_tok <= q_tok, s, -jnp.inf)
    m = jnp.max(s, axis=-1, keepdims=True)
    p = jnp.exp(s - m)
    m_scr[0] = m
    l_scr[0] = jnp.sum(p, axis=-1, keepdims=True)
    acc_scr[0] = _dot(p.astype(BF16), kc)
    for c in range(1, N_CHAINS):
        m_scr[c] = jnp.full((n_rows, 1), -jnp.inf, F32)
        l_scr[c] = jnp.zeros((n_rows, 1), F32)
        acc_scr[c] = jnp.zeros((n_rows, KV_LORA), F32)

    per_chain = chunk // N_CHAINS
    chain_rows = per_chain * PAGE_SIZE

    def scores(t):
        j, c = divmod(t, N_CHAINS)
        if c == 0:
            start_ahead(j)
            wait_chunk(j)
        rows = slice(c * chain_rows, (c + 1) * chain_rows)
        kc_scr[rows, :] = kbuf[j, rows, :].astype(BF16)
        kc = kc_scr[rows, :]
        kr_t = jnp.concatenate([rbuf[j, k] for k in range(c * per_chain, (c + 1) * per_chain)],
                               axis=1).astype(BF16)
        return (_dot_nt(ql, kc) + _dot(qr, kr_t)) * SCALE, kc

    n_blocks = n_chunks * N_CHAINS
    pending = [scores(t) for t in range(SAMPLE_LOOKAHEAD)]
    for t in range(n_blocks):
        s, kc = pending.pop(0)
        if t + SAMPLE_LOOKAHEAD < n_blocks:
            pending.append(scores(t + SAMPLE_LOOKAHEAD))
        c = t % N_CHAINS
        m_prev = m_scr[c]
        m_new = jnp.maximum(m_prev, jnp.max(s, axis=-1, keepdims=True))
        corr = jnp.exp(m_prev - m_new)
        p = jnp.exp(s - m_new)
        l_scr[c] = l_scr[c] * corr + jnp.sum(p, axis=-1, keepdims=True)
        acc_scr[c] = acc_scr[c] * corr + _dot(p.astype(BF16), kc)
        m_scr[c] = m_new

    m = m_scr[0]
    for c in range(1, N_CHAINS):
        m = jnp.maximum(m, m_scr[c])
    l = jnp.zeros((n_rows, 1), F32)
    acc = jnp.zeros((n_rows, KV_LORA), F32)
    for c in range(N_CHAINS):
        w = jnp.exp(m_scr[c] - m)
        l = l + l_scr[c] * w
        acc = acc + acc_scr[c] * w
    o = acc / l
    for h in range(N_HEADS):
        o_ref[h] = o[h * n_new:(h + 1) * n_new]


def _value_up_kernel(o_ref, wuv_ref, y_ref):
    y_ref[...] = _dot(o_ref[...].astype(BF16), wuv_ref[...])


def _value_up(o_lat, wuv):
    _, t, _ = o_lat.shape
    return pl.pallas_call(
        _value_up_kernel,
        grid=(N_HEADS,),
        in_specs=[pl.BlockSpec((None, t, KV_LORA), lambda h: (h, 0, 0)),
                  pl.BlockSpec((None, KV_LORA, V_DIM), lambda h: (h, 0, 0))],
        out_specs=pl.BlockSpec((t, V_DIM), lambda h: (0, h)),
        out_shape=jax.ShapeDtypeStruct((t, N_HEADS * V_DIM), F32),
        compiler_params=_params(("parallel",)),
        name="value_up",
    )(o_lat, wuv)


def _sample_attention(page_table, ql, qr, kc_new, kr_new, cache_ckv, cache_kr_t, n_new, chunk):
    bsz, n_pages = page_table.shape
    n_rows = N_HEADS * n_new
    n_chunks = n_pages // chunk
    assert n_pages % chunk == 0 and chunk % N_CHAINS == 0 and 0 < SAMPLE_PREFETCH < n_chunks
    kern = functools.partial(_sattn_kernel, n_new=n_new, n_pages=n_pages, chunk=chunk)
    in_specs = [pl.BlockSpec((None, n_rows, KV_LORA), lambda b, pt: (b, 0, 0)),
                pl.BlockSpec((None, n_rows, LANES), lambda b, pt: (b, 0, 0)),
                pl.BlockSpec((n_new, KV_LORA), lambda b, pt: (b, 0)),
                pl.BlockSpec((n_new, ROPE_DIM), lambda b, pt: (b, 0)),
                pl.BlockSpec(memory_space=pl.ANY), pl.BlockSpec(memory_space=pl.ANY)]
    return pl.pallas_call(
        kern,
        grid_spec=pltpu.PrefetchScalarGridSpec(
            num_scalar_prefetch=1,
            grid=(bsz,),
            in_specs=in_specs,
            out_specs=pl.BlockSpec((N_HEADS, n_new, KV_LORA), lambda b, pt: (0, b, 0)),
            scratch_shapes=[pltpu.VMEM((n_chunks, chunk * PAGE_SIZE, KV_LORA), F32),
                            pltpu.VMEM((n_chunks, chunk, ROPE_DIM, PAGE_SIZE), F32),
                            pltpu.SemaphoreType.DMA((2, n_chunks)),
                            pltpu.VMEM((N_CHAINS, n_rows, 1), F32), pltpu.VMEM((N_CHAINS, n_rows, 1), F32),
                            pltpu.VMEM((N_CHAINS, n_rows, KV_LORA), F32),
                            pltpu.VMEM((chunk * PAGE_SIZE, KV_LORA), BF16)]),
        out_shape=jax.ShapeDtypeStruct((N_HEADS, bsz * n_new, KV_LORA), F32),
        compiler_params=_params(("arbitrary",)),
        name="sample_attention",
    )(page_table, ql, qr, kc_new, kr_new, cache_ckv, cache_kr_t)


def _mix_kernel(g_ref, ya_ref, x_ref, wglu_ref, bglu_ref, wout_ref, lg_ref, lb_ref, o_ref, ob_ref, *, alpha):
    part = g_ref.shape[0] // MIX_PARTS
    rows = [pl.ds(r * part, part) for r in range(MIX_PARTS)]

    def gate(r):
        g = g_ref[rows[r], :]
        return g, _dot(g.astype(BF16), wglu_ref[...])

    def glu(g, z):
        return (g * jax.nn.sigmoid(z + bglu_ref[...])).astype(BF16)

    def project(r, y):
        return (_dot(y, wout_ref[:SSM_WIDTH, :])
                + _dot(ya_ref[rows[r], :].astype(BF16), wout_ref[SSM_WIDTH:, :]))

    def finish(r, mix):
        x1 = _layer_norm(alpha * x_ref[rows[r], :] + mix, lg_ref[...], lb_ref[...])
        o_ref[rows[r], :] = x1
        ob_ref[rows[r], :] = x1.astype(BF16)

    gated = [gate(r) for r in range(MIX_PARTS)]
    mixes = []
    for r in range(MIX_PARTS):
        mixes.append(project(r, glu(*gated[r])))
        if r > 0:
            finish(r - 1, mixes[r - 1])
    finish(MIX_PARTS - 1, mixes[-1])


def _mix(g, y_att, x, wts, alpha, tm):
    t = x.shape[0]
    row = lambda w: pl.BlockSpec((tm, w), lambda i: (i, 0))
    return pl.pallas_call(
        functools.partial(_mix_kernel, alpha=alpha),
        grid=(t // tm,),
        in_specs=[row(SSM_WIDTH), row(N_HEADS * V_DIM), row(D_MODEL),
                  _const_spec((SSM_WIDTH, SSM_WIDTH)), _const_spec((1, SSM_WIDTH)),
                  _const_spec((D_MODEL, D_MODEL)), _const_spec((1, D_MODEL)), _const_spec((1, D_MODEL))],
        out_specs=[row(D_MODEL), row(D_MODEL)],
        out_shape=[jax.ShapeDtypeStruct((t, D_MODEL), F32), jax.ShapeDtypeStruct((t, D_MODEL), BF16)],
        compiler_params=_params(("parallel",)),
        name="mix",
    )(g, y_att, x, wts["w_glu"], wts["b_glu"], wts["w_out"], wts["ln1_g"], wts["ln1_b"])


def _ffn_kernel(x_ref, xb_ref, wg_ref, wu_ref, wd_ref, lg_ref, lb_ref, o_ref, *, alpha):
    f = pl.program_id(1)

    @pl.when(f == 0)
    def _():
        o_ref[...] = jnp.zeros(o_ref.shape, F32)

    xb = xb_ref[...]
    gate = _dot(xb, wg_ref[...])
    up = _dot(xb, wu_ref[...])
    o_ref[...] += _dot((jax.nn.silu(gate) * up).astype(BF16), wd_ref[...])

    @pl.when(f == pl.num_programs(1) - 1)
    def _():
        o_ref[...] = _layer_norm(alpha * x_ref[...] + o_ref[...], lg_ref[...], lb_ref[...])


def _ffn(x, xb, wts, alpha, tm, tf):
    t = x.shape[0]
    d_ff = wts["w_gate"].shape[1]
    return pl.pallas_call(
        functools.partial(_ffn_kernel, alpha=alpha),
        grid=(t // tm, d_ff // tf),
        in_specs=[pl.BlockSpec((tm, D_MODEL), lambda i, f: (i, 0)),
                  pl.BlockSpec((tm, D_MODEL), lambda i, f: (i, 0)),
                  pl.BlockSpec((D_MODEL, tf), lambda i, f: (0, f)),
                  pl.BlockSpec((D_MODEL, tf), lambda i, f: (0, f)),
                  pl.BlockSpec((tf, D_MODEL), lambda i, f: (f, 0)),
                  pl.BlockSpec((1, D_MODEL), lambda i, f: (0, 0)),
                  pl.BlockSpec((1, D_MODEL), lambda i, f: (0, 0))],
        out_specs=pl.BlockSpec((tm, D_MODEL), lambda i, f: (i, 0)),
        out_shape=jax.ShapeDtypeStruct((t, D_MODEL), F32),
        compiler_params=_params(("parallel", "arbitrary")),
        name="ffn",
    )(x, xb, wts["w_gate"], wts["w_up"], wts["w_down"], wts["ln2_g"], wts["ln2_b"])


def _swap_halves(w):
    half = w.shape[-1] // 2
    return jnp.concatenate([w[..., half:], w[..., :half]], axis=-1)


def _pad_lanes(w):
    return jnp.concatenate([w, jnp.zeros(w.shape[:-1] + (LANES - w.shape[-1],), w.dtype)], axis=-1)


def _layer_weights(w_in, g_q, w_uq, w_uk, g_kv, w_uv, w_glu, b_glu, w_out, ln1_g, ln1_b,
                   w_gate, w_up, w_down, ln2_g, ln2_b):
    n_main = SSM_WIDTH + Q_LORA + KV_LORA
    w_kr = w_in[:, n_main:]
    uq = w_uq.reshape(Q_LORA, N_HEADS, NOPE_DIM + ROPE_DIM)
    uq_rope = uq[:, :, NOPE_DIM:]
    w_qn = uq[:, :, :NOPE_DIM].reshape(Q_LORA, N_HEADS * NOPE_DIM).astype(BF16)
    w_qr = _pad_lanes(uq_rope).reshape(Q_LORA, N_HEADS * LANES).astype(BF16)
    w_qs = _pad_lanes(_swap_halves(uq_rope)).reshape(Q_LORA, N_HEADS * LANES).astype(BF16)
    return {
        "w_main": w_in[:, :n_main].astype(BF16),
        "w_kr": jnp.concatenate([_pad_lanes(w_kr), _pad_lanes(_swap_halves(w_kr))], axis=1).astype(BF16),
        "g_q": g_q.reshape(1, Q_LORA), "g_kv": g_kv.reshape(1, KV_LORA),
        "w_qn": w_qn, "w_qr": w_qr, "w_qs": w_qs,
        "w_qnT": w_qn.T,
        "w_qrT": uq_rope.reshape(Q_LORA, N_HEADS * ROPE_DIM).T.astype(BF16),
        "w_qsT": _swap_halves(uq_rope).reshape(Q_LORA, N_HEADS * ROPE_DIM).T.astype(BF16),
        "w_ukT": jnp.transpose(w_uk, (1, 2, 0)).astype(BF16),
        "w_uk": jnp.transpose(w_uk, (1, 0, 2)).astype(BF16),
        "w_uv": jnp.transpose(w_uv, (1, 0, 2)).astype(BF16),
        "w_uvT": jnp.transpose(w_uv, (1, 2, 0)).astype(BF16),
        "w_glu": w_glu.astype(BF16), "b_glu": b_glu.reshape(1, SSM_WIDTH),
        "w_out": w_out.astype(BF16),
        "ln1_g": ln1_g.reshape(1, D_MODEL), "ln1_b": ln1_b.reshape(1, D_MODEL),
        "w_gate": w_gate.astype(BF16), "w_up": w_up.astype(BF16), "w_down": w_down.astype(BF16),
        "ln2_g": ln2_g.reshape(1, D_MODEL), "ln2_b": ln2_b.reshape(1, D_MODEL),
    }


def _s5_weights(a_re, a_im, log_step, b_re, b_im, c_re, c_im, d):
    rep = lambda a: jnp.repeat(a, SSM_CH, axis=0)
    bt = lambda b: jnp.transpose(b, (0, 2, 1)).reshape(SSM_GROUPS * SSM_CH, SSM_STATE)
    abr, abi, bbr, bbi = _s5prep(rep(a_re), rep(a_im), rep(log_step.reshape(SSM_GROUPS, 1)),
                                 bt(b_re), bt(b_im))
    abr = abr[::SSM_CH]
    abi = abi[::SSM_CH]
    eye = jnp.eye(SLAB_GROUPS, dtype=F32)

    def blockdiag_in(b):
        b4 = b.reshape(N_SLABS, SLAB_GROUPS, SSM_CH, SSM_STATE)
        return jnp.einsum("sghp,gk->sghkp", b4, eye).reshape(N_SLABS, LANES, SLAB_STATE)

    def blockdiag_out(c):
        c4 = c.reshape(N_SLABS, SLAB_GROUPS, SSM_CH, SSM_STATE)
        return jnp.einsum("sghp,gk->skpgh", c4, eye).reshape(N_SLABS, SLAB_STATE, LANES)

    slab_row = lambda a: a.reshape(N_SLABS, 1, SLAB_STATE)
    return {
        "wb": jnp.concatenate([blockdiag_in(bbr), blockdiag_in(bbi)], axis=2).astype(BF16),
        "wc": jnp.concatenate([blockdiag_out(c_re), -blockdiag_out(c_im)], axis=1).astype(BF16),
        "a": jnp.concatenate([slab_row(abr), slab_row(abi)], axis=2),
        "d": d.reshape(N_SLABS, 1, LANES),
    }


def _rope_rows():
    half = ROPE_DIM // 2
    inv = ROPE_BASE ** (-2.0 * jnp.arange(half, dtype=F32) / ROPE_DIM)
    sign = jnp.concatenate([-jnp.ones((half,), F32), jnp.ones((half,), F32)])
    return (_pad_lanes(jnp.concatenate([inv, inv]).reshape(1, ROPE_DIM)),
            _pad_lanes(sign.reshape(1, ROPE_DIM)))


def kernel(x_prompt, x_sample, cache_ckv, cache_krope, state_ssm_re, state_ssm_im, page_table, w_in, g_q, w_uq, w_uk, g_kv, w_uv, ssm_a_re, ssm_a_im, ssm_log_step, ssm_b_re, ssm_b_im, ssm_c_re, ssm_c_im, ssm_d, w_glu, b_glu, w_out, ln1_g, ln1_b, w_gate, w_up, w_down, ln2_g, ln2_b):
    depth = w_in.shape[0]
    alpha = (2 * depth) ** 0.25
    bsz, seq, _ = x_prompt.shape
    dbsz, dseq, _ = x_sample.shape
    past = page_table.shape[1] * PAGE_SIZE
    rope_rows = _rope_rows()
    cos_p, sin_p = _rope_table(rope_rows, seq, 0, seq)
    cos_s, sin_s = _rope_table(rope_rows, ROW_TM, past, dseq)

    y_p = x_prompt.reshape(bsz * seq, D_MODEL)
    y_s = x_sample.reshape(dbsz * dseq, D_MODEL)
    outs = [[] for _ in range(8)]
    for l in range(depth):
        wts = _layer_weights(w_in[l], g_q[l], w_uq[l], w_uk[l], g_kv[l], w_uv[l], w_glu[l], b_glu[l],
                             w_out[l], ln1_g[l], ln1_b[l], w_gate[l], w_up[l], w_down[l], ln2_g[l], ln2_b[l])
        s5w = _s5_weights(ssm_a_re[l], ssm_a_im[l], ssm_log_step[l], ssm_b_re[l], ssm_b_im[l],
                          ssm_c_re[l], ssm_c_im[l], ssm_d[l])

        u, ckv, kr, ckvb, krb, ql_t, qr_t, v_t = _inproj(y_p, cos_p, sin_p, wts, tm=ATTN_TQ, prompt_layout=True)
        g, hl = _s5_prompt(u.reshape(bsz, seq, SSM_WIDTH), s5w)
        nk = seq // ATTN_TK
        y_att = _prompt_attention(ql_t, qr_t, ckvb.reshape(bsz, nk, ATTN_TK, KV_LORA),
                                  krb.reshape(bsz, nk, ATTN_TK, LANES), v_t.reshape(bsz, nk, KV_LORA, ATTN_TK),
                                  wts["w_uvT"], bsz, seq, tq=ATTN_TQ, tk=ATTN_TK)
        x1, x1b = _mix(g.reshape(bsz * seq, SSM_WIDTH), y_att, y_p, wts, alpha, tm=MIX_TM)
        y_p = _ffn(x1, x1b, wts, alpha, tm=FFN_TM, tf=FFN_TF)
        h_last = hl[:, :, SUBLANES - 1, :]
        outs[0].append(ckv.reshape(bsz, seq, KV_LORA))
        outs[1].append(kr.reshape(bsz, seq, ROPE_DIM))
        outs[2].append(h_last[:, :, :SLAB_STATE].reshape(bsz, SSM_GROUPS, SSM_STATE))
        outs[3].append(h_last[:, :, SLAB_STATE:].reshape(bsz, SSM_GROUPS, SSM_STATE))

        u, ckv, kr, _, _, ql, qr = _inproj(y_s, cos_s, sin_s, wts, tm=ROW_TM)
        g, hl_re, hl_im = _s5_sample(
            u, state_ssm_re[l].reshape(dbsz, SSM_GROUPS * SSM_STATE),
            state_ssm_im[l].reshape(dbsz, SSM_GROUPS * SSM_STATE), s5w, dseq)
        per_batch = lambda q: q.reshape(N_HEADS, dbsz, dseq, q.shape[-1]).transpose(1, 0, 2, 3).reshape(
            dbsz, N_HEADS * dseq, q.shape[-1])
        o_lat = _sample_attention(page_table, per_batch(ql), per_batch(qr), ckv, kr,
                                  cache_ckv[l], jnp.swapaxes(cache_krope[l], 1, 2), dseq,
                                  chunk=SAMPLE_PAGES_PER_STEP)
        y_att = _value_up(o_lat, wts["w_uv"])
        x1, x1b = _mix(g, y_att, y_s, wts, alpha, tm=MIX_TM)
        y_s = _ffn(x1, x1b, wts, alpha, tm=FFN_TM, tf=FFN_TF)
        outs[4].append(ckv.reshape(dbsz, dseq, KV_LORA))
        outs[5].append(kr.reshape(dbsz, dseq, ROPE_DIM))
        outs[6].append(hl_re.reshape(dbsz, SSM_GROUPS, SSM_STATE))
        outs[7].append(hl_im.reshape(dbsz, SSM_GROUPS, SSM_STATE))

    return (y_p.reshape(bsz, seq, D_MODEL), y_s.reshape(dbsz, dseq, D_MODEL),
            *[jnp.stack(o) for o in outs])
```

```python
import functools
import math

import jax
import jax.numpy as jnp
from jax import lax
from jax.experimental import pallas as pl
from jax.experimental.pallas import tpu as pltpu

F32 = jnp.float32
BF16 = jnp.bfloat16

D_MODEL = 2048
SSM_WIDTH = 1024
SSM_CH = 16
SSM_GROUPS = 64
SSM_STATE = 64
N_HEADS = 8
NOPE_DIM = 128
ROPE_DIM = 64
V_DIM = 128
Q_LORA = 512
KV_LORA = 512
ROPE_BASE = 10000.0
SCALE = (NOPE_DIM + ROPE_DIM) ** -0.5
PAGE_SIZE = 128
LANES = 128
SUBLANES = 8
SLAB_GROUPS = LANES // SSM_CH
N_SLABS = SSM_GROUPS // SLAB_GROUPS
SLAB_STATE = SLAB_GROUPS * SSM_STATE
VMEM_LIMIT = 56 * 1024 * 1024
ATTN_TQ = 256
ATTN_TK = 512
ATTN_HEAD_GROUP = 2
ATTN_LOOKAHEAD = 2
SAMPLE_PAGES_PER_STEP = 16
SAMPLE_LOOKAHEAD = 3
SAMPLE_PREFETCH = 3
ROW_TM = 256
MIX_TM = 512
MIX_PARTS = 2
FFN_TM = 512
FFN_TF = 512


def _dot(a, b):
    return jnp.dot(a, b, preferred_element_type=F32)


def _dot_nt(a, b):
    return lax.dot_general(a, b, (((1,), (1,)), ((), ())), preferred_element_type=F32)


def _const_spec(shape):
    n = len(shape)
    return pl.BlockSpec(shape, lambda *_: (0,) * n, pipeline_mode=pl.Buffered(1))


def _params(sem):
    return pltpu.CompilerParams(dimension_semantics=sem, vmem_limit_bytes=VMEM_LIMIT)


def _rms(x, g, eps=1e-6):
    return x * lax.rsqrt(jnp.mean(x * x, axis=-1, keepdims=True) + eps) * g


def _layer_norm(x, g, b, eps=1e-5):
    mu = jnp.mean(x, axis=-1, keepdims=True)
    xc = x - mu
    var = jnp.mean(xc * xc, axis=-1, keepdims=True)
    return xc * lax.rsqrt(var + eps) * g + b


def _rope_table_kernel(inv_ref, sgn_ref, cos_ref, sin_ref, *, pos_offset, pos_period):
    row = lax.broadcasted_iota(jnp.int32, cos_ref.shape, 0)
    ang = (pos_offset + (row & (pos_period - 1))).astype(F32) * inv_ref[...]
    cos_ref[...] = jnp.cos(ang)
    sin_ref[...] = jnp.sin(ang) * sgn_ref[...]


def _rope_table(rope_rows, n_rows, pos_offset, pos_period):
    assert pos_period & (pos_period - 1) == 0
    shape = jax.ShapeDtypeStruct((n_rows, LANES), F32)
    return pl.pallas_call(
        functools.partial(_rope_table_kernel, pos_offset=pos_offset, pos_period=pos_period),
        out_shape=[shape, shape], name="rope_table")(*rope_rows)


def _inproj_kernel(x_ref, wm_ref, wkr_ref, gq_ref, gkv_ref, cos_ref, sin_ref,
                   wqn_ref, wqr_ref, wqs_ref, wuk_ref,
                   u_ref, ckv_ref, kr_ref, ckvb_ref, krb_ref, ql_ref, qr_ref, *maybe_vt_ref, q_transposed):
    xb = x_ref[...].astype(BF16)
    cos = cos_ref[...]
    sin = sin_ref[...]
    proj = _dot(xb, wm_ref[...])
    u_ref[...] = proj[:, :SSM_WIDTH]
    cq = proj[:, SSM_WIDTH:SSM_WIDTH + Q_LORA]
    ckv = _rms(proj[:, SSM_WIDTH + Q_LORA:], gkv_ref[...])
    ckv_ref[...] = ckv
    ckvb_ref[...] = ckv.astype(BF16)
    kr2 = _dot(xb, wkr_ref[...])
    krope = kr2[:, :LANES] * cos + kr2[:, LANES:] * sin
    kr_ref[...] = krope[:, :ROPE_DIM]
    krb_ref[...] = krope.astype(BF16)
    cqn = _rms(cq, gq_ref[...])
    if q_transposed:
        maybe_vt_ref[0][...] = ckv.T.astype(BF16)
        cqt = cqn.T.astype(BF16)
        qn = _dot(wqn_ref[...], cqt)
        qa = _dot(wqr_ref[...], cqt)
        qs = _dot(wqs_ref[...], cqt)
        cos_t = cos.T[:ROPE_DIM]
        sin_t = sin.T[:ROPE_DIM]
        tm = cos.shape[0]
        for h in range(N_HEADS):
            sl = slice(h * LANES, (h + 1) * LANES)
            rl = slice(h * ROPE_DIM, (h + 1) * ROPE_DIM)
            cols = slice(h * tm, (h + 1) * tm)
            ql_ref[:, cols] = _dot(wuk_ref[h], qn[sl].astype(BF16)).astype(BF16)
            qr_ref[:ROPE_DIM, cols] = (qa[rl] * cos_t + qs[rl] * sin_t).astype(BF16)
            qr_ref[ROPE_DIM:, cols] = jnp.zeros((LANES - ROPE_DIM, tm), BF16)
    else:
        cqb = cqn.astype(BF16)
        qn = _dot(cqb, wqn_ref[...])
        qa = _dot(cqb, wqr_ref[...])
        qs = _dot(cqb, wqs_ref[...])
        for h in range(N_HEADS):
            sl = slice(h * LANES, (h + 1) * LANES)
            ql_ref[h] = _dot(qn[:, sl].astype(BF16), wuk_ref[h]).astype(BF16)
            qr_ref[h] = (qa[:, sl] * cos + qs[:, sl] * sin).astype(BF16)


def _inproj(x, cos, sin, wts, tm, prompt_layout=False):
    t = x.shape[0]
    table_tiles = cos.shape[0] // tm
    table = pl.BlockSpec((tm, LANES), lambda i: (i % table_tiles, 0))
    row = lambda w: pl.BlockSpec((tm, w), lambda i: (i, 0))
    extra_specs, extra_shapes = [], []
    if prompt_layout:
        assert tm == ATTN_TQ and ATTN_TK % tm == 0
        qspec = lambda w: pl.BlockSpec((None, w, N_HEADS * tm), lambda i: (i, 0, 0))
        qshape = lambda w: jax.ShapeDtypeStruct((t // tm, w, N_HEADS * tm), BF16)
        wq = [wts["w_qnT"], wts["w_qrT"], wts["w_qsT"], wts["w_uk"]]
        per_tk = ATTN_TK // tm
        extra_specs = [pl.BlockSpec((None, KV_LORA, tm), lambda i: (i // per_tk, 0, i % per_tk))]
        extra_shapes = [jax.ShapeDtypeStruct((t // ATTN_TK, KV_LORA, ATTN_TK), BF16)]
    else:
        qspec = lambda w: pl.BlockSpec((N_HEADS, tm, w), lambda i: (0, i, 0))
        qshape = lambda w: jax.ShapeDtypeStruct((N_HEADS, t, w), BF16)
        wq = [wts["w_qn"], wts["w_qr"], wts["w_qs"], wts["w_ukT"]]
    return pl.pallas_call(
        functools.partial(_inproj_kernel, q_transposed=prompt_layout),
        grid=(t // tm,),
        in_specs=[row(D_MODEL),
                  _const_spec((D_MODEL, 2048)), _const_spec((D_MODEL, 2 * LANES)),
                  _const_spec((1, Q_LORA)), _const_spec((1, KV_LORA)),
                  table, table] + [_const_spec(w.shape) for w in wq],
        out_specs=[row(SSM_WIDTH), row(KV_LORA), row(ROPE_DIM), row(KV_LORA), row(LANES),
                   qspec(KV_LORA), qspec(LANES)] + extra_specs,
        out_shape=[jax.ShapeDtypeStruct((t, SSM_WIDTH), F32),
                   jax.ShapeDtypeStruct((t, KV_LORA), F32),
                   jax.ShapeDtypeStruct((t, ROPE_DIM), F32),
                   jax.ShapeDtypeStruct((t, KV_LORA), BF16),
                   jax.ShapeDtypeStruct((t, LANES), BF16),
                   qshape(KV_LORA), qshape(LANES)] + extra_shapes,
        compiler_params=_params(("parallel",)),
        name="inproj",
    )(x, wts["w_main"], wts["w_kr"], wts["g_q"], wts["g_kv"], cos, sin, *wq)


def _s5prep_kernel(are_ref, aim_ref, ls_ref, bre_ref, bim_ref,
                   abr_ref, abi_ref, bbr_ref, bbi_ref):
    lr = are_ref[...]
    li = aim_ref[...]
    delta = jnp.exp(ls_ref[...])
    mag = jnp.exp(lr * delta)
    ar = mag * jnp.cos(li * delta)
    ai = mag * jnp.sin(li * delta)
    abr_ref[...] = ar
    abi_ref[...] = ai
    den = lr * lr + li * li
    fr = ((ar - 1.0) * lr + ai * li) / den
    fi = (ai * lr - (ar - 1.0) * li) / den
    br = bre_ref[...]
    bi = bim_ref[...]
    bbr_ref[...] = fr * br - fi * bi
    bbi_ref[...] = fr * bi + fi * br


def _s5prep(a_re, a_im, log_step, bt_re, bt_im):
    shp = jax.ShapeDtypeStruct((SSM_GROUPS * SSM_CH, SSM_STATE), F32)
    return pl.pallas_call(_s5prep_kernel, out_shape=[shp] * 4, name="s5prep")(
        a_re, a_im, log_step, bt_re, bt_im)


def _cmul_add(ar, ai, hr, hi, br, bi):
    return ar * hr - ai * hi + br, ar * hi + ai * hr + bi


def _s5p_kernel(u_ref, wb_ref, wc_ref, a_ref, d_ref, g_ref, hl_ref, bu_scr, e_scr, u_scr, g_scr,
                *, seg_len, chunk):
    n_rows = SUBLANES * seg_len
    n_chunks = n_rows // chunk
    steps = chunk // SUBLANES
    half = SLAB_STATE
    wb = wb_ref[...]
    wc = wc_ref[...]
    d = d_ref[...]
    ar = jnp.broadcast_to(a_ref[:, :half], (SUBLANES, half))
    ai = jnp.broadcast_to(a_ref[:, half:], (SUBLANES, half))

    for seg in range(SUBLANES):
        u_scr[pl.ds(seg, seg_len, stride=SUBLANES), :] = u_ref[pl.ds(seg * seg_len, seg_len), :]

    def project_in(c):
        rows = pl.ds(c * chunk, chunk)
        bu_scr[rows, :] = _dot(u_scr[rows, :].astype(BF16), wb)

    def project_out(c):
        rows = pl.ds(c * chunk, chunk)
        y = _dot(bu_scr[rows, :].astype(BF16), wc) + d * u_scr[rows, :]
        g_scr[rows, :] = jax.nn.gelu(y)
        for seg in range(SUBLANES):
            g_ref[pl.ds(seg * seg_len + c * steps, steps), :] = (
                g_scr[pl.ds(c * chunk + seg, steps, stride=SUBLANES), :])

    def scan_chunk(c, carry, store):
        hr, hi = carry
        for t in range(c * steps, (c + 1) * steps):
            rows = pl.ds(t * SUBLANES, SUBLANES)
            hr, hi = _cmul_add(ar, ai, hr, hi, bu_scr[rows, :half], bu_scr[rows, half:])
            if store:
                bu_scr[rows, :half] = hr
                bu_scr[rows, half:] = hi
        return hr, hi

    zero = jnp.zeros((SUBLANES, half), F32)
    carry = (zero, zero)
    project_in(0)
    for c in range(n_chunks):
        if c + 1 < n_chunks:
            project_in(c + 1)
        carry = scan_chunk(c, carry, store=False)
    er, ei = carry
    pr, pi = ar, ai
    for _ in range(int(math.log2(seg_len))):
        pr, pi = pr * pr - pi * pi, 2.0 * pr * pi
    e_scr[0:SUBLANES, :half] = er
    e_scr[0:SUBLANES, half:] = ei
    cr = jnp.zeros((1, half), F32)
    ci = jnp.zeros((1, half), F32)
    e_scr[SUBLANES:SUBLANES + 1, :half] = cr
    e_scr[SUBLANES:SUBLANES + 1, half:] = ci
    for s in range(SUBLANES - 1):
        cr, ci = _cmul_add(pr[0:1], pi[0:1], cr, ci, e_scr[s:s + 1, :half], e_scr[s:s + 1, half:])
        e_scr[SUBLANES + s + 1:SUBLANES + s + 2, :half] = cr
        e_scr[SUBLANES + s + 1:SUBLANES + s + 2, half:] = ci
    carry = (e_scr[SUBLANES:2 * SUBLANES, :half], e_scr[SUBLANES:2 * SUBLANES, half:])
    for c in range(n_chunks):
        if c > 0:
            project_out(c - 1)
        carry = scan_chunk(c, carry, store=True)
    project_out(n_chunks - 1)
    hr, hi = carry
    hl_ref[:, :half] = hr
    hl_ref[:, half:] = hi


def _s5_prompt(u, s5w):
    bsz, n_rows, _ = u.shape
    seg_len = n_rows // SUBLANES
    kern = functools.partial(_s5p_kernel, seg_len=seg_len, chunk=512)
    slab = lambda shape: pl.BlockSpec((None,) + shape, lambda b, s: (s, 0, 0))
    return pl.pallas_call(
        kern,
        grid=(bsz, N_SLABS),
        in_specs=[pl.BlockSpec((None, n_rows, LANES), lambda b, s: (b, 0, s)),
                  slab((LANES, 2 * SLAB_STATE)), slab((2 * SLAB_STATE, LANES)),
                  slab((1, 2 * SLAB_STATE)), slab((1, LANES))],
        out_specs=[pl.BlockSpec((None, n_rows, LANES), lambda b, s: (b, 0, s)),
                   pl.BlockSpec((None, None, SUBLANES, 2 * SLAB_STATE), lambda b, s: (b, s, 0, 0))],
        out_shape=[jax.ShapeDtypeStruct((bsz, n_rows, SSM_WIDTH), F32),
                   jax.ShapeDtypeStruct((bsz, N_SLABS, SUBLANES, 2 * SLAB_STATE), F32)],
        scratch_shapes=[pltpu.VMEM((n_rows, 2 * SLAB_STATE), F32),
                        pltpu.VMEM((2 * SUBLANES, 2 * SLAB_STATE), F32),
                        pltpu.VMEM((n_rows, LANES), F32), pltpu.VMEM((n_rows, LANES), F32)],
        compiler_params=_params(("parallel", "parallel")),
        name="s5_prompt",
    )(u, s5w["wb"], s5w["wc"], s5w["a"], s5w["d"])


def _s5s_kernel(u_ref, wb_ref, wc_ref, a_ref, d_ref, h0r_ref, h0i_ref,
                g_ref, hlr_ref, hli_ref, h_scr, u_scr, *, n_steps, bsz):
    half = SLAB_STATE
    for t in range(n_steps):
        u_scr[pl.ds(t * bsz, bsz), :] = u_ref[pl.ds(t, bsz, stride=n_steps), :]
    u = u_scr[...]
    h_scr[...] = _dot(u.astype(BF16), wb_ref[...])
    ar = a_ref[:, :half]
    ai = a_ref[:, half:]
    hr = h0r_ref[...]
    hi = h0i_ref[...]
    for t in range(n_steps):
        rows = pl.ds(t * bsz, bsz)
        hr, hi = _cmul_add(ar, ai, hr, hi, h_scr[rows, :half], h_scr[rows, half:])
        h_scr[rows, :half] = hr
        h_scr[rows, half:] = hi
    hlr_ref[...] = hr
    hli_ref[...] = hi
    y = _dot(h_scr[...].astype(BF16), wc_ref[...]) + d_ref[...] * u
    u_scr[...] = jax.nn.gelu(y)
    for t in range(n_steps):
        g_ref[pl.ds(t, bsz, stride=n_steps), :] = u_scr[pl.ds(t * bsz, bsz), :]


def _s5_sample(u, h0_re, h0_im, s5w, n_steps):
    n_rows = u.shape[0]
    bsz = n_rows // n_steps
    kern = functools.partial(_s5s_kernel, n_steps=n_steps, bsz=bsz)
    slab = lambda shape: pl.BlockSpec((None,) + shape, lambda s: (s, 0, 0))
    col = lambda r, w: pl.BlockSpec((r, w), lambda s: (0, s))
    return pl.pallas_call(
        kern,
        grid=(N_SLABS,),
        in_specs=[col(n_rows, LANES),
                  slab((LANES, 2 * SLAB_STATE)), slab((2 * SLAB_STATE, LANES)),
                  slab((1, 2 * SLAB_STATE)), slab((1, LANES)),
                  col(bsz, SLAB_STATE), col(bsz, SLAB_STATE)],
        out_specs=[col(n_rows, LANES), col(bsz, SLAB_STATE), col(bsz, SLAB_STATE)],
        out_shape=[jax.ShapeDtypeStruct((n_rows, SSM_WIDTH), F32),
                   jax.ShapeDtypeStruct((bsz, SSM_GROUPS * SSM_STATE), F32),
                   jax.ShapeDtypeStruct((bsz, SSM_GROUPS * SSM_STATE), F32)],
        scratch_shapes=[pltpu.VMEM((n_rows, 2 * SLAB_STATE), F32), pltpu.VMEM((n_rows, LANES), F32)],
        compiler_params=_params(("parallel",)),
        name="s5_sample",
    )(u, s5w["wb"], s5w["wc"], s5w["a"], s5w["d"], h0_re, h0_im)


def _pattn_kernel(ql_ref, qr_ref, kc_ref, kr_ref, vt_ref, wuv_ref, o_ref, m_scr, l_scr, acc_scr, *, tq, tk):
    i = pl.program_id(1)
    gw = ATTN_HEAD_GROUP * tq
    n_groups = N_HEADS // ATTN_HEAD_GROUP
    m_scr[...] = jnp.full(m_scr.shape, -jnp.inf, F32)
    l_scr[...] = jnp.zeros(l_scr.shape, F32)
    acc_scr[...] = jnp.zeros(acc_scr.shape, F32)

    def block(j, lo=0, n=tk, diagonal=False):
        kc = kc_ref[j, lo:lo + n, :]
        kr = kr_ref[j, lo:lo + n, :]
        vt = vt_ref[j, :, lo:lo + n]
        if diagonal:
            keep = (lax.broadcasted_iota(jnp.int32, (n, gw), 0)
                    <= (lax.broadcasted_iota(jnp.int32, (n, gw), 1) & (tq - 1)))

        def scores(g):
            cols = slice(g * gw, (g + 1) * gw)
            s = (_dot(kc, ql_ref[:, cols]) + _dot(kr, qr_ref[:, cols])) * SCALE
            return jnp.where(keep, s, -jnp.inf) if diagonal else s

        pending = [scores(g) for g in range(ATTN_LOOKAHEAD)]
        for g in range(n_groups):
            s = pending.pop(0)
            if g + ATTN_LOOKAHEAD < n_groups:
                pending.append(scores(g + ATTN_LOOKAHEAD))
            cols = slice(g * gw, (g + 1) * gw)
            m_prev = m_scr[:, cols]
            m_new = jnp.maximum(m_prev, jnp.max(s, axis=0, keepdims=True))
            corr = jnp.exp(m_prev - m_new)
            p = jnp.exp(s - m_new)
            l_scr[:, cols] = l_scr[:, cols] * corr + jnp.sum(p, axis=0, keepdims=True)
            acc_scr[:, cols] = acc_scr[:, cols] * corr + _dot(vt, p.astype(BF16))
            m_scr[:, cols] = m_new

    def body(j, carry):
        block(j)
        return carry

    j_last = (i * tq) // tk
    lax.fori_loop(0, j_last, body, 0)

    @pl.when(i % 2 == 0)
    def _():
        block(j_last, 0, tq, diagonal=True)

    @pl.when(i % 2 == 1)
    def _():
        block(j_last, 0, tq)
        block(j_last, tq, tq, diagonal=True)

    for h in range(N_HEADS):
        cols = slice(h * tq, (h + 1) * tq)
        y_t = _dot(wuv_ref[h], acc_scr[:, cols].astype(BF16)) / l_scr[:, cols]
        o_ref[:, h * V_DIM:(h + 1) * V_DIM] = y_t.T.astype(o_ref.dtype)


def _prompt_attention(ql_t, qr_t, kc, kr, v_t, wuv_t, bsz, seq, tq, tk):
    nq = seq // tq
    nk = seq // tk
    assert tk == 2 * tq and tq & (tq - 1) == 0 and N_HEADS % ATTN_HEAD_GROUP == 0
    kern = functools.partial(_pattn_kernel, tq=tq, tk=tk)
    per_batch = lambda r, c: pl.BlockSpec((None, nk, r, c), lambda b, i: (b, 0, 0, 0))
    return pl.pallas_call(
        kern,
        grid=(bsz, nq),
        in_specs=[pl.BlockSpec((None, KV_LORA, N_HEADS * tq), lambda b, i: (b * nq + i, 0, 0)),
                  pl.BlockSpec((None, LANES, N_HEADS * tq), lambda b, i: (b * nq + i, 0, 0)),
                  per_batch(tk, KV_LORA), per_batch(tk, LANES), per_batch(KV_LORA, tk),
                  _const_spec((N_HEADS, V_DIM, KV_LORA))],
        out_specs=pl.BlockSpec((tq, N_HEADS * V_DIM), lambda b, i: (b * nq + i, 0)),
        out_shape=jax.ShapeDtypeStruct((bsz * seq, N_HEADS * V_DIM), BF16),
        scratch_shapes=[pltpu.VMEM((1, N_HEADS * tq), F32), pltpu.VMEM((1, N_HEADS * tq), F32),
                        pltpu.VMEM((KV_LORA, N_HEADS * tq), F32)],
        compiler_params=_params(("parallel", "arbitrary")),
        name="prompt_attention",
    )(ql_t, qr_t, kc, kr, v_t, wuv_t)


N_CHAINS = 4


def _sattn_kernel(pt_ref, ql_ref, qr_ref, kn_ref, krn_ref, ck_hbm, krt_hbm, o_ref,
                  kbuf, rbuf, sems, m_scr, l_scr, acc_scr, kc_scr, *, n_new, n_pages, chunk):
    b = pl.program_id(0)
    n_rows = N_HEADS * n_new
    n_chunks = n_pages // chunk

    def page_copies(pid, slot, k):
        rows = pl.ds(k * PAGE_SIZE, PAGE_SIZE)
        return (pltpu.make_async_copy(ck_hbm.at[pid], kbuf.at[slot, rows, :], sems.at[0, slot]),
                pltpu.make_async_copy(krt_hbm.at[pid], rbuf.at[slot, k], sems.at[1, slot]))

    def start_chunk(row, j):
        for k in range(chunk):
            for cp in page_copies(pt_ref[row, j * chunk + k], j, k):
                cp.start(priority=k % 2)

    def wait_chunk(j):
        for k in range(chunk):
            for cp in page_copies(0, j, k):
                cp.wait()

    def start_ahead(j):
        ahead = j + SAMPLE_PREFETCH
        if ahead < n_chunks:
            start_chunk(b, ahead)
        else:
            @pl.when(b + 1 < pl.num_programs(0))
            def _():
                start_chunk(b + 1, ahead - n_chunks)

    @pl.when(b == 0)
    def _():
        for j in range(SAMPLE_PREFETCH):
            start_chunk(0, j)

    ql = ql_ref[...]
    qr = qr_ref[:, :ROPE_DIM]

    kc = kn_ref[...].astype(BF16)
    s = (_dot_nt(ql, kc) + _dot_nt(qr, krn_ref[...].astype(BF16))) * SCALE
    q_tok = lax.broadcasted_iota(jnp.int32, (n_rows, n_new), 0) & (n_new - 1)
    k_tok = lax.broadcasted_iota(jnp.int32, (n_rows, n_new), 1)
    s = jnp.where(k_tok <= q_tok, s, -jnp.inf)
    m = jnp.max(s, axis=-1, keepdims=True)
    p = jnp.exp(s - m)
    m_scr[0] = m
    l_scr[0] = jnp.sum(p, axis=-1, keepdims=True)
    acc_scr[0] = _dot(p.astype(BF16), kc)
    for c in range(1, N_CHAINS):
        m_scr[c] = jnp.full((n_rows, 1), -jnp.inf, F32)
        l_scr[c] = jnp.zeros((n_rows, 1), F32)
        acc_scr[c] = jnp.zeros((n_rows, KV_LORA), F32)

    per_chain = chunk // N_CHAINS
    chain_rows = per_chain * PAGE_SIZE

    def scores(t):
        j, c = divmod(t, N_CHAINS)
        if c == 0:
            start_ahead(j)
            wait_chunk(j)
        rows = slice(c * chain_rows, (c + 1) * chain_rows)
        kc_scr[rows, :] = kbuf[j, rows, :].astype(BF16)
        kc = kc_scr[rows, :]
        kr_t = jnp.concatenate([rbuf[j, k] for k in range(c * per_chain, (c + 1) * per_chain)],
                               axis=1).astype(BF16)
        return (_dot_nt(ql, kc) + _dot(qr, kr_t)) * SCALE, kc

    n_blocks = n_chunks * N_CHAINS
    pending = [scores(t) for t in range(SAMPLE_LOOKAHEAD)]
    for t in range(n_blocks):
        s, kc = pending.pop(0)
        if t + SAMPLE_LOOKAHEAD < n_blocks:
            pending.append(scores(t + SAMPLE_LOOKAHEAD))
        c = t % N_CHAINS
        m_prev = m_scr[c]
        m_new = jnp.maximum(m_prev, jnp.max(s, axis=-1, keepdims=True))
        corr = jnp.exp(m_prev - m_new)
        p = jnp.exp(s - m_new)
        l_scr[c] = l_scr[c] * corr + jnp.sum(p, axis=-1, keepdims=True)
        acc_scr[c] = acc_scr[c] * corr + _dot(p.astype(BF16), kc)
        m_scr[c] = m_new

    m = m_scr[0]
    for c in range(1, N_CHAINS):
        m = jnp.maximum(m, m_scr[c])
    l = jnp.zeros((n_rows, 1), F32)
    acc = jnp.zeros((n_rows, KV_LORA), F32)
    for c in range(N_CHAINS):
        w = jnp.exp(m_scr[c] - m)
        l = l + l_scr[c] * w
        acc = acc + acc_scr[c] * w
    o = acc / l
    for h in range(N_HEADS):
        o_ref[h] = o[h * n_new:(h + 1) * n_new]


def _value_up_kernel(o_ref, wuv_ref, y_ref):
    y_ref[...] = _dot(o_ref[...].astype(BF16), wuv_ref[...])


def _value_up(o_lat, wuv):
    _, t, _ = o_lat.shape
    return pl.pallas_call(
        _value_up_kernel,
        grid=(N_HEADS,),
        in_specs=[pl.BlockSpec((None, t, KV_LORA), lambda h: (h, 0, 0)),
                  pl.BlockSpec((None, KV_LORA, V_DIM), lambda h: (h, 0, 0))],
        out_specs=pl.BlockSpec((t, V_DIM), lambda h: (0, h)),
        out_shape=jax.ShapeDtypeStruct((t, N_HEADS * V_DIM), F32),
        compiler_params=_params(("parallel",)),
        name="value_up",
    )(o_lat, wuv)


def _sample_attention(page_table, ql, qr, kc_new, kr_new, cache_ckv, cache_kr_t, n_new, chunk):
    bsz, n_pages = page_table.shape
    n_rows = N_HEADS * n_new
    n_chunks = n_pages // chunk
    assert n_pages % chunk == 0 and chunk % N_CHAINS == 0 and 0 < SAMPLE_PREFETCH < n_chunks
    kern = functools.partial(_sattn_kernel, n_new=n_new, n_pages=n_pages, chunk=chunk)
    in_specs = [pl.BlockSpec((None, n_rows, KV_LORA), lambda b, pt: (b, 0, 0)),
                pl.BlockSpec((None, n_rows, LANES), lambda b, pt: (b, 0, 0)),
                pl.BlockSpec((n_new, KV_LORA), lambda b, pt: (b, 0)),
                pl.BlockSpec((n_new, ROPE_DIM), lambda b, pt: (b, 0)),
                pl.BlockSpec(memory_space=pl.ANY), pl.BlockSpec(memory_space=pl.ANY)]
    return pl.pallas_call(
        kern,
        grid_spec=pltpu.PrefetchScalarGridSpec(
            num_scalar_prefetch=1,
            grid=(bsz,),
            in_specs=in_specs,
            out_specs=pl.BlockSpec((N_HEADS, n_new, KV_LORA), lambda b, pt: (0, b, 0)),
            scratch_shapes=[pltpu.VMEM((n_chunks, chunk * PAGE_SIZE, KV_LORA), F32),
                            pltpu.VMEM((n_chunks, chunk, ROPE_DIM, PAGE_SIZE), F32),
                            pltpu.SemaphoreType.DMA((2, n_chunks)),
                            pltpu.VMEM((N_CHAINS, n_rows, 1), F32), pltpu.VMEM((N_CHAINS, n_rows, 1), F32),
                            pltpu.VMEM((N_CHAINS, n_rows, KV_LORA), F32),
                            pltpu.VMEM((chunk * PAGE_SIZE, KV_LORA), BF16)]),
        out_shape=jax.ShapeDtypeStruct((N_HEADS, bsz * n_new, KV_LORA), F32),
        compiler_params=_params(("arbitrary",)),
        name="sample_attention",
    )(page_table, ql, qr, kc_new, kr_new, cache_ckv, cache_kr_t)


def _mix_kernel(g_ref, ya_ref, x_ref, wglu_ref, bglu_ref, wout_ref, lg_ref, lb_ref, o_ref, ob_ref, *, alpha):
    part = g_ref.shape[0] // MIX_PARTS
    rows = [pl.ds(r * part, part) for r in range(MIX_PARTS)]

    def gate(r):
        g = g_ref[rows[r], :]
        return g, _dot(g.astype(BF16), wglu_ref[...])

    def glu(g, z):
        return (g * jax.nn.sigmoid(z + bglu_ref[...])).astype(BF16)

    def project(r, y):
        return (_dot(y, wout_ref[:SSM_WIDTH, :])
                + _dot(ya_ref[rows[r], :].astype(BF16), wout_ref[SSM_WIDTH:, :]))

    def finish(r, mix):
        x1 = _layer_norm(alpha * x_ref[rows[r], :] + mix, lg_ref[...], lb_ref[...])
        o_ref[rows[r], :] = x1
        ob_ref[rows[r], :] = x1.astype(BF16)

    gated = [gate(r) for r in range(MIX_PARTS)]
    mixes = []
    for r in range(MIX_PARTS):
        mixes.append(project(r, glu(*gated[r])))
        if r > 0:
            finish(r - 1, mixes[r - 1])
    finish(MIX_PARTS - 1, mixes[-1])


def _mix(g, y_att, x, wts, alpha, tm):
    t = x.shape[0]
    row = lambda w: pl.BlockSpec((tm, w), lambda i: (i, 0))
    return pl.pallas_call(
        functools.partial(_mix_kernel, alpha=alpha),
        grid=(t // tm,),
        in_specs=[row(SSM_WIDTH), row(N_HEADS * V_DIM), row(D_MODEL),
                  _const_spec((SSM_WIDTH, SSM_WIDTH)), _const_spec((1, SSM_WIDTH)),
                  _const_spec((D_MODEL, D_MODEL)), _const_spec((1, D_MODEL)), _const_spec((1, D_MODEL))],
        out_specs=[row(D_MODEL), row(D_MODEL)],
        out_shape=[jax.ShapeDtypeStruct((t, D_MODEL), F32), jax.ShapeDtypeStruct((t, D_MODEL), BF16)],
        compiler_params=_params(("parallel",)),
        name="mix",
    )(g, y_att, x, wts["w_glu"], wts["b_glu"], wts["w_out"], wts["ln1_g"], wts["ln1_b"])


def _ffn_kernel(x_ref, xb_ref, wg_ref, wu_ref, wd_ref, lg_ref, lb_ref, o_ref, *, alpha):
    f = pl.program_id(1)

    @pl.when(f == 0)
    def _():
        o_ref[...] = jnp.zeros(o_ref.shape, F32)

    xb = xb_ref[...]
    gate = _dot(xb, wg_ref[...])
    up = _dot(xb, wu_ref[...])
    o_ref[...] += _dot((jax.nn.silu(gate) * up).astype(BF16), wd_ref[...])

    @pl.when(f == pl.num_programs(1) - 1)
    def _():
        o_ref[...] = _layer_norm(alpha * x_ref[...] + o_ref[...], lg_ref[...], lb_ref[...])


def _ffn(x, xb, wts, alpha, tm, tf):
    t = x.shape[0]
    d_ff = wts["w_gate"].shape[1]
    return pl.pallas_call(
        functools.partial(_ffn_kernel, alpha=alpha),
        grid=(t // tm, d_ff // tf),
        in_specs=[pl.BlockSpec((tm, D_MODEL), lambda i, f: (i, 0)),
                  pl.BlockSpec((tm, D_MODEL), lambda i, f: (i, 0)),
                  pl.BlockSpec((D_MODEL, tf), lambda i, f: (0, f)),
                  pl.BlockSpec((D_MODEL, tf), lambda i, f: (0, f)),
                  pl.BlockSpec((tf, D_MODEL), lambda i, f: (f, 0)),
                  pl.BlockSpec((1, D_MODEL), lambda i, f: (0, 0)),
                  pl.BlockSpec((1, D_MODEL), lambda i, f: (0, 0))],
        out_specs=pl.BlockSpec((tm, D_MODEL), lambda i, f: (i, 0)),
        out_shape=jax.ShapeDtypeStruct((t, D_MODEL), F32),
        compiler_params=_params(("parallel", "arbitrary")),
        name="ffn",
    )(x, xb, wts["w_gate"], wts["w_up"], wts["w_down"], wts["ln2_g"], wts["ln2_b"])


def _swap_halves(w):
    half = w.shape[-1] // 2
    return jnp.concatenate([w[..., half:], w[..., :half]], axis=-1)


def _pad_lanes(w):
    return jnp.concatenate([w, jnp.zeros(w.shape[:-1] + (LANES - w.shape[-1],), w.dtype)], axis=-1)


def _layer_weights(w_in, g_q, w_uq, w_uk, g_kv, w_uv, w_glu, b_glu, w_out, ln1_g, ln1_b,
                   w_gate, w_up, w_down, ln2_g, ln2_b):
    n_main = SSM_WIDTH + Q_LORA + KV_LORA
    w_kr = w_in[:, n_main:]
    uq = w_uq.reshape(Q_LORA, N_HEADS, NOPE_DIM + ROPE_DIM)
    uq_rope = uq[:, :, NOPE_DIM:]
    w_qn = uq[:, :, :NOPE_DIM].reshape(Q_LORA, N_HEADS * NOPE_DIM).astype(BF16)
    w_qr = _pad_lanes(uq_rope).reshape(Q_LORA, N_HEADS * LANES).astype(BF16)
    w_qs = _pad_lanes(_swap_halves(uq_rope)).reshape(Q_LORA, N_HEADS * LANES).astype(BF16)
    return {
        "w_main": w_in[:, :n_main].astype(BF16),
        "w_kr": jnp.concatenate([_pad_lanes(w_kr), _pad_lanes(_swap_halves(w_kr))], axis=1).astype(BF16),
        "g_q": g_q.reshape(1, Q_LORA), "g_kv": g_kv.reshape(1, KV_LORA),
        "w_qn": w_qn, "w_qr": w_qr, "w_qs": w_qs,
        "w_qnT": w_qn.T,
        "w_qrT": uq_rope.reshape(Q_LORA, N_HEADS * ROPE_DIM).T.astype(BF16),
        "w_qsT": _swap_halves(uq_rope).reshape(Q_LORA, N_HEADS * ROPE_DIM).T.astype(BF16),
        "w_ukT": jnp.transpose(w_uk, (1, 2, 0)).astype(BF16),
        "w_uk": jnp.transpose(w_uk, (1, 0, 2)).astype(BF16),
        "w_uv": jnp.transpose(w_uv, (1, 0, 2)).astype(BF16),
        "w_uvT": jnp.transpose(w_uv, (1, 2, 0)).astype(BF16),
        "w_glu": w_glu.astype(BF16), "b_glu": b_glu.reshape(1, SSM_WIDTH),
        "w_out": w_out.astype(BF16),
        "ln1_g": ln1_g.reshape(1, D_MODEL), "ln1_b": ln1_b.reshape(1, D_MODEL),
        "w_gate": w_gate.astype(BF16), "w_up": w_up.astype(BF16), "w_down": w_down.astype(BF16),
        "ln2_g": ln2_g.reshape(1, D_MODEL), "ln2_b": ln2_b.reshape(1, D_MODEL),
    }


def _s5_weights(a_re, a_im, log_step, b_re, b_im, c_re, c_im, d):
    rep = lambda a: jnp.repeat(a, SSM_CH, axis=0)
    bt = lambda b: jnp.transpose(b, (0, 2, 1)).reshape(SSM_GROUPS * SSM_CH, SSM_STATE)
    abr, abi, bbr, bbi = _s5prep(rep(a_re), rep(a_im), rep(log_step.reshape(SSM_GROUPS, 1)),
                                 bt(b_re), bt(b_im))
    abr = abr[::SSM_CH]
    abi = abi[::SSM_CH]
    eye = jnp.eye(SLAB_GROUPS, dtype=F32)

    def blockdiag_in(b):
        b4 = b.reshape(N_SLABS, SLAB_GROUPS, SSM_CH, SSM_STATE)
        return jnp.einsum("sghp,gk->sghkp", b4, eye).reshape(N_SLABS, LANES, SLAB_STATE)

    def blockdiag_out(c):
        c4 = c.reshape(N_SLABS, SLAB_GROUPS, SSM_CH, SSM_STATE)
        return jnp.einsum("sghp,gk->skpgh", c4, eye).reshape(N_SLABS, SLAB_STATE, LANES)

    slab_row = lambda a: a.reshape(N_SLABS, 1, SLAB_STATE)
    return {
        "wb": jnp.concatenate([blockdiag_in(bbr), blockdiag_in(bbi)], axis=2).astype(BF16),
        "wc": jnp.concatenate([blockdiag_out(c_re), -blockdiag_out(c_im)], axis=1).astype(BF16),
        "a": jnp.concatenate([slab_row(abr), slab_row(abi)], axis=2),
        "d": d.reshape(N_SLABS, 1, LANES),
    }


def _rope_rows():
    half = ROPE_DIM // 2
    inv = ROPE_BASE ** (-2.0 * jnp.arange(half, dtype=F32) / ROPE_DIM)
    sign = jnp.concatenate([-jnp.ones((half,), F32), jnp.ones((half,), F32)])
    return (_pad_lanes(jnp.concatenate([inv, inv]).reshape(1, ROPE_DIM)),
            _pad_lanes(sign.reshape(1, ROPE_DIM)))


def kernel(x_prompt, x_sample, cache_ckv, cache_krope, state_ssm_re, state_ssm_im, page_table, w_in, g_q, w_uq, w_uk, g_kv, w_uv, ssm_a_re, ssm_a_im, ssm_log_step, ssm_b_re, ssm_b_im, ssm_c_re, ssm_c_im, ssm_d, w_glu, b_glu, w_out, ln1_g, ln1_b, w_gate, w_up, w_down, ln2_g, ln2_b):
    depth = w_in.shape[0]
    alpha = (2 * depth) ** 0.25
    bsz, seq, _ = x_prompt.shape
    dbsz, dseq, _ = x_sample.shape
    past = page_table.shape[1] * PAGE_SIZE
    rope_rows = _rope_rows()
    cos_p, sin_p = _rope_table(rope_rows, seq, 0, seq)
    cos_s, sin_s = _rope_table(rope_rows, ROW_TM, past, dseq)

    y_p = x_prompt.reshape(bsz * seq, D_MODEL)
    y_s = x_sample.reshape(dbsz * dseq, D_MODEL)
    outs = [[] for _ in range(8)]
    for l in range(depth):
        wts = _layer_weights(w_in[l], g_q[l], w_uq[l], w_uk[l], g_kv[l], w_uv[l], w_glu[l], b_glu[l],
                             w_out[l], ln1_g[l], ln1_b[l], w_gate[l], w_up[l], w_down[l], ln2_g[l], ln2_b[l])
        s5w = _s5_weights(ssm_a_re[l], ssm_a_im[l], ssm_log_step[l], ssm_b_re[l], ssm_b_im[l],
                          ssm_c_re[l], ssm_c_im[l], ssm_d[l])

        u, ckv, kr, ckvb, krb, ql_t, qr_t, v_t = _inproj(y_p, cos_p, sin_p, wts, tm=ATTN_TQ, prompt_layout=True)
        g, hl = _s5_prompt(u.reshape(bsz, seq, SSM_WIDTH), s5w)
        nk = seq // ATTN_TK
        y_att = _prompt_attention(ql_t, qr_t, ckvb.reshape(bsz, nk, ATTN_TK, KV_LORA),
                                  krb.reshape(bsz, nk, ATTN_TK, LANES), v_t.reshape(bsz, nk, KV_LORA, ATTN_TK),
                                  wts["w_uvT"], bsz, seq, tq=ATTN_TQ, tk=ATTN_TK)
        x1, x1b = _mix(g.reshape(bsz * seq, SSM_WIDTH), y_att, y_p, wts, alpha, tm=MIX_TM)
        y_p = _ffn(x1, x1b, wts, alpha, tm=FFN_TM, tf=FFN_TF)
        h_last = hl[:, :, SUBLANES - 1, :]
        outs[0].append(ckv.reshape(bsz, seq, KV_LORA))
        outs[1].append(kr.reshape(bsz, seq, ROPE_DIM))
        outs[2].append(h_last[:, :, :SLAB_STATE].reshape(bsz, SSM_GROUPS, SSM_STATE))
        outs[3].append(h_last[:, :, SLAB_STATE:].reshape(bsz, SSM_GROUPS, SSM_STATE))

        u, ckv, kr, _, _, ql, qr = _inproj(y_s, cos_s, sin_s, wts, tm=ROW_TM)
        g, hl_re, hl_im = _s5_sample(
            u, state_ssm_re[l].reshape(dbsz, SSM_GROUPS * SSM_STATE),
            state_ssm_im[l].reshape(dbsz, SSM_GROUPS * SSM_STATE), s5w, dseq)
        per_batch = lambda q: q.reshape(N_HEADS, dbsz, dseq, q.shape[-1]).transpose(1, 0, 2, 3).reshape(
            dbsz, N_HEADS * dseq, q.shape[-1])
        o_lat = _sample_attention(page_table, per_batch(ql), per_batch(qr), ckv, kr,
                                  cache_ckv[l], jnp.swapaxes(cache_krope[l], 1, 2), dseq,
                                  chunk=SAMPLE_PAGES_PER_STEP)
        y_att = _value_up(o_lat, wts["w_uv"])
        x1, x1b = _mix(g, y_att, y_s, wts, alpha, tm=MIX_TM)
        y_s = _ffn(x1, x1b, wts, alpha, tm=FFN_TM, tf=FFN_TF)
        outs[4].append(ckv.reshape(dbsz, dseq, KV_LORA))
        outs[5].append(kr.reshape(dbsz, dseq, ROPE_DIM))
        outs[6].append(hl_re.reshape(dbsz, SSM_GROUPS, SSM_STATE))
        outs[7].append(hl_im.reshape(dbsz, SSM_GROUPS, SSM_STATE))

    return (y_p.reshape(bsz, seq, D_MODEL), y_s.reshape(dbsz, dseq, D_MODEL),
            *[jnp.stack(o) for o in outs])
```

```python
import functools
import math

import jax
import jax.numpy as jnp
from jax import lax
from jax.experimental import pallas as pl
from jax.experimental.pallas import tpu as pltpu

F32 = jnp.float32
BF16 = jnp.bfloat16

D_MODEL = 2048
SSM_WIDTH = 1024
SSM_CH = 16
SSM_GROUPS = 64
SSM_STATE = 64
N_HEADS = 8
NOPE_DIM = 128
ROPE_DIM = 64
V_DIM = 128
Q_LORA = 512
KV_LORA = 512
ROPE_BASE = 10000.0
SCALE = (NOPE_DIM + ROPE_DIM) ** -0.5
PAGE_SIZE = 128
LANES = 128
SUBLANES = 8
SLAB_GROUPS = LANES // SSM_CH
N_SLABS = SSM_GROUPS // SLAB_GROUPS
SLAB_STATE = SLAB_GROUPS * SSM_STATE
VMEM_LIMIT = 56 * 1024 * 1024
ATTN_TQ = 256
ATTN_TK = 512
ATTN_HEAD_GROUP = 2
ATTN_LOOKAHEAD = 2
SAMPLE_PAGES_PER_STEP = 32
SAMPLE_LOOKAHEAD = 3
SAMPLE_PREFETCH = 1
ROW_TM = 256
MIX_TM = 512
MIX_PARTS = 2
FFN_TM = 512
FFN_TF = 512


def _dot(a, b):
    return jnp.dot(a, b, preferred_element_type=F32)


def _dot_nt(a, b):
    return lax.dot_general(a, b, (((1,), (1,)), ((), ())), preferred_element_type=F32)


def _const_spec(shape):
    n = len(shape)
    return pl.BlockSpec(shape, lambda *_: (0,) * n, pipeline_mode=pl.Buffered(1))


def _params(sem):
    return pltpu.CompilerParams(dimension_semantics=sem, vmem_limit_bytes=VMEM_LIMIT)


def _rms(x, g, eps=1e-6):
    return x * lax.rsqrt(jnp.mean(x * x, axis=-1, keepdims=True) + eps) * g


def _layer_norm(x, g, b, eps=1e-5):
    mu = jnp.mean(x, axis=-1, keepdims=True)
    xc = x - mu
    var = jnp.mean(xc * xc, axis=-1, keepdims=True)
    return xc * lax.rsqrt(var + eps) * g + b


def _rope_table_kernel(inv_ref, sgn_ref, cos_ref, sin_ref, *, pos_offset, pos_period):
    row = lax.broadcasted_iota(jnp.int32, cos_ref.shape, 0)
    ang = (pos_offset + (row & (pos_period - 1))).astype(F32) * inv_ref[...]
    cos_ref[...] = jnp.cos(ang)
    sin_ref[...] = jnp.sin(ang) * sgn_ref[...]


def _rope_table(rope_rows, n_rows, pos_offset, pos_period):
    assert pos_period & (pos_period - 1) == 0
    shape = jax.ShapeDtypeStruct((n_rows, LANES), F32)
    return pl.pallas_call(
        functools.partial(_rope_table_kernel, pos_offset=pos_offset, pos_period=pos_period),
        out_shape=[shape, shape], name="rope_table")(*rope_rows)


def _inproj_kernel(x_ref, wm_ref, wkr_ref, gq_ref, gkv_ref, cos_ref, sin_ref,
                   wqn_ref, wqr_ref, wqs_ref, wuk_ref,
                   u_ref, ckv_ref, kr_ref, ckvb_ref, krb_ref, ql_ref, qr_ref, *maybe_vt_ref, q_transposed):
    xb = x_ref[...].astype(BF16)
    cos = cos_ref[...]
    sin = sin_ref[...]
    proj = _dot(xb, wm_ref[...])
    u_ref[...] = proj[:, :SSM_WIDTH]
    cq = proj[:, SSM_WIDTH:SSM_WIDTH + Q_LORA]
    ckv = _rms(proj[:, SSM_WIDTH + Q_LORA:], gkv_ref[...])
    ckv_ref[...] = ckv
    ckvb_ref[...] = ckv.astype(BF16)
    kr2 = _dot(xb, wkr_ref[...])
    krope = kr2[:, :LANES] * cos + kr2[:, LANES:] * sin
    kr_ref[...] = krope[:, :ROPE_DIM]
    krb_ref[...] = krope.astype(BF16)
    cqn = _rms(cq, gq_ref[...])
    if q_transposed:
        maybe_vt_ref[0][...] = ckv.T.astype(BF16)
        cqt = cqn.T.astype(BF16)
        qn = _dot(wqn_ref[...], cqt)
        qa = _dot(wqr_ref[...], cqt)
        qs = _dot(wqs_ref[...], cqt)
        cos_t = cos.T[:ROPE_DIM]
        sin_t = sin.T[:ROPE_DIM]
        tm = cos.shape[0]
        for h in range(N_HEADS):
            sl = slice(h * LANES, (h + 1) * LANES)
            rl = slice(h * ROPE_DIM, (h + 1) * ROPE_DIM)
            cols = slice(h * tm, (h + 1) * tm)
            ql_ref[:, cols] = _dot(wuk_ref[h], qn[sl].astype(BF16)).astype(BF16)
            qr_ref[:ROPE_DIM, cols] = (qa[rl] * cos_t + qs[rl] * sin_t).astype(BF16)
            qr_ref[ROPE_DIM:, cols] = jnp.zeros((LANES - ROPE_DIM, tm), BF16)
    else:
        cqb = cqn.astype(BF16)
        qn = _dot(cqb, wqn_ref[...])
        qa = _dot(cqb, wqr_ref[...])
        qs = _dot(cqb, wqs_ref[...])
        for h in range(N_HEADS):
            sl = slice(h * LANES, (h + 1) * LANES)
            ql_ref[h] = _dot(qn[:, sl].astype(BF16), wuk_ref[h]).astype(BF16)
            qr_ref[h] = (qa[:, sl] * cos + qs[:, sl] * sin).astype(BF16)


def _inproj(x, cos, sin, wts, tm, prompt_layout=False):
    t = x.shape[0]
    table_tiles = cos.shape[0] // tm
    table = pl.BlockSpec((tm, LANES), lambda i: (i % table_tiles, 0))
    row = lambda w: pl.BlockSpec((tm, w), lambda i: (i, 0))
    extra_specs, extra_shapes = [], []
    if prompt_layout:
        assert tm == ATTN_TQ and ATTN_TK % tm == 0
        qspec = lambda w: pl.BlockSpec((None, w, N_HEADS * tm), lambda i: (i, 0, 0))
        qshape = lambda w: jax.ShapeDtypeStruct((t // tm, w, N_HEADS * tm), BF16)
        wq = [wts["w_qnT"], wts["w_qrT"], wts["w_qsT"], wts["w_uk"]]
        per_tk = ATTN_TK // tm
        extra_specs = [pl.BlockSpec((None, KV_LORA, tm), lambda i: (i // per_tk, 0, i % per_tk))]
        extra_shapes = [jax.ShapeDtypeStruct((t // ATTN_TK, KV_LORA, ATTN_TK), BF16)]
    else:
        qspec = lambda w: pl.BlockSpec((N_HEADS, tm, w), lambda i: (0, i, 0))
        qshape = lambda w: jax.ShapeDtypeStruct((N_HEADS, t, w), BF16)
        wq = [wts["w_qn"], wts["w_qr"], wts["w_qs"], wts["w_ukT"]]
    return pl.pallas_call(
        functools.partial(_inproj_kernel, q_transposed=prompt_layout),
        grid=(t // tm,),
        in_specs=[row(D_MODEL),
                  _const_spec((D_MODEL, 2048)), _const_spec((D_MODEL, 2 * LANES)),
                  _const_spec((1, Q_LORA)), _const_spec((1, KV_LORA)),
                  table, table] + [_const_spec(w.shape) for w in wq],
        out_specs=[row(SSM_WIDTH), row(KV_LORA), row(ROPE_DIM), row(KV_LORA), row(LANES),
                   qspec(KV_LORA), qspec(LANES)] + extra_specs,
        out_shape=[jax.ShapeDtypeStruct((t, SSM_WIDTH), F32),
                   jax.ShapeDtypeStruct((t, KV_LORA), F32),
                   jax.ShapeDtypeStruct((t, ROPE_DIM), F32),
                   jax.ShapeDtypeStruct((t, KV_LORA), BF16),
                   jax.ShapeDtypeStruct((t, LANES), BF16),
                   qshape(KV_LORA), qshape(LANES)] + extra_shapes,
        compiler_params=_params(("parallel",)),
        name="inproj",
    )(x, wts["w_main"], wts["w_kr"], wts["g_q"], wts["g_kv"], cos, sin, *wq)


def _s5prep_kernel(are_ref, aim_ref, ls_ref, bre_ref, bim_ref,
                   abr_ref, abi_ref, bbr_ref, bbi_ref):
    lr = are_ref[...]
    li = aim_ref[...]
    delta = jnp.exp(ls_ref[...])
    mag = jnp.exp(lr * delta)
    ar = mag * jnp.cos(li * delta)
    ai = mag * jnp.sin(li * delta)
    abr_ref[...] = ar
    abi_ref[...] = ai
    den = lr * lr + li * li
    fr = ((ar - 1.0) * lr + ai * li) / den
    fi = (ai * lr - (ar - 1.0) * li) / den
    br = bre_ref[...]
    bi = bim_ref[...]
    bbr_ref[...] = fr * br - fi * bi
    bbi_ref[...] = fr * bi + fi * br


def _s5prep(a_re, a_im, log_step, bt_re, bt_im):
    shp = jax.ShapeDtypeStruct((SSM_GROUPS * SSM_CH, SSM_STATE), F32)
    return pl.pallas_call(_s5prep_kernel, out_shape=[shp] * 4, name="s5prep")(
        a_re, a_im, log_step, bt_re, bt_im)


def _cmul_add(ar, ai, hr, hi, br, bi):
    return ar * hr - ai * hi + br, ar * hi + ai * hr + bi


def _s5p_kernel(u_ref, wb_ref, wc_ref, a_ref, d_ref, g_ref, hl_ref, bu_scr, e_scr, u_scr, g_scr,
                *, seg_len, chunk):
    n_rows = SUBLANES * seg_len
    n_chunks = n_rows // chunk
    steps = chunk // SUBLANES
    half = SLAB_STATE
    wb = wb_ref[...]
    wc = wc_ref[...]
    d = d_ref[...]
    ar = jnp.broadcast_to(a_ref[:, :half], (SUBLANES, half))
    ai = jnp.broadcast_to(a_ref[:, half:], (SUBLANES, half))

    for seg in range(SUBLANES):
        u_scr[pl.ds(seg, seg_len, stride=SUBLANES), :] = u_ref[pl.ds(seg * seg_len, seg_len), :]

    def project_in(c):
        rows = pl.ds(c * chunk, chunk)
        bu_scr[rows, :] = _dot(u_scr[rows, :].astype(BF16), wb)

    def project_out(c):
        rows = pl.ds(c * chunk, chunk)
        y = _dot(bu_scr[rows, :].astype(BF16), wc) + d * u_scr[rows, :]
        g_scr[rows, :] = jax.nn.gelu(y)
        for seg in range(SUBLANES):
            g_ref[pl.ds(seg * seg_len + c * steps, steps), :] = (
                g_scr[pl.ds(c * chunk + seg, steps, stride=SUBLANES), :])

    def scan_chunk(c, carry, store):
        hr, hi = carry
        for t in range(c * steps, (c + 1) * steps):
            rows = pl.ds(t * SUBLANES, SUBLANES)
            hr, hi = _cmul_add(ar, ai, hr, hi, bu_scr[rows, :half], bu_scr[rows, half:])
            if store:
                bu_scr[rows, :half] = hr
                bu_scr[rows, half:] = hi
        return hr, hi

    zero = jnp.zeros((SUBLANES, half), F32)
    carry = (zero, zero)
    project_in(0)
    for c in range(n_chunks):
        if c + 1 < n_chunks:
            project_in(c + 1)
        carry = scan_chunk(c, carry, store=False)
    er, ei = carry
    pr, pi = ar, ai
    for _ in range(int(math.log2(seg_len))):
        pr, pi = pr * pr - pi * pi, 2.0 * pr * pi
    e_scr[0:SUBLANES, :half] = er
    e_scr[0:SUBLANES, half:] = ei
    cr = jnp.zeros((1, half), F32)
    ci = jnp.zeros((1, half), F32)
    e_scr[SUBLANES:SUBLANES + 1, :half] = cr
    e_scr[SUBLANES:SUBLANES + 1, half:] = ci
    for s in range(SUBLANES - 1):
        cr, ci = _cmul_add(pr[0:1], pi[0:1], cr, ci, e_scr[s:s + 1, :half], e_scr[s:s + 1, half:])
        e_scr[SUBLANES + s + 1:SUBLANES + s + 2, :half] = cr
        e_scr[SUBLANES + s + 1:SUBLANES + s + 2, half:] = ci
    carry = (e_scr[SUBLANES:2 * SUBLANES, :half], e_scr[SUBLANES:2 * SUBLANES, half:])
    for c in range(n_chunks):
        if c > 0:
            project_out(c - 1)
        carry = scan_chunk(c, carry, store=True)
    project_out(n_chunks - 1)
    hr, hi = carry
    hl_ref[:, :half] = hr
    hl_ref[:, half:] = hi


def _s5_prompt(u, s5w):
    bsz, n_rows, _ = u.shape
    seg_len = n_rows // SUBLANES
    kern = functools.partial(_s5p_kernel, seg_len=seg_len, chunk=512)
    slab = lambda shape: pl.BlockSpec((None,) + shape, lambda b, s: (s, 0, 0))
    return pl.pallas_call(
        kern,
        grid=(bsz, N_SLABS),
        in_specs=[pl.BlockSpec((None, n_rows, LANES), lambda b, s: (b, 0, s)),
                  slab((LANES, 2 * SLAB_STATE)), slab((2 * SLAB_STATE, LANES)),
                  slab((1, 2 * SLAB_STATE)), slab((1, LANES))],
        out_specs=[pl.BlockSpec((None, n_rows, LANES), lambda b, s: (b, 0, s)),
                   pl.BlockSpec((None, None, SUBLANES, 2 * SLAB_STATE), lambda b, s: (b, s, 0, 0))],
        out_shape=[jax.ShapeDtypeStruct((bsz, n_rows, SSM_WIDTH), F32),
                   jax.ShapeDtypeStruct((bsz, N_SLABS, SUBLANES, 2 * SLAB_STATE), F32)],
        scratch_shapes=[pltpu.VMEM((n_rows, 2 * SLAB_STATE), F32),
                        pltpu.VMEM((2 * SUBLANES, 2 * SLAB_STATE), F32),
                        pltpu.VMEM((n_rows, LANES), F32), pltpu.VMEM((n_rows, LANES), F32)],
        compiler_params=_params(("parallel", "parallel")),
        name="s5_prompt",
    )(u, s5w["wb"], s5w["wc"], s5w["a"], s5w["d"])


def _s5s_kernel(u_ref, wb_ref, wc_ref, a_ref, d_ref, h0r_ref, h0i_ref,
                g_ref, hlr_ref, hli_ref, h_scr, u_scr, *, n_steps, bsz):
    half = SLAB_STATE
    for t in range(n_steps):
        u_scr[pl.ds(t * bsz, bsz), :] = u_ref[pl.ds(t, bsz, stride=n_steps), :]
    u = u_scr[...]
    h_scr[...] = _dot(u.astype(BF16), wb_ref[...])
    ar = a_ref[:, :half]
    ai = a_ref[:, half:]
    hr = h0r_ref[...]
    hi = h0i_ref[...]
    for t in range(n_steps):
        rows = pl.ds(t * bsz, bsz)
        hr, hi = _cmul_add(ar, ai, hr, hi, h_scr[rows, :half], h_scr[rows, half:])
        h_scr[rows, :half] = hr
        h_scr[rows, half:] = hi
    hlr_ref[...] = hr
    hli_ref[...] = hi
    y = _dot(h_scr[...].astype(BF16), wc_ref[...]) + d_ref[...] * u
    u_scr[...] = jax.nn.gelu(y)
    for t in range(n_steps):
        g_ref[pl.ds(t, bsz, stride=n_steps), :] = u_scr[pl.ds(t * bsz, bsz), :]


def _s5_sample(u, h0_re, h0_im, s5w, n_steps):
    n_rows = u.shape[0]
    bsz = n_rows // n_steps
    kern = functools.partial(_s5s_kernel, n_steps=n_steps, bsz=bsz)
    slab = lambda shape: pl.BlockSpec((None,) + shape, lambda s: (s, 0, 0))
    col = lambda r, w: pl.BlockSpec((r, w), lambda s: (0, s))
    return pl.pallas_call(
        kern,
        grid=(N_SLABS,),
        in_specs=[col(n_rows, LANES),
                  slab((LANES, 2 * SLAB_STATE)), slab((2 * SLAB_STATE, LANES)),
                  slab((1, 2 * SLAB_STATE)), slab((1, LANES)),
                  col(bsz, SLAB_STATE), col(bsz, SLAB_STATE)],
        out_specs=[col(n_rows, LANES), col(bsz, SLAB_STATE), col(bsz, SLAB_STATE)],
        out_shape=[jax.ShapeDtypeStruct((n_rows, SSM_WIDTH), F32),
                   jax.ShapeDtypeStruct((bsz, SSM_GROUPS * SSM_STATE), F32),
                   jax.ShapeDtypeStruct((bsz, SSM_GROUPS * SSM_STATE), F32)],
        scratch_shapes=[pltpu.VMEM((n_rows, 2 * SLAB_STATE), F32), pltpu.VMEM((n_rows, LANES), F32)],
        compiler_params=_params(("parallel",)),
        name="s5_sample",
    )(u, s5w["wb"], s5w["wc"], s5w["a"], s5w["d"], h0_re, h0_im)


def _pattn_kernel(ql_ref, qr_ref, kc_ref, kr_ref, vt_ref, wuv_ref, o_ref, m_scr, l_scr, acc_scr, *, tq, tk):
    i = pl.program_id(1)
    gw = ATTN_HEAD_GROUP * tq
    n_groups = N_HEADS // ATTN_HEAD_GROUP
    m_scr[...] = jnp.full(m_scr.shape, -jnp.inf, F32)
    l_scr[...] = jnp.zeros(l_scr.shape, F32)
    acc_scr[...] = jnp.zeros(acc_scr.shape, F32)

    def block(j, lo=0, n=tk, diagonal=False):
        kc = kc_ref[j, lo:lo + n, :]
        kr = kr_ref[j, lo:lo + n, :]
        vt = vt_ref[j, :, lo:lo + n]
        if diagonal:
            keep = (lax.broadcasted_iota(jnp.int32, (n, gw), 0)
                    <= (lax.broadcasted_iota(jnp.int32, (n, gw), 1) & (tq - 1)))

        def scores(g):
            cols = slice(g * gw, (g + 1) * gw)
            s = (_dot(kc, ql_ref[:, cols]) + _dot(kr, qr_ref[:, cols])) * SCALE
            return jnp.where(keep, s, -jnp.inf) if diagonal else s

        pending = [scores(g) for g in range(ATTN_LOOKAHEAD)]
        for g in range(n_groups):
            s = pending.pop(0)
            if g + ATTN_LOOKAHEAD < n_groups:
                pending.append(scores(g + ATTN_LOOKAHEAD))
            cols = slice(g * gw, (g + 1) * gw)
            m_prev = m_scr[:, cols]
            m_new = jnp.maximum(m_prev, jnp.max(s, axis=0, keepdims=True))
            corr = jnp.exp(m_prev - m_new)
            p = jnp.exp(s - m_new)
            l_scr[:, cols] = l_scr[:, cols] * corr + jnp.sum(p, axis=0, keepdims=True)
            acc_scr[:, cols] = acc_scr[:, cols] * corr + _dot(vt, p.astype(BF16))
            m_scr[:, cols] = m_new

    def body(j, carry):
        block(j)
        return carry

    j_last = (i * tq) // tk
    lax.fori_loop(0, j_last, body, 0)

    @pl.when(i % 2 == 0)
    def _():
        block(j_last, 0, tq, diagonal=True)

    @pl.when(i % 2 == 1)
    def _():
        block(j_last, 0, tq)
        block(j_last, tq, tq, diagonal=True)

    for h in range(N_HEADS):
        cols = slice(h * tq, (h + 1) * tq)
        y_t = _dot(wuv_ref[h], acc_scr[:, cols].astype(BF16)) / l_scr[:, cols]
        o_ref[:, h * V_DIM:(h + 1) * V_DIM] = y_t.T.astype(o_ref.dtype)


def _prompt_attention(ql_t, qr_t, kc, kr, v_t, wuv_t, bsz, seq, tq, tk):
    nq = seq // tq
    nk = seq // tk
    assert tk == 2 * tq and tq & (tq - 1) == 0 and N_HEADS % ATTN_HEAD_GROUP == 0
    kern = functools.partial(_pattn_kernel, tq=tq, tk=tk)
    per_batch = lambda r, c: pl.BlockSpec((None, nk, r, c), lambda b, i: (b, 0, 0, 0))
    return pl.pallas_call(
        kern,
        grid=(bsz, nq),
        in_specs=[pl.BlockSpec((None, KV_LORA, N_HEADS * tq), lambda b, i: (b * nq + i, 0, 0)),
                  pl.BlockSpec((None, LANES, N_HEADS * tq), lambda b, i: (b * nq + i, 0, 0)),
                  per_batch(tk, KV_LORA), per_batch(tk, LANES), per_batch(KV_LORA, tk),
                  _const_spec((N_HEADS, V_DIM, KV_LORA))],
        out_specs=pl.BlockSpec((tq, N_HEADS * V_DIM), lambda b, i: (b * nq + i, 0)),
        out_shape=jax.ShapeDtypeStruct((bsz * seq, N_HEADS * V_DIM), BF16),
        scratch_shapes=[pltpu.VMEM((1, N_HEADS * tq), F32), pltpu.VMEM((1, N_HEADS * tq), F32),
                        pltpu.VMEM((KV_LORA, N_HEADS * tq), F32)],
        compiler_params=_params(("parallel", "arbitrary")),
        name="prompt_attention",
    )(ql_t, qr_t, kc, kr, v_t, wuv_t)


N_CHAINS = 4


def _sattn_kernel(pt_ref, ql_ref, qr_ref, kn_ref, krn_ref, ck_hbm, krt_hbm, o_ref,
                  kbuf, rbuf, sems, m_scr, l_scr, acc_scr, kc_scr, *, n_new, n_pages, chunk):
    b = pl.program_id(0)
    n_rows = N_HEADS * n_new
    n_chunks = n_pages // chunk

    def page_copies(pid, slot, k):
        rows = pl.ds(k * PAGE_SIZE, PAGE_SIZE)
        return (pltpu.make_async_copy(ck_hbm.at[pid], kbuf.at[slot, rows, :], sems.at[0, slot]),
                pltpu.make_async_copy(krt_hbm.at[pid], rbuf.at[slot, k], sems.at[1, slot]))

    def start_chunk(row, j):
        for k in range(chunk):
            for cp in page_copies(pt_ref[row, j * chunk + k], j, k):
                cp.start(priority=k % 2)

    def wait_chunk(j):
        for k in range(chunk):
            for cp in page_copies(0, j, k):
                cp.wait()

    def start_ahead(j):
        ahead = j + SAMPLE_PREFETCH
        if ahead < n_chunks:
            start_chunk(b, ahead)
        else:
            @pl.when(b + 1 < pl.num_programs(0))
            def _():
                start_chunk(b + 1, ahead - n_chunks)

    @pl.when(b == 0)
    def _():
        for j in range(SAMPLE_PREFETCH):
            start_chunk(0, j)

    ql = ql_ref[...]
    qr = qr_ref[:, :ROPE_DIM]

    kc = kn_ref[...].astype(BF16)
    s = (_dot_nt(ql, kc) + _dot_nt(qr, krn_ref[...].astype(BF16))) * SCALE
    q_tok = lax.broadcasted_iota(jnp.int32, (n_rows, n_new), 0) & (n_new - 1)
    k_tok = lax.broadcasted_iota(jnp.int32, (n_rows, n_new), 1)
    s = jnp.where(k_tok <= q_tok, s, -jnp.inf)
    m = jnp.max(s, axis=-1, keepdims=True)
    p = jnp.exp(s - m)
    m_scr[0] = m
    l_scr[0] = jnp.sum(p, axis=-1, keepdims=True)
    acc_scr[0] = _dot(p.astype(BF16), kc)
    for c in range(1, N_CHAINS):
        m_scr[c] = jnp.full((n_rows, 1), -jnp.inf, F32)
        l_scr[c] = jnp.zeros((n_rows, 1), F32)
        acc_scr[c] = jnp.zeros((n_rows, KV_LORA), F32)

    per_chain = chunk // N_CHAINS
    chain_rows = per_chain * PAGE_SIZE

    def scores(t):
        j, c = divmod(t, N_CHAINS)
        if c == 0:
            start_ahead(j)
            wait_chunk(j)
        rows = slice(c * chain_rows, (c + 1) * chain_rows)
        kc_scr[rows, :] = kbuf[j, rows, :].astype(BF16)
        kc = kc_scr[rows, :]
        kr_t = jnp.concatenate([rbuf[j, k] for k in range(c * per_chain, (c + 1) * per_chain)],
                               axis=1).astype(BF16)
        return (_dot_nt(ql, kc) + _dot(qr, kr_t)) * SCALE, kc

    n_blocks = n_chunks * N_CHAINS
    pending = [scores(t) for t in range(SAMPLE_LOOKAHEAD)]
    for t in range(n_blocks):
        s, kc = pending.pop(0)
        if t + SAMPLE_LOOKAHEAD < n_blocks:
            pending.append(scores(t + SAMPLE_LOOKAHEAD))
        c = t % N_CHAINS
        m_prev = m_scr[c]
        m_new = jnp.maximum(m_prev, jnp.max(s, axis=-1, keepdims=True))
        corr = jnp.exp(m_prev - m_new)
        p = jnp.exp(s - m_new)
        l_scr[c] = l_scr[c] * corr + jnp.sum(p, axis=-1, keepdims=True)
        acc_scr[c] = acc_scr[c] * corr + _dot(p.astype(BF16), kc)
        m_scr[c] = m_new

    m = m_scr[0]
    for c in range(1, N_CHAINS):
        m = jnp.maximum(m, m_scr[c])
    l = jnp.zeros((n_rows, 1), F32)
    acc = jnp.zeros((n_rows, KV_LORA), F32)
    for c in range(N_CHAINS):
        w = jnp.exp(m_scr[c] - m)
        l = l + l_scr[c] * w
        acc = acc + acc_scr[c] * w
    o = acc / l
    for h in range(N_HEADS):
        o_ref[h] = o[h * n_new:(h + 1) * n_new]


def _value_up_kernel(o_ref, wuv_ref, y_ref):
    y_ref[...] = _dot(o_ref[...].astype(BF16), wuv_ref[...])


def _value_up(o_lat, wuv):
    _, t, _ = o_lat.shape
    return pl.pallas_call(
        _value_up_kernel,
        grid=(N_HEADS,),
        in_specs=[pl.BlockSpec((None, t, KV_LORA), lambda h: (h, 0, 0)),
                  pl.BlockSpec((None, KV_LORA, V_DIM), lambda h: (h, 0, 0))],
        out_specs=pl.BlockSpec((t, V_DIM), lambda h: (0, h)),
        out_shape=jax.ShapeDtypeStruct((t, N_HEADS * V_DIM), F32),
        compiler_params=_params(("parallel",)),
        name="value_up",
    )(o_lat, wuv)


def _sample_attention(page_table, ql, qr, kc_new, kr_new, cache_ckv, cache_kr_t, n_new, chunk):
    bsz, n_pages = page_table.shape
    n_rows = N_HEADS * n_new
    n_chunks = n_pages // chunk
    assert n_pages % chunk == 0 and chunk % N_CHAINS == 0 and 0 < SAMPLE_PREFETCH < n_chunks
    kern = functools.partial(_sattn_kernel, n_new=n_new, n_pages=n_pages, chunk=chunk)
    in_specs = [pl.BlockSpec((None, n_rows, KV_LORA), lambda b, pt: (b, 0, 0)),
                pl.BlockSpec((None, n_rows, LANES), lambda b, pt: (b, 0, 0)),
                pl.BlockSpec((n_new, KV_LORA), lambda b, pt: (b, 0)),
                pl.BlockSpec((n_new, ROPE_DIM), lambda b, pt: (b, 0)),
                pl.BlockSpec(memory_space=pl.ANY), pl.BlockSpec(memory_space=pl.ANY)]
    return pl.pallas_call(
        kern,
        grid_spec=pltpu.PrefetchScalarGridSpec(
            num_scalar_prefetch=1,
            grid=(bsz,),
            in_specs=in_specs,
            out_specs=pl.BlockSpec((N_HEADS, n_new, KV_LORA), lambda b, pt: (0, b, 0)),
            scratch_shapes=[pltpu.VMEM((n_chunks, chunk * PAGE_SIZE, KV_LORA), F32),
                            pltpu.VMEM((n_chunks, chunk, ROPE_DIM, PAGE_SIZE), F32),
                            pltpu.SemaphoreType.DMA((2, n_chunks)),
                            pltpu.VMEM((N_CHAINS, n_rows, 1), F32), pltpu.VMEM((N_CHAINS, n_rows, 1), F32),
                            pltpu.VMEM((N_CHAINS, n_rows, KV_LORA), F32),
                            pltpu.VMEM((chunk * PAGE_SIZE, KV_LORA), BF16)]),
        out_shape=jax.ShapeDtypeStruct((N_HEADS, bsz * n_new, KV_LORA), F32),
        compiler_params=_params(("arbitrary",)),
        name="sample_attention",
    )(page_table, ql, qr, kc_new, kr_new, cache_ckv, cache_kr_t)


def _mix_kernel(g_ref, ya_ref, x_ref, wglu_ref, bglu_ref, wout_ref, lg_ref, lb_ref, o_ref, ob_ref, *, alpha):
    part = g_ref.shape[0] // MIX_PARTS
    rows = [pl.ds(r * part, part) for r in range(MIX_PARTS)]

    def gate(r):
        g = g_ref[rows[r], :]
        return g, _dot(g.astype(BF16), wglu_ref[...])

    def glu(g, z):
        return (g * jax.nn.sigmoid(z + bglu_ref[...])).astype(BF16)

    def project(r, y):
        return (_dot(y, wout_ref[:SSM_WIDTH, :])
                + _dot(ya_ref[rows[r], :].astype(BF16), wout_ref[SSM_WIDTH:, :]))

    def finish(r, mix):
        x1 = _layer_norm(alpha * x_ref[rows[r], :] + mix, lg_ref[...], lb_ref[...])
        o_ref[rows[r], :] = x1
        ob_ref[rows[r], :] = x1.astype(BF16)

    gated = [gate(r) for r in range(MIX_PARTS)]
    mixes = []
    for r in range(MIX_PARTS):
        mixes.append(project(r, glu(*gated[r])))
        if r > 0:
            finish(r - 1, mixes[r - 1])
    finish(MIX_PARTS - 1, mixes[-1])


def _mix(g, y_att, x, wts, alpha, tm):
    t = x.shape[0]
    row = lambda w: pl.BlockSpec((tm, w), lambda i: (i, 0))
    return pl.pallas_call(
        functools.partial(_mix_kernel, alpha=alpha),
        grid=(t // tm,),
        in_specs=[row(SSM_WIDTH), row(N_HEADS * V_DIM), row(D_MODEL),
                  _const_spec((SSM_WIDTH, SSM_WIDTH)), _const_spec((1, SSM_WIDTH)),
                  _const_spec((D_MODEL, D_MODEL)), _const_spec((1, D_MODEL)), _const_spec((1, D_MODEL))],
        out_specs=[row(D_MODEL), row(D_MODEL)],
        out_shape=[jax.ShapeDtypeStruct((t, D_MODEL), F32), jax.ShapeDtypeStruct((t, D_MODEL), BF16)],
        compiler_params=_params(("parallel",)),
        name="mix",
    )(g, y_att, x, wts["w_glu"], wts["b_glu"], wts["w_out"], wts["ln1_g"], wts["ln1_b"])


def _ffn_kernel(x_ref, xb_ref, wg_ref, wu_ref, wd_ref, lg_ref, lb_ref, o_ref, *, alpha):
    f = pl.program_id(1)

    @pl.when(f == 0)
    def _():
        o_ref[...] = jnp.zeros(o_ref.shape, F32)

    xb = xb_ref[...]
    gate = _dot(xb, wg_ref[...])
    up = _dot(xb, wu_ref[...])
    o_ref[...] += _dot((jax.nn.silu(gate) * up).astype(BF16), wd_ref[...])

    @pl.when(f == pl.num_programs(1) - 1)
    def _():
        o_ref[...] = _layer_norm(alpha * x_ref[...] + o_ref[...], lg_ref[...], lb_ref[...])


def _ffn(x, xb, wts, alpha, tm, tf):
    t = x.shape[0]
    d_ff = wts["w_gate"].shape[1]
    return pl.pallas_call(
        functools.partial(_ffn_kernel, alpha=alpha),
        grid=(t // tm, d_ff // tf),
        in_specs=[pl.BlockSpec((tm, D_MODEL), lambda i, f: (i, 0)),
                  pl.BlockSpec((tm, D_MODEL), lambda i, f: (i, 0)),
                  pl.BlockSpec((D_MODEL, tf), lambda i, f: (0, f)),
                  pl.BlockSpec((D_MODEL, tf), lambda i, f: (0, f)),
                  pl.BlockSpec((tf, D_MODEL), lambda i, f: (f, 0)),
                  pl.BlockSpec((1, D_MODEL), lambda i, f: (0, 0)),
                  pl.BlockSpec((1, D_MODEL), lambda i, f: (0, 0))],
        out_specs=pl.BlockSpec((tm, D_MODEL), lambda i, f: (i, 0)),
        out_shape=jax.ShapeDtypeStruct((t, D_MODEL), F32),
        compiler_params=_params(("parallel", "arbitrary")),
        name="ffn",
    )(x, xb, wts["w_gate"], wts["w_up"], wts["w_down"], wts["ln2_g"], wts["ln2_b"])


def _swap_halves(w):
    half = w.shape[-1] // 2
    return jnp.concatenate([w[..., half:], w[..., :half]], axis=-1)


def _pad_lanes(w):
    return jnp.concatenate([w, jnp.zeros(w.shape[:-1] + (LANES - w.shape[-1],), w.dtype)], axis=-1)


def _layer_weights(w_in, g_q, w_uq, w_uk, g_kv, w_uv, w_glu, b_glu, w_out, ln1_g, ln1_b,
                   w_gate, w_up, w_down, ln2_g, ln2_b):
    n_main = SSM_WIDTH + Q_LORA + KV_LORA
    w_kr = w_in[:, n_main:]
    uq = w_uq.reshape(Q_LORA, N_HEADS, NOPE_DIM + ROPE_DIM)
    uq_rope = uq[:, :, NOPE_DIM:]
    w_qn = uq[:, :, :NOPE_DIM].reshape(Q_LORA, N_HEADS * NOPE_DIM).astype(BF16)
    w_qr = _pad_lanes(uq_rope).reshape(Q_LORA, N_HEADS * LANES).astype(BF16)
    w_qs = _pad_lanes(_swap_halves(uq_rope)).reshape(Q_LORA, N_HEADS * LANES).astype(BF16)
    return {
        "w_main": w_in[:, :n_main].astype(BF16),
        "w_kr": jnp.concatenate([_pad_lanes(w_kr), _pad_lanes(_swap_halves(w_kr))], axis=1).astype(BF16),
        "g_q": g_q.reshape(1, Q_LORA), "g_kv": g_kv.reshape(1, KV_LORA),
        "w_qn": w_qn, "w_qr": w_qr, "w_qs": w_qs,
        "w_qnT": w_qn.T,
        "w_qrT": uq_rope.reshape(Q_LORA, N_HEADS * ROPE_DIM).T.astype(BF16),
        "w_qsT": _swap_halves(uq_rope).reshape(Q_LORA, N_HEADS * ROPE_DIM).T.astype(BF16),
        "w_ukT": jnp.transpose(w_uk, (1, 2, 0)).astype(BF16),
        "w_uk": jnp.transpose(w_uk, (1, 0, 2)).astype(BF16),
        "w_uv": jnp.transpose(w_uv, (1, 0, 2)).astype(BF16),
        "w_uvT": jnp.transpose(w_uv, (1, 2, 0)).astype(BF16),
        "w_glu": w_glu.astype(BF16), "b_glu": b_glu.reshape(1, SSM_WIDTH),
        "w_out": w_out.astype(BF16),
        "ln1_g": ln1_g.reshape(1, D_MODEL), "ln1_b": ln1_b.reshape(1, D_MODEL),
        "w_gate": w_gate.astype(BF16), "w_up": w_up.astype(BF16), "w_down": w_down.astype(BF16),
        "ln2_g": ln2_g.reshape(1, D_MODEL), "ln2_b": ln2_b.reshape(1, D_MODEL),
    }


def _s5_weights(a_re, a_im, log_step, b_re, b_im, c_re, c_im, d):
    rep = lambda a: jnp.repeat(a, SSM_CH, axis=0)
    bt = lambda b: jnp.transpose(b, (0, 2, 1)).reshape(SSM_GROUPS * SSM_CH, SSM_STATE)
    abr, abi, bbr, bbi = _s5prep(rep(a_re), rep(a_im), rep(log_step.reshape(SSM_GROUPS, 1)),
                                 bt(b_re), bt(b_im))
    abr = abr[::SSM_CH]
    abi = abi[::SSM_CH]
    eye = jnp.eye(SLAB_GROUPS, dtype=F32)

    def blockdiag_in(b):
        b4 = b.reshape(N_SLABS, SLAB_GROUPS, SSM_CH, SSM_STATE)
        return jnp.einsum("sghp,gk->sghkp", b4, eye).reshape(N_SLABS, LANES, SLAB_STATE)

    def blockdiag_out(c):
        c4 = c.reshape(N_SLABS, SLAB_GROUPS, SSM_CH, SSM_STATE)
        return jnp.einsum("sghp,gk->skpgh", c4, eye).reshape(N_SLABS, SLAB_STATE, LANES)

    slab_row = lambda a: a.reshape(N_SLABS, 1, SLAB_STATE)
    return {
        "wb": jnp.concatenate([blockdiag_in(bbr), blockdiag_in(bbi)], axis=2).astype(BF16),
        "wc": jnp.concatenate([blockdiag_out(c_re), -blockdiag_out(c_im)], axis=1).astype(BF16),
        "a": jnp.concatenate([slab_row(abr), slab_row(abi)], axis=2),
        "d": d.reshape(N_SLABS, 1, LANES),
    }


def _rope_rows():
    half = ROPE_DIM // 2
    inv = ROPE_BASE ** (-2.0 * jnp.arange(half, dtype=F32) / ROPE_DIM)
    sign = jnp.concatenate([-jnp.ones((half,), F32), jnp.ones((half,), F32)])
    return (_pad_lanes(jnp.concatenate([inv, inv]).reshape(1, ROPE_DIM)),
            _pad_lanes(sign.reshape(1, ROPE_DIM)))


def kernel(x_prompt, x_sample, cache_ckv, cache_krope, state_ssm_re, state_ssm_im, page_table, w_in, g_q, w_uq, w_uk, g_kv, w_uv, ssm_a_re, ssm_a_im, ssm_log_step, ssm_b_re, ssm_b_im, ssm_c_re, ssm_c_im, ssm_d, w_glu, b_glu, w_out, ln1_g, ln1_b, w_gate, w_up, w_down, ln2_g, ln2_b):
    depth = w_in.shape[0]
    alpha = (2 * depth) ** 0.25
    bsz, seq, _ = x_prompt.shape
    dbsz, dseq, _ = x_sample.shape
    past = page_table.shape[1] * PAGE_SIZE
    rope_rows = _rope_rows()
    cos_p, sin_p = _rope_table(rope_rows, seq, 0, seq)
    cos_s, sin_s = _rope_table(rope_rows, ROW_TM, past, dseq)

    y_p = x_prompt.reshape(bsz * seq, D_MODEL)
    y_s = x_sample.reshape(dbsz * dseq, D_MODEL)
    outs = [[] for _ in range(8)]
    for l in range(depth):
        wts = _layer_weights(w_in[l], g_q[l], w_uq[l], w_uk[l], g_kv[l], w_uv[l], w_glu[l], b_glu[l],
                             w_out[l], ln1_g[l], ln1_b[l], w_gate[l], w_up[l], w_down[l], ln2_g[l], ln2_b[l])
        s5w = _s5_weights(ssm_a_re[l], ssm_a_im[l], ssm_log_step[l], ssm_b_re[l], ssm_b_im[l],
                          ssm_c_re[l], ssm_c_im[l], ssm_d[l])

        u, ckv, kr, ckvb, krb, ql_t, qr_t, v_t = _inproj(y_p, cos_p, sin_p, wts, tm=ATTN_TQ, prompt_layout=True)
        g, hl = _s5_prompt(u.reshape(bsz, seq, SSM_WIDTH), s5w)
        nk = seq // ATTN_TK
        y_att = _prompt_attention(ql_t, qr_t, ckvb.reshape(bsz, nk, ATTN_TK, KV_LORA),
                                  krb.reshape(bsz, nk, ATTN_TK, LANES), v_t.reshape(bsz, nk, KV_LORA, ATTN_TK),
                                  wts["w_uvT"], bsz, seq, tq=ATTN_TQ, tk=ATTN_TK)
        x1, x1b = _mix(g.reshape(bsz * seq, SSM_WIDTH), y_att, y_p, wts, alpha, tm=MIX_TM)
        y_p = _ffn(x1, x1b, wts, alpha, tm=FFN_TM, tf=FFN_TF)
        h_last = hl[:, :, SUBLANES - 1, :]
        outs[0].append(ckv.reshape(bsz, seq, KV_LORA))
        outs[1].append(kr.reshape(bsz, seq, ROPE_DIM))
        outs[2].append(h_last[:, :, :SLAB_STATE].reshape(bsz, SSM_GROUPS, SSM_STATE))
        outs[3].append(h_last[:, :, SLAB_STATE:].reshape(bsz, SSM_GROUPS, SSM_STATE))

        u, ckv, kr, _, _, ql, qr = _inproj(y_s, cos_s, sin_s, wts, tm=ROW_TM)
        g, hl_re, hl_im = _s5_sample(
            u, state_ssm_re[l].reshape(dbsz, SSM_GROUPS * SSM_STATE),
            state_ssm_im[l].reshape(dbsz, SSM_GROUPS * SSM_STATE), s5w, dseq)
        per_batch = lambda q: q.reshape(N_HEADS, dbsz, dseq, q.shape[-1]).transpose(1, 0, 2, 3).reshape(
            dbsz, N_HEADS * dseq, q.shape[-1])
        o_lat = _sample_attention(page_table, per_batch(ql), per_batch(qr), ckv, kr,
                                  cache_ckv[l], jnp.swapaxes(cache_krope[l], 1, 2), dseq,
                                  chunk=SAMPLE_PAGES_PER_STEP)
        y_att = _value_up(o_lat, wts["w_uv"])
        x1, x1b = _mix(g, y_att, y_s, wts, alpha, tm=MIX_TM)
        y_s = _ffn(x1, x1b, wts, alpha, tm=FFN_TM, tf=FFN_TF)
        outs[4].append(ckv.reshape(dbsz, dseq, KV_LORA))
        outs[5].append(kr.reshape(dbsz, dseq, ROPE_DIM))
        outs[6].append(hl_re.reshape(dbsz, SSM_GROUPS, SSM_STATE))
        outs[7].append(hl_im.reshape(dbsz, SSM_GROUPS, SSM_STATE))

    return (y_p.reshape(bsz, seq, D_MODEL), y_s.reshape(dbsz, dseq, D_MODEL),
            *[jnp.stack(o) for o in outs])
```

```python
import functools
import math

import jax
import jax.numpy as jnp
from jax import lax
from jax.experimental import pallas as pl
from jax.experimental.pallas import tpu as pltpu

F32 = jnp.float32
BF16 = jnp.bfloat16

D_MODEL = 2048
SSM_WIDTH = 1024
SSM_CH = 16
SSM_GROUPS = 64
SSM_STATE = 64
N_HEADS = 8
NOPE_DIM = 128
ROPE_DIM = 64
V_DIM = 128
Q_LORA = 512
KV_LORA = 512
ROPE_BASE = 10000.0
SCALE = (NOPE_DIM + ROPE_DIM) ** -0.5
PAGE_SIZE = 128
LANES = 128
SUBLANES = 8
SLAB_GROUPS = LANES // SSM_CH
N_SLABS = SSM_GROUPS // SLAB_GROUPS
SLAB_STATE = SLAB_GROUPS * SSM_STATE
VMEM_LIMIT = 56 * 1024 * 1024
ATTN_TQ = 256
ATTN_TK = 512
ATTN_HEAD_GROUP = 2
ATTN_LOOKAHEAD = 2
SAMPLE_PAGES_PER_STEP = 32
SAMPLE_ROWS_PER_STEP = 2
SAMPLE_LOOKAHEAD = 3
SAMPLE_PREFETCH = 3
ROW_TM = 256
MIX_TM = 512
MIX_PARTS = 2
FFN_TM = 512
FFN_TF = 512


def _dot(a, b):
    return jnp.dot(a, b, preferred_element_type=F32)


def _dot_nt(a, b):
    return lax.dot_general(a, b, (((1,), (1,)), ((), ())), preferred_element_type=F32)


def _const_spec(shape):
    n = len(shape)
    return pl.BlockSpec(shape, lambda *_: (0,) * n, pipeline_mode=pl.Buffered(1))


def _params(sem):
    return pltpu.CompilerParams(dimension_semantics=sem, vmem_limit_bytes=VMEM_LIMIT)


def _rms(x, g, eps=1e-6):
    return x * lax.rsqrt(jnp.mean(x * x, axis=-1, keepdims=True) + eps) * g


def _layer_norm(x, g, b, eps=1e-5):
    mu = jnp.mean(x, axis=-1, keepdims=True)
    xc = x - mu
    var = jnp.mean(xc * xc, axis=-1, keepdims=True)
    return xc * lax.rsqrt(var + eps) * g + b


def _rope_table_kernel(inv_ref, sgn_ref, cos_ref, sin_ref, *, pos_offset, pos_period):
    row = lax.broadcasted_iota(jnp.int32, cos_ref.shape, 0)
    ang = (pos_offset + (row & (pos_period - 1))).astype(F32) * inv_ref[...]
    cos_ref[...] = jnp.cos(ang)
    sin_ref[...] = jnp.sin(ang) * sgn_ref[...]


def _rope_table(rope_rows, n_rows, pos_offset, pos_period):
    assert pos_period & (pos_period - 1) == 0
    shape = jax.ShapeDtypeStruct((n_rows, LANES), F32)
    return pl.pallas_call(
        functools.partial(_rope_table_kernel, pos_offset=pos_offset, pos_period=pos_period),
        out_shape=[shape, shape], name="rope_table")(*rope_rows)


def _inproj_kernel(x_ref, wm_ref, wkr_ref, gq_ref, gkv_ref, cos_ref, sin_ref,
                   wqn_ref, wqr_ref, wqs_ref, wuk_ref,
                   u_ref, ckv_ref, kr_ref, ckvb_ref, krb_ref, ql_ref, qr_ref, *maybe_vt_ref, q_transposed):
    xb = x_ref[...].astype(BF16)
    cos = cos_ref[...]
    sin = sin_ref[...]
    proj = _dot(xb, wm_ref[...])
    u_ref[...] = proj[:, :SSM_WIDTH]
    cq = proj[:, SSM_WIDTH:SSM_WIDTH + Q_LORA]
    ckv = _rms(proj[:, SSM_WIDTH + Q_LORA:], gkv_ref[...])
    ckv_ref[...] = ckv
    ckvb_ref[...] = ckv.astype(BF16)
    kr2 = _dot(xb, wkr_ref[...])
    krope = kr2[:, :LANES] * cos + kr2[:, LANES:] * sin
    kr_ref[...] = krope[:, :ROPE_DIM]
    krb_ref[...] = krope.astype(BF16)
    cqn = _rms(cq, gq_ref[...])
    if q_transposed:
        maybe_vt_ref[0][...] = ckv.T.astype(BF16)
        cqt = cqn.T.astype(BF16)
        qn = _dot(wqn_ref[...], cqt)
        qa = _dot(wqr_ref[...], cqt)
        qs = _dot(wqs_ref[...], cqt)
        cos_t = cos.T[:ROPE_DIM]
        sin_t = sin.T[:ROPE_DIM]
        tm = cos.shape[0]
        for h in range(N_HEADS):
            sl = slice(h * LANES, (h + 1) * LANES)
            rl = slice(h * ROPE_DIM, (h + 1) * ROPE_DIM)
            cols = slice(h * tm, (h + 1) * tm)
            ql_ref[:, cols] = _dot(wuk_ref[h], qn[sl].astype(BF16)).astype(BF16)
            qr_ref[:ROPE_DIM, cols] = (qa[rl] * cos_t + qs[rl] * sin_t).astype(BF16)
            qr_ref[ROPE_DIM:, cols] = jnp.zeros((LANES - ROPE_DIM, tm), BF16)
    else:
        cqb = cqn.astype(BF16)
        qn = _dot(cqb, wqn_ref[...])
        qa = _dot(cqb, wqr_ref[...])
        qs = _dot(cqb, wqs_ref[...])
        for h in range(N_HEADS):
            sl = slice(h * LANES, (h + 1) * LANES)
            ql_ref[h] = _dot(qn[:, sl].astype(BF16), wuk_ref[h]).astype(BF16)
            qr_ref[h] = (qa[:, sl] * cos + qs[:, sl] * sin).astype(BF16)


def _inproj(x, cos, sin, wts, tm, prompt_layout=False):
    t = x.shape[0]
    table_tiles = cos.shape[0] // tm
    table = pl.BlockSpec((tm, LANES), lambda i: (i % table_tiles, 0))
    row = lambda w: pl.BlockSpec((tm, w), lambda i: (i, 0))
    extra_specs, extra_shapes = [], []
    if prompt_layout:
        assert tm == ATTN_TQ and ATTN_TK % tm == 0
        qspec = lambda w: pl.BlockSpec((None, w, N_HEADS * tm), lambda i: (i, 0, 0))
        qshape = lambda w: jax.ShapeDtypeStruct((t // tm, w, N_HEADS * tm), BF16)
        wq = [wts["w_qnT"], wts["w_qrT"], wts["w_qsT"], wts["w_uk"]]
        per_tk = ATTN_TK // tm
        extra_specs = [pl.BlockSpec((None, KV_LORA, tm), lambda i: (i // per_tk, 0, i % per_tk))]
        extra_shapes = [jax.ShapeDtypeStruct((t // ATTN_TK, KV_LORA, ATTN_TK), BF16)]
    else:
        qspec = lambda w: pl.BlockSpec((N_HEADS, tm, w), lambda i: (0, i, 0))
        qshape = lambda w: jax.ShapeDtypeStruct((N_HEADS, t, w), BF16)
        wq = [wts["w_qn"], wts["w_qr"], wts["w_qs"], wts["w_ukT"]]
    return pl.pallas_call(
        functools.partial(_inproj_kernel, q_transposed=prompt_layout),
        grid=(t // tm,),
        in_specs=[row(D_MODEL),
                  _const_spec((D_MODEL, 2048)), _const_spec((D_MODEL, 2 * LANES)),
                  _const_spec((1, Q_LORA)), _const_spec((1, KV_LORA)),
                  table, table] + [_const_spec(w.shape) for w in wq],
        out_specs=[row(SSM_WIDTH), row(KV_LORA), row(ROPE_DIM), row(KV_LORA), row(LANES),
                   qspec(KV_LORA), qspec(LANES)] + extra_specs,
        out_shape=[jax.ShapeDtypeStruct((t, SSM_WIDTH), F32),
                   jax.ShapeDtypeStruct((t, KV_LORA), F32),
                   jax.ShapeDtypeStruct((t, ROPE_DIM), F32),
                   jax.ShapeDtypeStruct((t, KV_LORA), BF16),
                   jax.ShapeDtypeStruct((t, LANES), BF16),
                   qshape(KV_LORA), qshape(LANES)] + extra_shapes,
        compiler_params=_params(("parallel",)),
        name="inproj",
    )(x, wts["w_main"], wts["w_kr"], wts["g_q"], wts["g_kv"], cos, sin, *wq)


def _s5prep_kernel(are_ref, aim_ref, ls_ref, bre_ref, bim_ref,
                   abr_ref, abi_ref, bbr_ref, bbi_ref):
    lr = are_ref[...]
    li = aim_ref[...]
    delta = jnp.exp(ls_ref[...])
    mag = jnp.exp(lr * delta)
    ar = mag * jnp.cos(li * delta)
    ai = mag * jnp.sin(li * delta)
    abr_ref[...] = ar
    abi_ref[...] = ai
    den = lr * lr + li * li
    fr = ((ar - 1.0) * lr + ai * li) / den
    fi = (ai * lr - (ar - 1.0) * li) / den
    br = bre_ref[...]
    bi = bim_ref[...]
    bbr_ref[...] = fr * br - fi * bi
    bbi_ref[...] = fr * bi + fi * br


def _s5prep(a_re, a_im, log_step, bt_re, bt_im):
    shp = jax.ShapeDtypeStruct((SSM_GROUPS * SSM_CH, SSM_STATE), F32)
    return pl.pallas_call(_s5prep_kernel, out_shape=[shp] * 4, name="s5prep")(
        a_re, a_im, log_step, bt_re, bt_im)


def _cmul_add(ar, ai, hr, hi, br, bi):
    return ar * hr - ai * hi + br, ar * hi + ai * hr + bi


def _s5p_kernel(u_ref, wb_ref, wc_ref, a_ref, d_ref, g_ref, hl_ref, bu_scr, e_scr, u_scr, g_scr,
                *, seg_len, chunk):
    n_rows = SUBLANES * seg_len
    n_chunks = n_rows // chunk
    steps = chunk // SUBLANES
    half = SLAB_STATE
    wb = wb_ref[...]
    wc = wc_ref[...]
    d = d_ref[...]
    ar = jnp.broadcast_to(a_ref[:, :half], (SUBLANES, half))
    ai = jnp.broadcast_to(a_ref[:, half:], (SUBLANES, half))

    for seg in range(SUBLANES):
        u_scr[pl.ds(seg, seg_len, stride=SUBLANES), :] = u_ref[pl.ds(seg * seg_len, seg_len), :]

    def project_in(c):
        rows = pl.ds(c * chunk, chunk)
        bu_scr[rows, :] = _dot(u_scr[rows, :].astype(BF16), wb)

    def project_out(c):
        rows = pl.ds(c * chunk, chunk)
        y = _dot(bu_scr[rows, :].astype(BF16), wc) + d * u_scr[rows, :]
        g_scr[rows, :] = jax.nn.gelu(y)
        for seg in range(SUBLANES):
            g_ref[pl.ds(seg * seg_len + c * steps, steps), :] = (
                g_scr[pl.ds(c * chunk + seg, steps, stride=SUBLANES), :])

    def scan_chunk(c, carry, store):
        hr, hi = carry
        for t in range(c * steps, (c + 1) * steps):
            rows = pl.ds(t * SUBLANES, SUBLANES)
            hr, hi = _cmul_add(ar, ai, hr, hi, bu_scr[rows, :half], bu_scr[rows, half:])
            if store:
                bu_scr[rows, :half] = hr
                bu_scr[rows, half:] = hi
        return hr, hi

    zero = jnp.zeros((SUBLANES, half), F32)
    carry = (zero, zero)
    project_in(0)
    for c in range(n_chunks):
        if c + 1 < n_chunks:
            project_in(c + 1)
        carry = scan_chunk(c, carry, store=False)
    er, ei = carry
    pr, pi = ar, ai
    for _ in range(int(math.log2(seg_len))):
        pr, pi = pr * pr - pi * pi, 2.0 * pr * pi
    e_scr[0:SUBLANES, :half] = er
    e_scr[0:SUBLANES, half:] = ei
    cr = jnp.zeros((1, half), F32)
    ci = jnp.zeros((1, half), F32)
    e_scr[SUBLANES:SUBLANES + 1, :half] = cr
    e_scr[SUBLANES:SUBLANES + 1, half:] = ci
    for s in range(SUBLANES - 1):
        cr, ci = _cmul_add(pr[0:1], pi[0:1], cr, ci, e_scr[s:s + 1, :half], e_scr[s:s + 1, half:])
        e_scr[SUBLANES + s + 1:SUBLANES + s + 2, :half] = cr
        e_scr[SUBLANES + s + 1:SUBLANES + s + 2, half:] = ci
    carry = (e_scr[SUBLANES:2 * SUBLANES, :half], e_scr[SUBLANES:2 * SUBLANES, half:])
    for c in range(n_chunks):
        if c > 0:
            project_out(c - 1)
        carry = scan_chunk(c, carry, store=True)
    project_out(n_chunks - 1)
    hr, hi = carry
    hl_ref[:, :half] = hr
    hl_ref[:, half:] = hi


def _s5_prompt(u, s5w):
    bsz, n_rows, _ = u.shape
    seg_len = n_rows // SUBLANES
    kern = functools.partial(_s5p_kernel, seg_len=seg_len, chunk=512)
    slab = lambda shape: pl.BlockSpec((None,) + shape, lambda b, s: (s, 0, 0))
    return pl.pallas_call(
        kern,
        grid=(bsz, N_SLABS),
        in_specs=[pl.BlockSpec((None, n_rows, LANES), lambda b, s: (b, 0, s)),
                  slab((LANES, 2 * SLAB_STATE)), slab((2 * SLAB_STATE, LANES)),
                  slab((1, 2 * SLAB_STATE)), slab((1, LANES))],
        out_specs=[pl.BlockSpec((None, n_rows, LANES), lambda b, s: (b, 0, s)),
                   pl.BlockSpec((None, None, SUBLANES, 2 * SLAB_STATE), lambda b, s: (b, s, 0, 0))],
        out_shape=[jax.ShapeDtypeStruct((bsz, n_rows, SSM_WIDTH), F32),
                   jax.ShapeDtypeStruct((bsz, N_SLABS, SUBLANES, 2 * SLAB_STATE), F32)],
        scratch_shapes=[pltpu.VMEM((n_rows, 2 * SLAB_STATE), F32),
                        pltpu.VMEM((2 * SUBLANES, 2 * SLAB_STATE), F32),
                        pltpu.VMEM((n_rows, LANES), F32), pltpu.VMEM((n_rows, LANES), F32)],
        compiler_params=_params(("parallel", "parallel")),
        name="s5_prompt",
    )(u, s5w["wb"], s5w["wc"], s5w["a"], s5w["d"])


def _s5s_kernel(u_ref, wb_ref, wc_ref, a_ref, d_ref, h0r_ref, h0i_ref,
                g_ref, hlr_ref, hli_ref, h_scr, u_scr, *, n_steps, bsz):
    half = SLAB_STATE
    for t in range(n_steps):
        u_scr[pl.ds(t * bsz, bsz), :] = u_ref[pl.ds(t, bsz, stride=n_steps), :]
    u = u_scr[...]
    h_scr[...] = _dot(u.astype(BF16), wb_ref[...])
    ar = a_ref[:, :half]
    ai = a_ref[:, half:]
    hr = h0r_ref[...]
    hi = h0i_ref[...]
    for t in range(n_steps):
        rows = pl.ds(t * bsz, bsz)
        hr, hi = _cmul_add(ar, ai, hr, hi, h_scr[rows, :half], h_scr[rows, half:])
        h_scr[rows, :half] = hr
        h_scr[rows, half:] = hi
    hlr_ref[...] = hr
    hli_ref[...] = hi
    y = _dot(h_scr[...].astype(BF16), wc_ref[...]) + d_ref[...] * u
    u_scr[...] = jax.nn.gelu(y)
    for t in range(n_steps):
        g_ref[pl.ds(t, bsz, stride=n_steps), :] = u_scr[pl.ds(t * bsz, bsz), :]


def _s5_sample(u, h0_re, h0_im, s5w, n_steps):
    n_rows = u.shape[0]
    bsz = n_rows // n_steps
    kern = functools.partial(_s5s_kernel, n_steps=n_steps, bsz=bsz)
    slab = lambda shape: pl.BlockSpec((None,) + shape, lambda s: (s, 0, 0))
    col = lambda r, w: pl.BlockSpec((r, w), lambda s: (0, s))
    return pl.pallas_call(
        kern,
        grid=(N_SLABS,),
        in_specs=[col(n_rows, LANES),
                  slab((LANES, 2 * SLAB_STATE)), slab((2 * SLAB_STATE, LANES)),
                  slab((1, 2 * SLAB_STATE)), slab((1, LANES)),
                  col(bsz, SLAB_STATE), col(bsz, SLAB_STATE)],
        out_specs=[col(n_rows, LANES), col(bsz, SLAB_STATE), col(bsz, SLAB_STATE)],
        out_shape=[jax.ShapeDtypeStruct((n_rows, SSM_WIDTH), F32),
                   jax.ShapeDtypeStruct((bsz, SSM_GROUPS * SSM_STATE), F32),
                   jax.ShapeDtypeStruct((bsz, SSM_GROUPS * SSM_STATE), F32)],
        scratch_shapes=[pltpu.VMEM((n_rows, 2 * SLAB_STATE), F32), pltpu.VMEM((n_rows, LANES), F32)],
        compiler_params=_params(("parallel",)),
        name="s5_sample",
    )(u, s5w["wb"], s5w["wc"], s5w["a"], s5w["d"], h0_re, h0_im)


def _pattn_kernel(ql_ref, qr_ref, kc_ref, kr_ref, vt_ref, wuv_ref, o_ref, m_scr, l_scr, acc_scr, *, tq, tk):
    i = pl.program_id(1)
    gw = ATTN_HEAD_GROUP * tq
    n_groups = N_HEADS // ATTN_HEAD_GROUP
    m_scr[...] = jnp.full(m_scr.shape, -jnp.inf, F32)
    l_scr[...] = jnp.zeros(l_scr.shape, F32)
    acc_scr[...] = jnp.zeros(acc_scr.shape, F32)

    def block(j, lo=0, n=tk, diagonal=False):
        kc = kc_ref[j, lo:lo + n, :]
        kr = kr_ref[j, lo:lo + n, :]
        vt = vt_ref[j, :, lo:lo + n]
        if diagonal:
            keep = (lax.broadcasted_iota(jnp.int32, (n, gw), 0)
                    <= (lax.broadcasted_iota(jnp.int32, (n, gw), 1) & (tq - 1)))

        def scores(g):
            cols = slice(g * gw, (g + 1) * gw)
            s = (_dot(kc, ql_ref[:, cols]) + _dot(kr, qr_ref[:, cols])) * SCALE
            return jnp.where(keep, s, -jnp.inf) if diagonal else s

        pending = [scores(g) for g in range(ATTN_LOOKAHEAD)]
        for g in range(n_groups):
            s = pending.pop(0)
            if g + ATTN_LOOKAHEAD < n_groups:
                pending.append(scores(g + ATTN_LOOKAHEAD))
            cols = slice(g * gw, (g + 1) * gw)
            m_prev = m_scr[:, cols]
            m_new = jnp.maximum(m_prev, jnp.max(s, axis=0, keepdims=True))
            corr = jnp.exp(m_prev - m_new)
            p = jnp.exp(s - m_new)
            l_scr[:, cols] = l_scr[:, cols] * corr + jnp.sum(p, axis=0, keepdims=True)
            acc_scr[:, cols] = acc_scr[:, cols] * corr + _dot(vt, p.astype(BF16))
            m_scr[:, cols] = m_new

    def body(j, carry):
        block(j)
        return carry

    j_last = (i * tq) // tk
    lax.fori_loop(0, j_last, body, 0)

    @pl.when(i % 2 == 0)
    def _():
        block(j_last, 0, tq, diagonal=True)

    @pl.when(i % 2 == 1)
    def _():
        block(j_last, 0, tq)
        block(j_last, tq, tq, diagonal=True)

    for h in range(N_HEADS):
        cols = slice(h * tq, (h + 1) * tq)
        y_t = _dot(wuv_ref[h], acc_scr[:, cols].astype(BF16)) / l_scr[:, cols]
        o_ref[:, h * V_DIM:(h + 1) * V_DIM] = y_t.T.astype(o_ref.dtype)


def _prompt_attention(ql_t, qr_t, kc, kr, v_t, wuv_t, bsz, seq, tq, tk):
    nq = seq // tq
    nk = seq // tk
    assert tk == 2 * tq and tq & (tq - 1) == 0 and N_HEADS % ATTN_HEAD_GROUP == 0
    kern = functools.partial(_pattn_kernel, tq=tq, tk=tk)
    per_batch = lambda r, c: pl.BlockSpec((None, nk, r, c), lambda b, i: (b, 0, 0, 0))
    return pl.pallas_call(
        kern,
        grid=(bsz, nq),
        in_specs=[pl.BlockSpec((None, KV_LORA, N_HEADS * tq), lambda b, i: (b * nq + i, 0, 0)),
                  pl.BlockSpec((None, LANES, N_HEADS * tq), lambda b, i: (b * nq + i, 0, 0)),
                  per_batch(tk, KV_LORA), per_batch(tk, LANES), per_batch(KV_LORA, tk),
                  _const_spec((N_HEADS, V_DIM, KV_LORA))],
        out_specs=pl.BlockSpec((tq, N_HEADS * V_DIM), lambda b, i: (b * nq + i, 0)),
        out_shape=jax.ShapeDtypeStruct((bsz * seq, N_HEADS * V_DIM), BF16),
        scratch_shapes=[pltpu.VMEM((1, N_HEADS * tq), F32), pltpu.VMEM((1, N_HEADS * tq), F32),
                        pltpu.VMEM((KV_LORA, N_HEADS * tq), F32)],
        compiler_params=_params(("parallel", "arbitrary")),
        name="prompt_attention",
    )(ql_t, qr_t, kc, kr, v_t, wuv_t)


N_CHAINS = 4


def _sattn_kernel(pt_ref, ql_ref, qr_ref, kn_ref, krn_ref, ck_hbm, krt_hbm, o_ref,
                  kbuf, rbuf, sems, m_scr, l_scr, acc_scr, kc_scr, *, n_new, n_pages, chunk, rows_per_step):
    g = pl.program_id(0)
    n_rows = N_HEADS * n_new
    row_chunks = n_pages // chunk
    n_slots = rows_per_step * row_chunks

    def page_copies(pid, slot, k):
        rows = pl.ds(k * PAGE_SIZE, PAGE_SIZE)
        return (pltpu.make_async_copy(ck_hbm.at[pid], kbuf.at[slot, rows, :], sems.at[0, slot]),
                pltpu.make_async_copy(krt_hbm.at[pid], rbuf.at[slot, k], sems.at[1, slot]))

    def start_chunk(step, jj):
        row = step * rows_per_step + jj // row_chunks
        for k in range(chunk):
            for cp in page_copies(pt_ref[row, (jj % row_chunks) * chunk + k], jj, k):
                cp.start(priority=k % 2)

    def wait_chunk(jj):
        for k in range(chunk):
            for cp in page_copies(0, jj, k):
                cp.wait()

    def start_ahead(jj):
        ahead = jj + SAMPLE_PREFETCH
        if ahead < n_slots:
            start_chunk(g, ahead)
        else:
            @pl.when(g + 1 < pl.num_programs(0))
            def _():
                start_chunk(g + 1, ahead - n_slots)

    @pl.when(g == 0)
    def _():
        for jj in range(SAMPLE_PREFETCH):
            start_chunk(0, jj)

    for r in range(rows_per_step):
        _sattn_row(r, ql_ref, qr_ref, kn_ref, krn_ref, o_ref, kbuf, rbuf, m_scr, l_scr, acc_scr, kc_scr,
                   start_ahead, wait_chunk, n_new=n_new, chunk=chunk, row_chunks=row_chunks)


def _sattn_row(r, ql_ref, qr_ref, kn_ref, krn_ref, o_ref, kbuf, rbuf, m_scr, l_scr, acc_scr, kc_scr,
               start_ahead, wait_chunk, *, n_new, chunk, row_chunks):
    n_rows = N_HEADS * n_new
    n_chunks = row_chunks
    new_rows = pl.ds(r * n_new, n_new)
    ql = ql_ref[r]
    qr = qr_ref[r][:, :ROPE_DIM]

    kc = kn_ref[new_rows, :].astype(BF16)
    s = (_dot_nt(ql, kc) + _dot_nt(qr, krn_ref[new_rows, :].astype(BF16))) * SCALE
    q_tok = lax.broadcasted_iota(jnp.int32, (n_rows, n_new), 0) & (n_new - 1)
    k_tok = lax.broadcasted_iota(jnp.int32, (n_rows, n_new), 1)
    s = jnp.where(k_tok <= q_tok, s, -jnp.inf)
    m = jnp.max(s, axis=-1, keepdims=True)
    p = jnp.exp(s - m)
    m_scr[0] = m
    l_scr[0] = jnp.sum(p, axis=-1, keepdims=True)
    acc_scr[0] = _dot(p.astype(BF16), kc)
    for c in range(1, N_CHAINS):
        m_scr[c] = jnp.full((n_rows, 1), -jnp.inf, F32)
        l_scr[c] = jnp.zeros((n_rows, 1), F32)
        acc_scr[c] = jnp.zeros((n_rows, KV_LORA), F32)

    per_chain = chunk // N_CHAINS
    chain_rows = per_chain * PAGE_SIZE

    def scores(t):
        j, c = divmod(t, N_CHAINS)
        j += r * row_chunks
        if c == 0:
            start_ahead(j)
            wait_chunk(j)
        rows = slice(c * chain_rows, (c + 1) * chain_rows)
        kc_scr[rows, :] = kbuf[j, rows, :].astype(BF16)
        kc = kc_scr[rows, :]
        kr_t = jnp.concatenate([rbuf[j, k] for k in range(c * per_chain, (c + 1) * per_chain)],
                               axis=1).astype(BF16)
        return (_dot_nt(ql, kc) + _dot(qr, kr_t)) * SCALE, kc

    n_blocks = n_chunks * N_CHAINS
    pending = [scores(t) for t in range(SAMPLE_LOOKAHEAD)]
    for t in range(n_blocks):
        s, kc = pending.pop(0)
        if t + SAMPLE_LOOKAHEAD < n_blocks:
            pending.append(scores(t + SAMPLE_LOOKAHEAD))
        c = t % N_CHAINS
        m_prev = m_scr[c]
        m_new = jnp.maximum(m_prev, jnp.max(s, axis=-1, keepdims=True))
        corr = jnp.exp(m_prev - m_new)
        p = jnp.exp(s - m_new)
        l_scr[c] = l_scr[c] * corr + jnp.sum(p, axis=-1, keepdims=True)
        acc_scr[c] = acc_scr[c] * corr + _dot(p.astype(BF16), kc)
        m_scr[c] = m_new

    m = m_scr[0]
    for c in range(1, N_CHAINS):
        m = jnp.maximum(m, m_scr[c])
    l = jnp.zeros((n_rows, 1), F32)
    acc = jnp.zeros((n_rows, KV_LORA), F32)
    for c in range(N_CHAINS):
        w = jnp.exp(m_scr[c] - m)
        l = l + l_scr[c] * w
        acc = acc + acc_scr[c] * w
    o = acc / l
    for h in range(N_HEADS):
        o_ref[h, new_rows, :] = o[h * n_new:(h + 1) * n_new]


def _value_up_kernel(o_ref, wuv_ref, y_ref):
    y_ref[...] = _dot(o_ref[...].astype(BF16), wuv_ref[...])


def _value_up(o_lat, wuv):
    _, t, _ = o_lat.shape
    return pl.pallas_call(
        _value_up_kernel,
        grid=(N_HEADS,),
        in_specs=[pl.BlockSpec((None, t, KV_LORA), lambda h: (h, 0, 0)),
                  pl.BlockSpec((None, KV_LORA, V_DIM), lambda h: (h, 0, 0))],
        out_specs=pl.BlockSpec((t, V_DIM), lambda h: (0, h)),
        out_shape=jax.ShapeDtypeStruct((t, N_HEADS * V_DIM), F32),
        compiler_params=_params(("parallel",)),
        name="value_up",
    )(o_lat, wuv)


def _sample_attention(page_table, ql, qr, kc_new, kr_new, cache_ckv, cache_kr_t, n_new, chunk):
    bsz, n_pages = page_table.shape
    n_rows = N_HEADS * n_new
    rps = SAMPLE_ROWS_PER_STEP
    n_chunks = rps * (n_pages // chunk)
    assert n_pages % chunk == 0 and chunk % N_CHAINS == 0 and bsz % rps == 0
    assert 0 < SAMPLE_PREFETCH < n_chunks
    kern = functools.partial(_sattn_kernel, n_new=n_new, n_pages=n_pages, chunk=chunk, rows_per_step=rps)
    in_specs = [pl.BlockSpec((rps, n_rows, KV_LORA), lambda b, pt: (b, 0, 0)),
                pl.BlockSpec((rps, n_rows, LANES), lambda b, pt: (b, 0, 0)),
                pl.BlockSpec((rps * n_new, KV_LORA), lambda b, pt: (b, 0)),
                pl.BlockSpec((rps * n_new, ROPE_DIM), lambda b, pt: (b, 0)),
                pl.BlockSpec(memory_space=pl.ANY), pl.BlockSpec(memory_space=pl.ANY)]
    return pl.pallas_call(
        kern,
        grid_spec=pltpu.PrefetchScalarGridSpec(
            num_scalar_prefetch=1,
            grid=(bsz // rps,),
            in_specs=in_specs,
            out_specs=pl.BlockSpec((N_HEADS, rps * n_new, KV_LORA), lambda b, pt: (0, b, 0)),
            scratch_shapes=[pltpu.VMEM((n_chunks, chunk * PAGE_SIZE, KV_LORA), F32),
                            pltpu.VMEM((n_chunks, chunk, ROPE_DIM, PAGE_SIZE), F32),
                            pltpu.SemaphoreType.DMA((2, n_chunks)),
                            pltpu.VMEM((N_CHAINS, n_rows, 1), F32), pltpu.VMEM((N_CHAINS, n_rows, 1), F32),
                            pltpu.VMEM((N_CHAINS, n_rows, KV_LORA), F32),
                            pltpu.VMEM((chunk * PAGE_SIZE, KV_LORA), BF16)]),
        out_shape=jax.ShapeDtypeStruct((N_HEADS, bsz * n_new, KV_LORA), F32),
        compiler_params=_params(("arbitrary",)),
        name="sample_attention",
    )(page_table, ql, qr, kc_new, kr_new, cache_ckv, cache_kr_t)


def _mix_kernel(g_ref, ya_ref, x_ref, wglu_ref, bglu_ref, wout_ref, lg_ref, lb_ref, o_ref, ob_ref, *, alpha):
    part = g_ref.shape[0] // MIX_PARTS
    rows = [pl.ds(r * part, part) for r in range(MIX_PARTS)]

    def gate(r):
        g = g_ref[rows[r], :]
        return g, _dot(g.astype(BF16), wglu_ref[...])

    def glu(g, z):
        return (g * jax.nn.sigmoid(z + bglu_ref[...])).astype(BF16)

    def project(r, y):
        return (_dot(y, wout_ref[:SSM_WIDTH, :])
                + _dot(ya_ref[rows[r], :].astype(BF16), wout_ref[SSM_WIDTH:, :]))

    def finish(r, mix):
        x1 = _layer_norm(alpha * x_ref[rows[r], :] + mix, lg_ref[...], lb_ref[...])
        o_ref[rows[r], :] = x1
        ob_ref[rows[r], :] = x1.astype(BF16)

    gated = [gate(r) for r in range(MIX_PARTS)]
    mixes = []
    for r in range(MIX_PARTS):
        mixes.append(project(r, glu(*gated[r])))
        if r > 0:
            finish(r - 1, mixes[r - 1])
    finish(MIX_PARTS - 1, mixes[-1])


def _mix(g, y_att, x, wts, alpha, tm):
    t = x.shape[0]
    row = lambda w: pl.BlockSpec((tm, w), lambda i: (i, 0))
    return pl.pallas_call(
        functools.partial(_mix_kernel, alpha=alpha),
        grid=(t // tm,),
        in_specs=[row(SSM_WIDTH), row(N_HEADS * V_DIM), row(D_MODEL),
                  _const_spec((SSM_WIDTH, SSM_WIDTH)), _const_spec((1, SSM_WIDTH)),
                  _const_spec((D_MODEL, D_MODEL)), _const_spec((1, D_MODEL)), _const_spec((1, D_MODEL))],
        out_specs=[row(D_MODEL), row(D_MODEL)],
        out_shape=[jax.ShapeDtypeStruct((t, D_MODEL), F32), jax.ShapeDtypeStruct((t, D_MODEL), BF16)],
        compiler_params=_params(("parallel",)),
        name="mix",
    )(g, y_att, x, wts["w_glu"], wts["b_glu"], wts["w_out"], wts["ln1_g"], wts["ln1_b"])


def _ffn_kernel(x_ref, xb_ref, wg_ref, wu_ref, wd_ref, lg_ref, lb_ref, o_ref, *, alpha):
    f = pl.program_id(1)

    @pl.when(f == 0)
    def _():
        o_ref[...] = jnp.zeros(o_ref.shape, F32)

    xb = xb_ref[...]
    gate = _dot(xb, wg_ref[...])
    up = _dot(xb, wu_ref[...])
    o_ref[...] += _dot((jax.nn.silu(gate) * up).astype(BF16), wd_ref[...])

    @pl.when(f == pl.num_programs(1) - 1)
    def _():
        o_ref[...] = _layer_norm(alpha * x_ref[...] + o_ref[...], lg_ref[...], lb_ref[...])


def _ffn(x, xb, wts, alpha, tm, tf):
    t = x.shape[0]
    d_ff = wts["w_gate"].shape[1]
    return pl.pallas_call(
        functools.partial(_ffn_kernel, alpha=alpha),
        grid=(t // tm, d_ff // tf),
        in_specs=[pl.BlockSpec((tm, D_MODEL), lambda i, f: (i, 0)),
                  pl.BlockSpec((tm, D_MODEL), lambda i, f: (i, 0)),
                  pl.BlockSpec((D_MODEL, tf), lambda i, f: (0, f)),
                  pl.BlockSpec((D_MODEL, tf), lambda i, f: (0, f)),
                  pl.BlockSpec((tf, D_MODEL), lambda i, f: (f, 0)),
                  pl.BlockSpec((1, D_MODEL), lambda i, f: (0, 0)),
                  pl.BlockSpec((1, D_MODEL), lambda i, f: (0, 0))],
        out_specs=pl.BlockSpec((tm, D_MODEL), lambda i, f: (i, 0)),
        out_shape=jax.ShapeDtypeStruct((t, D_MODEL), F32),
        compiler_params=_params(("parallel", "arbitrary")),
        name="ffn",
    )(x, xb, wts["w_gate"], wts["w_up"], wts["w_down"], wts["ln2_g"], wts["ln2_b"])


def _swap_halves(w):
    half = w.shape[-1] // 2
    return jnp.concatenate([w[..., half:], w[..., :half]], axis=-1)


def _pad_lanes(w):
    return jnp.concatenate([w, jnp.zeros(w.shape[:-1] + (LANES - w.shape[-1],), w.dtype)], axis=-1)


def _layer_weights(w_in, g_q, w_uq, w_uk, g_kv, w_uv, w_glu, b_glu, w_out, ln1_g, ln1_b,
                   w_gate, w_up, w_down, ln2_g, ln2_b):
    n_main = SSM_WIDTH + Q_LORA + KV_LORA
    w_kr = w_in[:, n_main:]
    uq = w_uq.reshape(Q_LORA, N_HEADS, NOPE_DIM + ROPE_DIM)
    uq_rope = uq[:, :, NOPE_DIM:]
    w_qn = uq[:, :, :NOPE_DIM].reshape(Q_LORA, N_HEADS * NOPE_DIM).astype(BF16)
    w_qr = _pad_lanes(uq_rope).reshape(Q_LORA, N_HEADS * LANES).astype(BF16)
    w_qs = _pad_lanes(_swap_halves(uq_rope)).reshape(Q_LORA, N_HEADS * LANES).astype(BF16)
    return {
        "w_main": w_in[:, :n_main].astype(BF16),
        "w_kr": jnp.concatenate([_pad_lanes(w_kr), _pad_lanes(_swap_halves(w_kr))], axis=1).astype(BF16),
        "g_q": g_q.reshape(1, Q_LORA), "g_kv": g_kv.reshape(1, KV_LORA),
        "w_qn": w_qn, "w_qr": w_qr, "w_qs": w_qs,
        "w_qnT": w_qn.T,
        "w_qrT": uq_rope.reshape(Q_LORA, N_HEADS * ROPE_DIM).T.astype(BF16),
        "w_qsT": _swap_halves(uq_rope).reshape(Q_LORA, N_HEADS * ROPE_DIM).T.astype(BF16),
        "w_ukT": jnp.transpose(w_uk, (1, 2, 0)).astype(BF16),
        "w_uk": jnp.transpose(w_uk, (1, 0, 2)).astype(BF16),
        "w_uv": jnp.transpose(w_uv, (1, 0, 2)).astype(BF16),
        "w_uvT": jnp.transpose(w_uv, (1, 2, 0)).astype(BF16),
        "w_glu": w_glu.astype(BF16), "b_glu": b_glu.reshape(1, SSM_WIDTH),
        "w_out": w_out.astype(BF16),
        "ln1_g": ln1_g.reshape(1, D_MODEL), "ln1_b": ln1_b.reshape(1, D_MODEL),
        "w_gate": w_gate.astype(BF16), "w_up": w_up.astype(BF16), "w_down": w_down.astype(BF16),
        "ln2_g": ln2_g.reshape(1, D_MODEL), "ln2_b": ln2_b.reshape(1, D_MODEL),
    }


def _s5_weights(a_re, a_im, log_step, b_re, b_im, c_re, c_im, d):
    rep = lambda a: jnp.repeat(a, SSM_CH, axis=0)
    bt = lambda b: jnp.transpose(b, (0, 2, 1)).reshape(SSM_GROUPS * SSM_CH, SSM_STATE)
    abr, abi, bbr, bbi = _s5prep(rep(a_re), rep(a_im), rep(log_step.reshape(SSM_GROUPS, 1)),
                                 bt(b_re), bt(b_im))
    abr = abr[::SSM_CH]
    abi = abi[::SSM_CH]
    eye = jnp.eye(SLAB_GROUPS, dtype=F32)

    def blockdiag_in(b):
        b4 = b.reshape(N_SLABS, SLAB_GROUPS, SSM_CH, SSM_STATE)
        return jnp.einsum("sghp,gk->sghkp", b4, eye).reshape(N_SLABS, LANES, SLAB_STATE)

    def blockdiag_out(c):
        c4 = c.reshape(N_SLABS, SLAB_GROUPS, SSM_CH, SSM_STATE)
        return jnp.einsum("sghp,gk->skpgh", c4, eye).reshape(N_SLABS, SLAB_STATE, LANES)

    slab_row = lambda a: a.reshape(N_SLABS, 1, SLAB_STATE)
    return {
        "wb": jnp.concatenate([blockdiag_in(bbr), blockdiag_in(bbi)], axis=2).astype(BF16),
        "wc": jnp.concatenate([blockdiag_out(c_re), -blockdiag_out(c_im)], axis=1).astype(BF16),
        "a": jnp.concatenate([slab_row(abr), slab_row(abi)], axis=2),
        "d": d.reshape(N_SLABS, 1, LANES),
    }


def _rope_rows():
    half = ROPE_DIM // 2
    inv = ROPE_BASE ** (-2.0 * jnp.arange(half, dtype=F32) / ROPE_DIM)
    sign = jnp.concatenate([-jnp.ones((half,), F32), jnp.ones((half,), F32)])
    return (_pad_lanes(jnp.concatenate([inv, inv]).reshape(1, ROPE_DIM)),
            _pad_lanes(sign.reshape(1, ROPE_DIM)))


def kernel(x_prompt, x_sample, cache_ckv, cache_krope, state_ssm_re, state_ssm_im, page_table, w_in, g_q, w_uq, w_uk, g_kv, w_uv, ssm_a_re, ssm_a_im, ssm_log_step, ssm_b_re, ssm_b_im, ssm_c_re, ssm_c_im, ssm_d, w_glu, b_glu, w_out, ln1_g, ln1_b, w_gate, w_up, w_down, ln2_g, ln2_b):
    depth = w_in.shape[0]
    alpha = (2 * depth) ** 0.25
    bsz, seq, _ = x_prompt.shape
    dbsz, dseq, _ = x_sample.shape
    past = page_table.shape[1] * PAGE_SIZE
    rope_rows = _rope_rows()
    cos_p, sin_p = _rope_table(rope_rows, seq, 0, seq)
    cos_s, sin_s = _rope_table(rope_rows, ROW_TM, past, dseq)

    y_p = x_prompt.reshape(bsz * seq, D_MODEL)
    y_s = x_sample.reshape(dbsz * dseq, D_MODEL)
    outs = [[] for _ in range(8)]
    for l in range(depth):
        wts = _layer_weights(w_in[l], g_q[l], w_uq[l], w_uk[l], g_kv[l], w_uv[l], w_glu[l], b_glu[l],
                             w_out[l], ln1_g[l], ln1_b[l], w_gate[l], w_up[l], w_down[l], ln2_g[l], ln2_b[l])
        s5w = _s5_weights(ssm_a_re[l], ssm_a_im[l], ssm_log_step[l], ssm_b_re[l], ssm_b_im[l],
                          ssm_c_re[l], ssm_c_im[l], ssm_d[l])

        u, ckv, kr, ckvb, krb, ql_t, qr_t, v_t = _inproj(y_p, cos_p, sin_p, wts, tm=ATTN_TQ, prompt_layout=True)
        g, hl = _s5_prompt(u.reshape(bsz, seq, SSM_WIDTH), s5w)
        nk = seq // ATTN_TK
        y_att = _prompt_attention(ql_t, qr_t, ckvb.reshape(bsz, nk, ATTN_TK, KV_LORA),
                                  krb.reshape(bsz, nk, ATTN_TK, LANES), v_t.reshape(bsz, nk, KV_LORA, ATTN_TK),
                                  wts["w_uvT"], bsz, seq, tq=ATTN_TQ, tk=ATTN_TK)
        x1, x1b = _mix(g.reshape(bsz * seq, SSM_WIDTH), y_att, y_p, wts, alpha, tm=MIX_TM)
        y_p = _ffn(x1, x1b, wts, alpha, tm=FFN_TM, tf=FFN_TF)
        h_last = hl[:, :, SUBLANES - 1, :]
        outs[0].append(ckv.reshape(bsz, seq, KV_LORA))
        outs[1].append(kr.reshape(bsz, seq, ROPE_DIM))
        outs[2].append(h_last[:, :, :SLAB_STATE].reshape(bsz, SSM_GROUPS, SSM_STATE))
        outs[3].append(h_last[:, :, SLAB_STATE:].reshape(bsz, SSM_GROUPS, SSM_STATE))

        u, ckv, kr, _, _, ql, qr = _inproj(y_s, cos_s, sin_s, wts, tm=ROW_TM)
        g, hl_re, hl_im = _s5_sample(
            u, state_ssm_re[l].reshape(dbsz, SSM_GROUPS * SSM_STATE),
            state_ssm_im[l].reshape(dbsz, SSM_GROUPS * SSM_STATE), s5w, dseq)
        per_batch = lambda q: q.reshape(N_HEADS, dbsz, dseq, q.shape[-1]).transpose(1, 0, 2, 3).reshape(
            dbsz, N_HEADS * dseq, q.shape[-1])
        o_lat = _sample_attention(page_table, per_batch(ql), per_batch(qr), ckv, kr,
                                  cache_ckv[l], jnp.swapaxes(cache_krope[l], 1, 2), dseq,
                                  chunk=SAMPLE_PAGES_PER_STEP)
        y_att = _value_up(o_lat, wts["w_uv"])
        x1, x1b = _mix(g, y_att, y_s, wts, alpha, tm=MIX_TM)
        y_s = _ffn(x1, x1b, wts, alpha, tm=FFN_TM, tf=FFN_TF)
        outs[4].append(ckv.reshape(dbsz, dseq, KV_LORA))
        outs[5].append(kr.reshape(dbsz, dseq, ROPE_DIM))
        outs[6].append(hl_re.reshape(dbsz, SSM_GROUPS, SSM_STATE))
        outs[7].append(hl_im.reshape(dbsz, SSM_GROUPS, SSM_STATE))

    return (y_p.reshape(bsz, seq, D_MODEL), y_s.reshape(dbsz, dseq, D_MODEL),
            *[jnp.stack(o) for o in outs])
```
